```python
import math
import jax, jax.numpy as jnp
from jax import lax
import numpy as np

D_MODEL = 1024
BATCH = 4
SEQ = 4096
DEPTH = 1
DEC_BATCH = 8
DEC_SEQ = 16
PAST_LEN = 1024

CHUNK = 64
QBLOCK = 128
ROPE_THETA = 10000.0
LN_EPS = 1e-5
GLA_HEADS = 4
GLA_DK = D_MODEL // 2 // GLA_HEADS
GLA_DV = D_MODEL // GLA_HEADS
GLA_GATE_RANK = 16
GLA_TAU = 16.0
DSA_HEADS = 8
DSA_KV_HEADS = 2
DSA_HEAD_DIM = D_MODEL // DSA_HEADS
IDX_HEADS = 8
IDX_DIM = 64
TOPK_MAX = 256
IDX_W_SCALE = (IDX_HEADS ** -0.5) * (IDX_DIM ** -0.5)
N_GROUPS = 4
EXPERTS_PER_GROUP = 4
N_EXPERTS = N_GROUPS * EXPERTS_PER_GROUP
D_EXPERT = 256
TOP_K_IN_GROUP = 2
ALPHA = (2 * DEPTH) ** 0.25
BETA = (8 * DEPTH) ** -0.25

_SPLIT_SIZES = (GLA_HEADS * GLA_DK, GLA_HEADS * GLA_DK, GLA_HEADS * GLA_DV, GLA_HEADS * GLA_DV, GLA_GATE_RANK,
                DSA_HEADS * DSA_HEAD_DIM, DSA_KV_HEADS * DSA_HEAD_DIM, DSA_KV_HEADS * DSA_HEAD_DIM,
                IDX_HEADS * IDX_DIM, IDX_DIM, IDX_HEADS, D_MODEL, D_MODEL)
D_IN_PROJ = sum(_SPLIT_SIZES)
_SPLIT_POINTS = tuple(sum(_SPLIT_SIZES[:i + 1]) for i in range(len(_SPLIT_SIZES) - 1))

kernel_name = 'hybrid_gla_dsa_hmoe_stream_step'


def layer_norm(x, g, b):
    xf = x.astype(jnp.float32)
    mu = jnp.mean(xf, axis=-1, keepdims=True)
    var = jnp.mean(jnp.square(xf - mu), axis=-1, keepdims=True)
    return ((xf - mu) * lax.rsqrt(var + LN_EPS) * g + b).astype(x.dtype)


def rope(x, pos):
    half = x.shape[-1] // 2
    inv = ROPE_THETA ** (-jnp.arange(half, dtype=jnp.float32) / half)
    ang = pos.astype(jnp.float32)[:, None] * inv[None, :]
    cos = jnp.cos(ang)[:, None, :]
    sin = jnp.sin(ang)[:, None, :]
    xf = x.astype(jnp.float32)
    x1, x2 = xf[..., :half], xf[..., half:]
    return jnp.concatenate([x1 * cos - x2 * sin, x2 * cos + x1 * sin], axis=-1).astype(x.dtype)


def gla_recurrence(q, k, v, logf, s0):
    B, L, H, DK = q.shape
    c = min(CHUNK, L)
    n = L // c

    def to_chunks(a):
        return a.reshape(B, n, c, *a.shape[2:]).swapaxes(0, 1)

    causal = jnp.tril(jnp.ones((c, c), dtype=bool))[None, :, :, None, None]

    def step(s, blk):
        qc, kc, vc, fc = blk
        b = jnp.cumsum(fc, axis=1)
        o_inter = jnp.einsum('bihk,bhkv->bihv', qc * jnp.exp(b), s)
        decay = jnp.exp(jnp.where(causal, b[:, :, None] - b[:, None, :], -jnp.inf))
        a = jnp.einsum('bihk,bjhk,bijhk->bhij', qc, kc, decay)
        o_intra = jnp.einsum('bhij,bjhv->bihv', a, vc)
        b_last = b[:, -1]
        k_dec = kc * jnp.exp(b_last[:, None] - b)
        s_new = jnp.exp(b_last)[..., None] * s + jnp.einsum('bjhk,bjhv->bhkv', k_dec, vc)
        return s_new, o_inter + o_intra

    s_fin, o = lax.scan(step, s0, (to_chunks(q), to_chunks(k), to_chunks(v), to_chunks(logf)))
    o = o.swapaxes(0, 1).reshape(B, L, H, v.shape[-1])
    return o, s_fin


def dsa_block(q, qi, wi, tpos, k_all, v_all, ki_all, kpos, topk):
    B, T = q.shape[0], q.shape[1]
    score = jnp.einsum('bthd,bsd->bths', qi, ki_all)
    idx_score = jnp.einsum('bth,bths->bts', wi, jax.nn.relu(score)).astype(jnp.float32)
    limit = (tpos // CHUNK + 1) * CHUNK
    admissible = kpos[None, :] < limit[:, None]
    idx_score = jnp.where(admissible[None], idx_score, -jnp.inf)
    sel_val, sel = lax.top_k(idx_score, topk)
    gather = jax.vmap(lambda a, i: a[i])
    kg = gather(k_all, sel)
    vg = gather(v_all, sel)
    qg = q.reshape(B, T, DSA_KV_HEADS, DSA_HEADS // DSA_KV_HEADS, DSA_HEAD_DIM)
    logits = jnp.einsum('btgrd,btkgd->btgrk', qg, kg).astype(jnp.float32) * (DSA_HEAD_DIM ** -0.5)
    logits = jnp.where(jnp.isfinite(sel_val)[:, :, None, None, :], logits, -jnp.inf)
    p = jax.nn.softmax(logits, axis=-1).astype(vg.dtype)
    o = jnp.einsum('btgrk,btkgd->btgrd', p, vg)
    return o.reshape(B, T, DSA_HEADS * DSA_HEAD_DIM)


def token_mixer(x, gla_s0, past_k, past_v, past_ki, w_in, w_gla_gate_up, b_gla_gate, gla_norm_g,
                gla_norm_b, w_branch_gla, w_branch_dsa, w_out):
    B, T, _ = x.shape
    past_len = past_k.shape[1]
    pos = past_len + jnp.arange(T, dtype=jnp.int32)
    z = x @ w_in
    (gq, gk, gv, gg, gr, dq, dk, dv, iq, ik, iw, ga, gb) = jnp.split(z, _SPLIT_POINTS, axis=-1)
    f32 = jnp.float32
    q = gq.reshape(B, T, GLA_HEADS, GLA_DK).astype(f32) * (GLA_DK ** -0.5)
    k = gk.reshape(B, T, GLA_HEADS, GLA_DK).astype(f32)
    v = gv.reshape(B, T, GLA_HEADS, GLA_DV).astype(f32)
    logf = jax.nn.log_sigmoid((gr @ w_gla_gate_up + b_gla_gate).astype(f32)) / GLA_TAU
    logf = logf.reshape(B, T, GLA_HEADS, GLA_DK)
    o, s_fin = gla_recurrence(q, k, v, logf, gla_s0.astype(f32))
    o = layer_norm(o, gla_norm_g, gla_norm_b).astype(x.dtype) * jax.nn.silu(gg).reshape(B, T, GLA_HEADS, GLA_DV)
    o_gla = o.reshape(B, T, GLA_HEADS * GLA_DV)
    qd = rope(dq.reshape(B, T, DSA_HEADS, DSA_HEAD_DIM), pos)
    k_new = rope(dk.reshape(B, T, DSA_KV_HEADS, DSA_HEAD_DIM), pos)
    v_new = dv.reshape(B, T, DSA_KV_HEADS, DSA_HEAD_DIM)
    qi = rope(iq.reshape(B, T, IDX_HEADS, IDX_DIM), pos)
    ki_new = rope(ik.reshape(B, T, 1, IDX_DIM), pos)[:, :, 0]
    wi = iw * IDX_W_SCALE
    k_all = jnp.concatenate([past_k.astype(x.dtype), k_new], axis=1)
    v_all = jnp.concatenate([past_v.astype(x.dtype), v_new], axis=1)
    ki_all = jnp.concatenate([past_ki.astype(x.dtype), ki_new], axis=1)
    L = past_len + T
    kpos = jnp.arange(L, dtype=jnp.int32)
    topk = min(TOPK_MAX, L // 4)
    qb = min(QBLOCK, T)
    nb = T // qb

    def to_blocks(a):
        return a.reshape(B, nb, qb, *a.shape[2:]).swapaxes(0, 1)

    def attend(blk):
        bq, bqi, bwi, btp = blk
        return dsa_block(bq, bqi, bwi, btp, k_all, v_all, ki_all, kpos, topk)

    o_dsa = lax.map(attend, (to_blocks(qd), to_blocks(qi), to_blocks(wi), pos.reshape(nb, qb)))
    o_dsa = o_dsa.swapaxes(0, 1).reshape(B, T, DSA_HEADS * DSA_HEAD_DIM)
    m = jax.nn.sigmoid(ga) * (o_gla @ w_branch_gla) + jax.nn.sigmoid(gb) * (o_dsa @ w_branch_dsa)
    return m @ w_out, s_fin, k_new, v_new, ki_new


def hier_moe(h, w_router_group, b_router_group, w_router_expert, b_router_expert, w_expert_gate,
             w_expert_up, w_expert_down):
    B, T, D = h.shape
    xt = h.reshape(B * T, D)
    g_logits = (xt @ w_router_group + b_router_group).astype(jnp.float32)
    g_prob = jax.nn.softmax(g_logits, axis=-1)
    g_sel = jnp.argmax(g_logits, axis=-1)
    e_logits = (jnp.einsum('nd,gde->nge', xt, w_router_expert) + b_router_expert).astype(jnp.float32)
    e_logits = jnp.take_along_axis(e_logits, g_sel[:, None, None], axis=1)[:, 0]
    top_val, top_idx = lax.top_k(e_logits, TOP_K_IN_GROUP)
    top_w = jax.nn.softmax(top_val, axis=-1) * jnp.take_along_axis(g_prob, g_sel[:, None], axis=1)
    expert_id = g_sel[:, None] * EXPERTS_PER_GROUP + top_idx
    gate = jnp.sum(jax.nn.one_hot(expert_id, N_EXPERTS, dtype=jnp.float32) * top_w[..., None], axis=1)
    hid = jax.nn.silu(jnp.einsum('nd,edf->nef', xt, w_expert_gate)) * jnp.einsum('nd,edf->nef', xt, w_expert_up)
    hid = (hid * gate[:, :, None]).astype(h.dtype)
    out = jnp.einsum('nef,efd->nd', hid, w_expert_down)
    return out.reshape(B, T, D).astype(h.dtype)


def trunk_layer(x, gla_s0, past_k, past_v, past_ki, w_in, w_gla_gate_up, b_gla_gate, gla_norm_g, gla_norm_b,
                w_branch_gla, w_branch_dsa, w_out, ln1_g, ln1_b, w_router_group, b_router_group,
                w_router_expert, b_router_expert, w_expert_gate, w_expert_up, w_expert_down, ln2_g, ln2_b):
    mix, s_fin, k_new, v_new, ki_new = token_mixer(x, gla_s0, past_k, past_v, past_ki, w_in, w_gla_gate_up,
                                                   b_gla_gate, gla_norm_g, gla_norm_b, w_branch_gla,
                                                   w_branch_dsa, w_out)
    h = layer_norm(ALPHA * x + mix, ln1_g, ln1_b)
    ffn = hier_moe(h, w_router_group, b_router_group, w_router_expert, b_router_expert, w_expert_gate,
                   w_expert_up, w_expert_down)
    y = layer_norm(ALPHA * h + ffn, ln2_g, ln2_b)
    return y, s_fin, k_new, v_new, ki_new


def setup_inputs(seed: int = 0) -> dict:
    key = jax.random.key(seed)
    ks = jax.random.split(key, 32)
    f32 = jnp.float32
    nrm = lambda k, shape, s: jax.random.normal(k, shape, f32) * s
    return {
        'x_prompt': nrm(ks[0], (BATCH, SEQ, D_MODEL), 1.0),
        'x_sample': nrm(ks[1], (DEC_BATCH, DEC_SEQ, D_MODEL), 1.0),
        'state_gla': nrm(ks[2], (DEPTH, DEC_BATCH, GLA_HEADS, GLA_DK, GLA_DV), 0.5),
        'cache_k': nrm(ks[3], (DEPTH, DEC_BATCH, PAST_LEN, DSA_KV_HEADS, DSA_HEAD_DIM), 1.0),
        'cache_v': nrm(ks[4], (DEPTH, DEC_BATCH, PAST_LEN, DSA_KV_HEADS, DSA_HEAD_DIM), 1.0),
        'cache_k_idx': nrm(ks[5], (DEPTH, DEC_BATCH, PAST_LEN, IDX_DIM), 1.0),
        'w_in': nrm(ks[6], (DEPTH, D_MODEL, D_IN_PROJ), D_MODEL ** -0.5),
        'w_gla_gate_up': nrm(ks[7], (DEPTH, GLA_GATE_RANK, GLA_HEADS * GLA_DK), GLA_GATE_RANK ** -0.5),
        'b_gla_gate': nrm(ks[8], (DEPTH, GLA_HEADS * GLA_DK), 0.1),
        'gla_norm_g': 1.0 + nrm(ks[9], (DEPTH, GLA_DV), 0.02),
        'gla_norm_b': nrm(ks[10], (DEPTH, GLA_DV), 0.02),
        'w_branch_gla': nrm(ks[11], (DEPTH, GLA_HEADS * GLA_DV, D_MODEL), (GLA_HEADS * GLA_DV) ** -0.5),
        'w_branch_dsa': nrm(ks[12], (DEPTH, DSA_HEADS * DSA_HEAD_DIM, D_MODEL), (DSA_HEADS * DSA_HEAD_DIM) ** -0.5),
        'w_out': nrm(ks[13], (DEPTH, D_MODEL, D_MODEL), BETA * D_MODEL ** -0.5),
        'ln1_g': 1.0 + nrm(ks[14], (DEPTH, D_MODEL), 0.02),
        'ln1_b': nrm(ks[15], (DEPTH, D_MODEL), 0.02),
        'w_router_group': nrm(ks[16], (DEPTH, D_MODEL, N_GROUPS), D_MODEL ** -0.5),
        'b_router_group': nrm(ks[17], (DEPTH, N_GROUPS), 0.01),
        'w_router_expert': nrm(ks[18], (DEPTH, N_GROUPS, D_MODEL, EXPERTS_PER_GROUP), D_MODEL ** -0.5),
        'b_router_expert': nrm(ks[19], (DEPTH, N_GROUPS, EXPERTS_PER_GROUP), 0.01),
        'w_expert_gate': nrm(ks[20], (DEPTH, N_EXPERTS, D_MODEL, D_EXPERT), D_MODEL ** -0.5),
        'w_expert_up': nrm(ks[21], (DEPTH, N_EXPERTS, D_MODEL, D_EXPERT), D_MODEL ** -0.5),
        'w_expert_down': nrm(ks[22], (DEPTH, N_EXPERTS, D_EXPERT, D_MODEL), BETA * D_EXPERT ** -0.5),
        'ln2_g': 1.0 + nrm(ks[23], (DEPTH, D_MODEL), 0.02),
        'ln2_b': nrm(ks[24], (DEPTH, D_MODEL), 0.02),
    }


def reference(x_prompt, x_sample, state_gla, cache_k, cache_v, cache_k_idx, w_in, w_gla_gate_up, b_gla_gate,
              gla_norm_g, gla_norm_b, w_branch_gla, w_branch_dsa, w_out, ln1_g, ln1_b, w_router_group,
              b_router_group, w_router_expert, b_router_expert, w_expert_gate, w_expert_up, w_expert_down,
              ln2_g, ln2_b):
    hp, hs = x_prompt, x_sample
    B = x_prompt.shape[0]
    gla_p, k_p, v_p, ki_p = [], [], [], []
    gla_s, k_s, v_s, ki_s = [], [], [], []
    for l in range(DEPTH):
        lw = (w_in[l], w_gla_gate_up[l], b_gla_gate[l], gla_norm_g[l], gla_norm_b[l], w_branch_gla[l],
              w_branch_dsa[l], w_out[l], ln1_g[l], ln1_b[l], w_router_group[l], b_router_group[l],
              w_router_expert[l], b_router_expert[l], w_expert_gate[l], w_expert_up[l], w_expert_down[l],
              ln2_g[l], ln2_b[l])
        s0 = jnp.zeros((B, GLA_HEADS, GLA_DK, GLA_DV), jnp.float32)
        ek = jnp.zeros((B, 0, DSA_KV_HEADS, DSA_HEAD_DIM), x_prompt.dtype)
        eki = jnp.zeros((B, 0, IDX_DIM), x_prompt.dtype)
        hp, sp, kp, vp, kip = trunk_layer(hp, s0, ek, ek, eki, *lw)
        hs, ss, ksn, vsn, kisn = trunk_layer(hs, state_gla[l], cache_k[l], cache_v[l], cache_k_idx[l], *lw)
        gla_p.append(sp); k_p.append(kp); v_p.append(vp); ki_p.append(kip)
        gla_s.append(ss); k_s.append(ksn); v_s.append(vsn); ki_s.append(kisn)
    return (hp, hs, jnp.stack(gla_p), jnp.stack(k_p), jnp.stack(v_p), jnp.stack(ki_p),
            jnp.stack(gla_s), jnp.stack(k_s), jnp.stack(v_s), jnp.stack(ki_s))
```

```python
import functools
import math

import jax
import jax.numpy as jnp
from jax import lax
from jax.experimental import pallas as pl
from jax.experimental.pallas import tpu as pltpu

D_MODEL = 1024
CHUNK = 64
ROPE_THETA = 10000.0
LN_EPS = 1e-5
GLA_HEADS = 4
GLA_DK = D_MODEL // 2 // GLA_HEADS
GLA_DV = D_MODEL // GLA_HEADS
GLA_GATE_RANK = 16
GLA_TAU = 16.0
DSA_HEADS = 8
DSA_KV_HEADS = 2
DSA_HEAD_DIM = D_MODEL // DSA_HEADS
IDX_HEADS = 8
IDX_DIM = 64
TOPK_MAX = 256
IDX_W_SCALE = (IDX_HEADS ** -0.5) * (IDX_DIM ** -0.5)
N_GROUPS = 4
EXPERTS_PER_GROUP = 4
N_EXPERTS = N_GROUPS * EXPERTS_PER_GROUP
D_EXPERT = 256
TOP_K_IN_GROUP = 2
DEPTH = 1
ALPHA = (2 * DEPTH) ** 0.25

LANES = 128
GLA_SUB = 16
VMEM_LIMIT = 56 * 1024 * 1024
NEG_BIG = -1e30

_GK = GLA_HEADS * GLA_DK
_GV = GLA_HEADS * GLA_DV
_DQ = DSA_HEADS * DSA_HEAD_DIM
_DKV = DSA_KV_HEADS * DSA_HEAD_DIM
_IQ = IDX_HEADS * IDX_DIM
_C_GQ = 0
_C_GK = _C_GQ + _GK
_C_GV = _C_GK + _GK
_C_GG = _C_GV + _GV
_C_DQ = _C_GG + _GV
_C_DK = _C_DQ + _DQ
_C_DV = _C_DK + _DKV
_C_IQ = _C_DV + _DKV
_C_GA = _C_IQ + _IQ
_C_GB = _C_GA + D_MODEL
_C_MISC = _C_GB + D_MODEL
_C_END = _C_MISC + LANES
_MISC_GR = IDX_DIM
_MISC_IW = IDX_DIM + GLA_GATE_RANK


def _bf16(a):
    return a.astype(jnp.bfloat16)


def _dot(a, b):
    return jnp.dot(a, b, preferred_element_type=jnp.float32)


def _dot_nt(a, b):
    return lax.dot_general(a, b, (((1,), (1,)), ((), ())), preferred_element_type=jnp.float32)


def _sigmoid(x):
    return 1.0 / (1.0 + jnp.exp(-x))


def _layer_norm(x, g, b):
    mu = jnp.mean(x, axis=-1, keepdims=True)
    xc = x - mu
    var = jnp.mean(xc * xc, axis=-1, keepdims=True)
    return xc * lax.rsqrt(var + LN_EPS) * g + b


def _rope_full(z, cos, sin):
    return z * cos + pltpu.roll(z, DSA_HEAD_DIM // 2, axis=1) * sin


def _rope_pair(z, cos, sin, first_half):
    q = IDX_DIM // 2
    partner = jnp.where(first_half, pltpu.roll(z, LANES - q, axis=1), pltpu.roll(z, q, axis=1))
    return z * cos + partner * sin


def _in_proj_kernel(x_ref, w_ref, wup_ref, bup_ref, cos_ref, sin_ref, cosi_ref, sini_ref,
                    gq_ref, gk_ref, gv_ref, sgg_ref, lf_ref, dq_ref, dk_ref, dv_ref, iq_ref,
                    sga_ref, sgb_ref, misc_ref):
    xb = _bf16(x_ref[...])

    def proj(c0, width):
        return _dot(xb, w_ref[:, c0:c0 + width])

    gq_ref[...] = proj(_C_GQ, _GK) * (GLA_DK ** -0.5)
    gk_ref[...] = proj(_C_GK, _GK)
    gv_ref[...] = proj(_C_GV, _GV)
    gg = proj(_C_GG, _GV)
    sgg_ref[...] = gg * _sigmoid(gg)
    sga_ref[...] = _sigmoid(proj(_C_GA, D_MODEL))
    sgb_ref[...] = _sigmoid(proj(_C_GB, D_MODEL))
    dv_ref[...] = proj(_C_DV, _DKV)

    cos = cos_ref[...]
    sin = sin_ref[...]
    for h in range(DSA_HEADS):
        z = proj(_C_DQ + h * DSA_HEAD_DIM, DSA_HEAD_DIM)
        dq_ref[:, h * DSA_HEAD_DIM:(h + 1) * DSA_HEAD_DIM] = _rope_full(z, cos, sin)
    for h in range(DSA_KV_HEADS):
        z = proj(_C_DK + h * DSA_HEAD_DIM, DSA_HEAD_DIM)
        dk_ref[:, h * DSA_HEAD_DIM:(h + 1) * DSA_HEAD_DIM] = _rope_full(z, cos, sin)

    cosi = cosi_ref[...]
    sini = sini_ref[...]
    lane = lax.broadcasted_iota(jnp.int32, cosi.shape, 1)
    first_half = (lane & (IDX_DIM - 1)) < (IDX_DIM // 2)
    for p in range(_IQ // LANES):
        z = proj(_C_IQ + p * LANES, LANES)
        iq_ref[:, p * LANES:(p + 1) * LANES] = _rope_pair(z, cosi, sini, first_half)

    misc = proj(_C_MISC, LANES)
    pre = _dot(_bf16(misc), wup_ref[...]) + bup_ref[...]
    lf_ref[...] = (jnp.minimum(pre, 0.0) - jnp.log(1.0 + jnp.exp(-jnp.abs(pre)))) * (1.0 / GLA_TAU)
    roped = _rope_pair(misc, cosi, sini, first_half)
    misc_ref[...] = jnp.where(lane < IDX_DIM, roped, misc * IDX_W_SCALE)


def _rope_tables(T, past, rows):
    pos = (past + jnp.arange(T, dtype=jnp.int32)).astype(jnp.float32)

    def tables(dim):
        half = dim // 2
        inv = ROPE_THETA ** (-jnp.arange(half, dtype=jnp.float32) / half)
        ang = pos[:, None] * inv[None, :]
        c, s = jnp.cos(ang), jnp.sin(ang)
        reps = LANES // dim
        cos = jnp.tile(jnp.concatenate([c, c], axis=1), (1, reps))
        sin = jnp.tile(jnp.concatenate([-s, s], axis=1), (1, reps))
        if rows > T:
            cos = jnp.tile(cos, (rows // T, 1))
            sin = jnp.tile(sin, (rows // T, 1))
        return cos, sin

    return tables(DSA_HEAD_DIM) + tables(IDX_DIM)


def _prep_w_in(w_in, w_up, b_up):
    sizes = (_GK, _GK, _GV, _GV, GLA_GATE_RANK, _DQ, _DKV, _DKV, _IQ, IDX_DIM, IDX_HEADS, D_MODEL, D_MODEL)
    offs = [0]
    for s in sizes:
        offs.append(offs[-1] + s)
    gq, gk, gv, gg, gr, dq, dk, dv, iq, ik, iw, ga, gb = (w_in[:, offs[i]:offs[i + 1]] for i in range(len(sizes)))
    pad = jnp.zeros((D_MODEL, LANES - IDX_DIM - GLA_GATE_RANK - IDX_HEADS), w_in.dtype)
    w = jnp.concatenate([gq, gk, gv, gg, dq, dk, dv, iq, ga, gb, ik, gr, iw, pad], axis=1)
    wup = jnp.zeros((LANES, _GK), w_up.dtype).at[_MISC_GR:_MISC_GR + GLA_GATE_RANK].set(w_up)
    return _bf16(w), _bf16(wup), b_up.reshape(1, _GK)


def _in_proj(x2d, w, wup, bup, T, past, tm):
    n = x2d.shape[0]
    rows = max(T, tm)
    cos, sin, cosi, sini = _rope_tables(T, past, rows)
    nt = rows // tm
    tok = lambda i: (i, 0)
    tab = lambda i: (i % nt, 0)
    whole = lambda i: (0, 0)
    widths = (_GK, _GK, _GV, _GV, _GK, _DQ, _DKV, _DKV, _IQ, D_MODEL, D_MODEL, LANES)
    return pl.pallas_call(
        _in_proj_kernel,
        grid=(n // tm,),
        in_specs=[pl.BlockSpec((tm, D_MODEL), tok),
                  pl.BlockSpec((D_MODEL, _C_END), whole),
                  pl.BlockSpec((LANES, _GK), whole),
                  pl.BlockSpec((1, _GK), whole),
                  pl.BlockSpec((tm, LANES), tab), pl.BlockSpec((tm, LANES), tab),
                  pl.BlockSpec((tm, LANES), tab), pl.BlockSpec((tm, LANES), tab)],
        out_specs=[pl.BlockSpec((tm, wd), tok) for wd in widths],
        out_shape=[jax.ShapeDtypeStruct((n, wd), jnp.float32) for wd in widths],
        compiler_params=pltpu.CompilerParams(dimension_semantics=("arbitrary",), vmem_limit_bytes=VMEM_LIMIT),
        name="in_proj",
    )(x2d, w, wup, bup, cos, sin, cosi, sini)


def _split3(a):
    hi = _bf16(a)
    r1 = a - hi.astype(jnp.float32)
    mid = _bf16(r1)
    lo = _bf16(r1 - mid.astype(jnp.float32))
    return hi, mid, lo


def _gla_kernel(q_ref, k_ref, lf_ref, v_ref, sgg_ref, s0_ref, g_ref, b_ref, o_ref, s_ref, a_ref, *, c):
    ci = pl.program_id(1)

    @pl.when(ci == 0)
    def _():
        s_ref[...] = s0_ref[...]

    nsub = c // GLA_SUB
    row = lax.broadcasted_iota(jnp.int32, (c, c), 0)
    col = lax.broadcasted_iota(jnp.int32, (c, c), 1)
    tri = _bf16(jnp.where(row >= col, 1.0, 0.0))
    srow = lax.broadcasted_iota(jnp.int32, (GLA_SUB, GLA_SUB), 0)
    scol = lax.broadcasted_iota(jnp.int32, (GLA_SUB, GLA_SUB), 1)
    gam = g_ref[...]
    bet = b_ref[...]

    for h in range(GLA_HEADS):
        ks = slice(h * GLA_DK, (h + 1) * GLA_DK)
        vs = slice(h * GLA_DV, (h + 1) * GLA_DV)
        q = q_ref[:, ks]
        k = k_ref[:, ks]
        v = v_ref[:, vs]
        hi, mid, lo = _split3(lf_ref[:, ks])
        b = _dot(tri, hi) + _dot(tri, mid) + _dot(tri, lo)
        s = s_ref[h]
        o = _dot(_bf16(q * jnp.exp(b)), _bf16(s))

        for bi in range(nsub):
            r0 = bi * GLA_SUB
            rs = slice(r0, r0 + GLA_SUB)
            qi = q[rs]
            bq = b[rs]
            ki = k[rs]
            d = jnp.zeros((GLA_SUB, GLA_SUB), jnp.float32)
            for j in range(GLA_SUB):
                f = jnp.exp(jnp.minimum(bq - bq[j:j + 1], 0.0)) * qi * ki[j:j + 1]
                d = jnp.where(scol == j, jnp.sum(f, axis=-1, keepdims=True), d)
            a_ref[rs, r0:r0 + GLA_SUB] = jnp.where(srow >= scol, d, 0.0)
            if bi > 0:
                ref_b = b[r0:r0 + 1]
                qt = qi * jnp.exp(bq - ref_b)
                kt = k[:r0] * jnp.exp(ref_b - b[:r0])
                a_ref[rs, :r0] = _dot_nt(_bf16(qt), _bf16(kt))
            if bi < nsub - 1:
                a_ref[rs, r0 + GLA_SUB:] = jnp.zeros((GLA_SUB, c - r0 - GLA_SUB), jnp.float32)
        o = o + _dot(_bf16(a_ref[...]), _bf16(v))

        bt = b.T
        bl = bt[:, c - 1:c]
        kdt = k.T * jnp.exp(bl - bt)
        s_ref[h] = jnp.exp(bl) * s + _dot(_bf16(kdt), _bf16(v))

        o_ref[:, vs] = _layer_norm(o, gam, bet) * sgg_ref[:, vs]


def _gla(q, k, lf, v, sgg, s0, gam, bet):
    B, T, _ = q.shape
    c = min(CHUNK, T)
    tokk = lambda b, i: (b, i, 0)
    st = lambda b, i: (b, 0, 0, 0)
    whole = lambda b, i: (0, 0)
    return pl.pallas_call(
        functools.partial(_gla_kernel, c=c),
        grid=(B, T // c),
        in_specs=[pl.BlockSpec((None, c, _GK), tokk), pl.BlockSpec((None, c, _GK), tokk),
                  pl.BlockSpec((None, c, _GK), tokk), pl.BlockSpec((None, c, _GV), tokk),
                  pl.BlockSpec((None, c, _GV), tokk),
                  pl.BlockSpec((None, GLA_HEADS, GLA_DK, GLA_DV), st),
                  pl.BlockSpec((1, GLA_DV), whole), pl.BlockSpec((1, GLA_DV), whole)],
        out_specs=[pl.BlockSpec((None, c, _GV), tokk),
                   pl.BlockSpec((None, GLA_HEADS, GLA_DK, GLA_DV), st)],
        out_shape=[jax.ShapeDtypeStruct((B, T, _GV), jnp.float32),
                   jax.ShapeDtypeStruct((B, GLA_HEADS, GLA_DK, GLA_DV), jnp.float32)],
        scratch_shapes=[pltpu.VMEM((c, c), jnp.float32)],
        compiler_params=pltpu.CompilerParams(dimension_semantics=("arbitrary", "arbitrary"),
                                             vmem_limit_bytes=VMEM_LIMIT),
        name="gla",
    )(q, k, lf, v, sgg, s0, gam, bet)


def _sortable_key(s):
    bits = pltpu.bitcast(s, jnp.int32)
    key = bits ^ ((bits >> 31) & 0x7FFFFFFF)
    return jnp.where(key == -1, 0, key)


def _dsa_kernel(qd_ref, qi_ref, misc_ref, k_ref, v_ref, ki_ref, o_ref,
                key_ref, tie_ref, m_ref, l_ref, acc_ref, *, tq, tk, past, seq_len, topk):
    t0 = pl.program_id(1) * tq
    total = past + seq_len
    rowpos = past + t0 + lax.broadcasted_iota(jnp.int32, (tq, 1), 0)
    limit = jnp.minimum(((rowpos >> 6) + 1) * CHUNK, total)
    lim_max = jnp.minimum(((past + t0 + tq - 1) // CHUNK + 1) * CHUNK, total)
    nk = (lim_max + tk - 1) // tk
    nslab = tk // LANES
    lane_pos = lax.broadcasted_iota(jnp.int32, (1, tk), 1)
    lp = key_ref.shape[0] * tk

    wi = misc_ref[:, _MISC_IW:_MISC_IW + IDX_HEADS]
    qis = [_bf16(qi_ref[:, h * IDX_DIM:(h + 1) * IDX_DIM]) for h in range(IDX_HEADS)]
    neg_inf = jnp.float32(-jnp.inf)

    def score_tile(kt, carry):
        kit = ki_ref[pl.ds(pl.multiple_of(kt * tk, tk), tk), :]
        s = None
        for h in range(IDX_HEADS):
            term = wi[:, h:h + 1] * jnp.maximum(_dot_nt(qis[h], kit), 0.0)
            s = term if s is None else s + term
        kpos = kt * tk + lane_pos
        s = jnp.where(kpos < limit, s, neg_inf)
        key_ref[kt] = _sortable_key(s)
        return carry

    lax.fori_loop(0, nk, score_tile, 0)

    def count_ge(ref, cand):
        cb = jnp.broadcast_to(cand, (tq, LANES))

        def body(kt, acc):
            tile = ref[kt]
            for sl in range(nslab):
                acc = acc + jnp.where(tile[:, sl * LANES:(sl + 1) * LANES] >= cb, 1, 0)
            return acc

        acc = lax.fori_loop(0, nk, body, jnp.zeros((tq, LANES), jnp.int32))
        return jnp.sum(acc, axis=-1, keepdims=True)

    def value_bit(i, thr):
        cand = thr ^ jnp.left_shift(jnp.int32(1), 31 - i)
        return jnp.where(count_ge(key_ref, cand) >= topk, cand, thr)

    int_min = jnp.int32(-2 ** 31)
    thr = lax.fori_loop(0, 32, value_bit, jnp.full((tq, 1), int_min, jnp.int32))

    n_gt = count_ge(key_ref, thr + 1)
    need = topk - n_gt

    def tie_tile(kt, carry):
        kpos = kt * tk + lane_pos
        tie_ref[kt] = jnp.where(key_ref[kt] == thr, lp - kpos, 0)
        return carry

    lax.fori_loop(0, nk, tie_tile, 0)
    nbits = lp.bit_length()

    def tie_bit(i, t2):
        cand = t2 | jnp.left_shift(jnp.int32(1), nbits - 1 - i)
        return jnp.where(count_ge(tie_ref, cand) >= need, cand, t2)

    t2 = lax.fori_loop(0, nbits, tie_bit, jnp.zeros((tq, 1), jnp.int32))
    t2 = jnp.maximum(t2, 1)

    def select_tile(kt, carry):
        kpos = kt * tk + lane_pos
        sel = ((key_ref[kt] > thr) | (tie_ref[kt] >= t2)) & (kpos < limit)
        tie_ref[kt] = jnp.where(sel, 1, 0)
        return carry

    lax.fori_loop(0, nk, select_tile, 0)

    scale = DSA_HEAD_DIM ** -0.5
    rep = DSA_HEADS // DSA_KV_HEADS
    m_ref[...] = jnp.full(m_ref.shape, NEG_BIG, jnp.float32)
    l_ref[...] = jnp.zeros(l_ref.shape, jnp.float32)
    acc_ref[...] = jnp.zeros(acc_ref.shape, jnp.float32)
    qs = [_bf16(qd_ref[:, h * DSA_HEAD_DIM:(h + 1) * DSA_HEAD_DIM]) for h in range(DSA_HEADS)]

    def attend_tile(kt, carry):
        rows = pl.ds(pl.multiple_of(kt * tk, tk), tk)
        sel = tie_ref[kt] > 0
        for g in range(DSA_KV_HEADS):
            kg = k_ref[rows, g * DSA_HEAD_DIM:(g + 1) * DSA_HEAD_DIM]
            vg = v_ref[rows, g * DSA_HEAD_DIM:(g + 1) * DSA_HEAD_DIM]
            for r in range(rep):
                h = g * rep + r
                logit = jnp.where(sel, _dot_nt(qs[h], kg) * scale, NEG_BIG)
                m_old = m_ref[h]
                m_new = jnp.maximum(m_old, jnp.max(logit, axis=-1, keepdims=True))
                p = jnp.where(sel, jnp.exp(logit - m_new), 0.0)
                a = jnp.exp(m_old - m_new)
                l_ref[h] = a * l_ref[h] + jnp.sum(p, axis=-1, keepdims=True)
                acc_ref[h] = a * acc_ref[h] + _dot(_bf16(p), vg)
                m_ref[h] = m_new
        return carry

    lax.fori_loop(0, nk, attend_tile, 0)
    for h in range(DSA_HEADS):
        o_ref[:, h * DSA_HEAD_DIM:(h + 1) * DSA_HEAD_DIM] = acc_ref[h] / l_ref[h]


def _dsa(qd, qi, misc, k_all, v_all, ki_all, past, seq_len, tk):
    B, T, _ = qd.shape
    tq = min(LANES, T)
    lp = k_all.shape[1]
    nkt = lp // tk
    topk = min(TOPK_MAX, (past + seq_len) // 4)
    tokk = lambda b, i: (b, i, 0)
    kv = lambda b, i: (b, 0, 0)
    return pl.pallas_call(
        functools.partial(_dsa_kernel, tq=tq, tk=tk, past=past, seq_len=seq_len, topk=topk),
        grid=(B, T // tq),
        in_specs=[pl.BlockSpec((None, tq, _DQ), tokk), pl.BlockSpec((None, tq, _IQ), tokk),
                  pl.BlockSpec((None, tq, LANES), tokk),
                  pl.BlockSpec((None, lp, _DKV), kv), pl.BlockSpec((None, lp, _DKV), kv),
                  pl.BlockSpec((None, lp, IDX_DIM), kv)],
        out_specs=pl.BlockSpec((None, tq, _DQ), tokk),
        out_shape=jax.ShapeDtypeStruct((B, T, _DQ), jnp.float32),
        scratch_shapes=[pltpu.VMEM((nkt, tq, tk), jnp.int32), pltpu.VMEM((nkt, tq, tk), jnp.int32),
                        pltpu.VMEM((DSA_HEADS, tq, 1), jnp.float32),
                        pltpu.VMEM((DSA_HEADS, tq, 1), jnp.float32),
                        pltpu.VMEM((DSA_HEADS, tq, DSA_HEAD_DIM), jnp.float32)],
        compiler_params=pltpu.CompilerParams(dimension_semantics=("arbitrary", "arbitrary"),
                                             vmem_limit_bytes=VMEM_LIMIT),
        name="dsa",
    )(qd, qi, misc, k_all, v_all, ki_all)


def _merge_kernel(og_ref, od_ref, sga_ref, sgb_ref, x_ref, wg_ref, wd_ref, wo_ref, g_ref, b_ref, h_ref):
    m = sga_ref[...] * _dot(_bf16(og_ref[...]), wg_ref[...]) + sgb_ref[...] * _dot(_bf16(od_ref[...]), wd_ref[...])
    mix = _dot(_bf16(m), wo_ref[...])
    h_ref[...] = _layer_norm(ALPHA * x_ref[...] + mix, g_ref[...], b_ref[...])


def _merge(og, od, sga, sgb, x2d, wg, wd, wo, g, b, tm):
    n = x2d.shape[0]
    tok = lambda i: (i, 0)
    whole = lambda i: (0, 0)
    act = pl.BlockSpec((tm, D_MODEL), tok)
    wsp = pl.BlockSpec((D_MODEL, D_MODEL), whole)
    vec = pl.BlockSpec((1, D_MODEL), whole)
    return pl.pallas_call(
        _merge_kernel,
        grid=(n // tm,),
        in_specs=[act, act, act, act, act, wsp, wsp, wsp, vec, vec],
        out_specs=act,
        out_shape=jax.ShapeDtypeStruct((n, D_MODEL), jnp.float32),
        compiler_params=pltpu.CompilerParams(dimension_semantics=("arbitrary",), vmem_limit_bytes=VMEM_LIMIT),
        name="merge",
    )(og, od, sga, sgb, x2d, wg, wd, wo, g, b)


def _first_lane_where(mask, lane):
    return jnp.min(jnp.where(mask, lane, LANES), axis=-1, keepdims=True)


def _route(logits):
    lane = lax.broadcasted_iota(jnp.int32, logits.shape, 1)
    is_g = lane < N_GROUPS
    gl = jnp.where(is_g, logits, NEG_BIG)
    gmax = jnp.max(gl, axis=-1, keepdims=True)
    g_sel = _first_lane_where(is_g & (gl == gmax), lane)
    g_prob = 1.0 / jnp.sum(jnp.where(is_g, jnp.exp(gl - gmax), 0.0), axis=-1, keepdims=True)
    e_lane = lane - N_GROUPS
    in_grp = (e_lane >= 0) & (e_lane < N_EXPERTS) & ((e_lane >> 2) == g_sel)
    el = jnp.where(in_grp, logits, NEG_BIG)
    v1 = jnp.max(el, axis=-1, keepdims=True)
    i1 = _first_lane_where(in_grp & (el == v1), lane)
    rest = in_grp & (lane != i1)
    el2 = jnp.where(rest, logits, NEG_BIG)
    v2 = jnp.max(el2, axis=-1, keepdims=True)
    i2 = _first_lane_where(rest & (el2 == v2), lane)
    e2 = jnp.exp(v2 - v1)
    w1 = g_prob / (1.0 + e2)
    w2 = g_prob * e2 / (1.0 + e2)
    return jnp.where(lane == i1, w1, 0.0) + jnp.where(lane == i2, w2, 0.0)


def _moe_kernel(h_ref, wr_ref, br_ref, wg_ref, wu_ref, wd_ref, g_ref, b_ref, y_ref, gate_ref, acc_ref):
    e = pl.program_id(1)
    h = h_ref[...]

    @pl.when(e == 0)
    def _():
        logits = jnp.dot(h, wr_ref[...], preferred_element_type=jnp.float32,
                         precision=lax.Precision.HIGHEST) + br_ref[...]
        gate_ref[...] = _route(logits)
        acc_ref[...] = jnp.zeros(acc_ref.shape, jnp.float32)

    hb = _bf16(h)
    lane = lax.broadcasted_iota(jnp.int32, gate_ref.shape, 1)
    gcol = jnp.sum(jnp.where(lane == e + N_GROUPS, gate_ref[...], 0.0), axis=-1, keepdims=True)
    a = _dot(hb, wg_ref[...])
    hid = a * _sigmoid(a) * _dot(hb, wu_ref[...]) * gcol
    acc_ref[...] += _dot(_bf16(hid), wd_ref[...])

    @pl.when(e == N_EXPERTS - 1)
    def _():
        y_ref[...] = _layer_norm(ALPHA * h + acc_ref[...], g_ref[...], b_ref[...])


def _moe(h2d, wr, br, wg, wu, wd, g, b, tm):
    n = h2d.shape[0]
    tok = lambda i, e: (i, 0)
    whole = lambda i, e: (0, 0)
    per_e = lambda i, e: (e, 0, 0)
    return pl.pallas_call(
        _moe_kernel,
        grid=(n // tm, N_EXPERTS),
        in_specs=[pl.BlockSpec((tm, D_MODEL), tok),
                  pl.BlockSpec((D_MODEL, LANES), whole), pl.BlockSpec((1, LANES), whole),
                  pl.BlockSpec((None, D_MODEL, D_EXPERT), per_e),
                  pl.BlockSpec((None, D_MODEL, D_EXPERT), per_e),
                  pl.BlockSpec((None, D_EXPERT, D_MODEL), per_e),
                  pl.BlockSpec((1, D_MODEL), whole), pl.BlockSpec((1, D_MODEL), whole)],
        out_specs=pl.BlockSpec((tm, D_MODEL), tok),
        out_shape=jax.ShapeDtypeStruct((n, D_MODEL), jnp.float32),
        scratch_shapes=[pltpu.VMEM((tm, LANES), jnp.float32), pltpu.VMEM((tm, D_MODEL), jnp.float32)],
        compiler_params=pltpu.CompilerParams(dimension_semantics=("arbitrary", "arbitrary"),
                                             vmem_limit_bytes=VMEM_LIMIT),
        name="moe",
    )(h2d, wr, br, wg, wu, wd, g, b)


def _prep_weights(w_in, w_up, b_up, gla_g, gla_b, w_bg, w_bd, w_o, ln1_g, ln1_b, w_rg, b_rg, w_re, b_re,
                  w_eg, w_eu, w_ed, ln2_g, ln2_b):
    w, wup, bup = _prep_w_in(w_in, w_up, b_up)
    w_re2 = jnp.transpose(w_re, (1, 0, 2)).reshape(D_MODEL, N_EXPERTS)
    rpad = jnp.zeros((D_MODEL, LANES - N_GROUPS - N_EXPERTS), w_rg.dtype)
    wr = jnp.concatenate([w_rg, w_re2, rpad], axis=1)
    br = jnp.concatenate([b_rg, b_re.reshape(N_EXPERTS), jnp.zeros((LANES - N_GROUPS - N_EXPERTS,), b_rg.dtype)])
    row = lambda a: a.reshape(1, -1)
    return dict(w=w, wup=wup, bup=bup, gla_g=row(gla_g), gla_b=row(gla_b), wbg=_bf16(w_bg), wbd=_bf16(w_bd),
                wo=_bf16(w_o), ln1_g=row(ln1_g), ln1_b=row(ln1_b), wr=wr, br=row(br), weg=_bf16(w_eg),
                weu=_bf16(w_eu), wed=_bf16(w_ed), ln2_g=row(ln2_g), ln2_b=row(ln2_b))


def _token_tile(n):
    for tm in (512, 256, 128):
        if n % tm == 0:
            return tm
    raise ValueError(f"token count {n} is not a multiple of 128")


def _trunk_layer(x, s0, past_k, past_v, past_ki, p):
    B, T, _ = x.shape
    past = past_k.shape[1]
    n = B * T
    tm = _token_tile(n)
    assert T % tm == 0 or tm % T == 0
    x2d = x.reshape(n, D_MODEL)
    gq, gk, gv, sgg, lf, dq, dk, dv, iq, sga, sgb, misc = _in_proj(x2d, p["w"], p["wup"], p["bup"], T, past, tm)
    seq = lambda a: a.reshape(B, T, a.shape[-1])

    o_gla, s_fin = _gla(seq(gq), seq(gk), seq(lf), seq(gv), seq(sgg), s0, p["gla_g"], p["gla_b"])

    k_new, v_new, ki_new = seq(dk), seq(dv), seq(misc)[:, :, :IDX_DIM]
    total = past + T
    tk = 512 if total % 512 == 0 else LANES
    lp = -(-total // tk) * tk

    def keys(past_rows, new_rows):
        rows = jnp.concatenate([past_rows.reshape(B, past, new_rows.shape[-1]), new_rows], axis=1)
        return _bf16(jnp.pad(rows, ((0, 0), (0, lp - total), (0, 0))))

    o_dsa = _dsa(seq(dq), seq(iq), seq(misc), keys(past_k, k_new), keys(past_v, v_new), keys(past_ki, ki_new),
                 past, T, tk)

    h = _merge(o_gla.reshape(n, _GV), o_dsa.reshape(n, _DQ), sga, sgb, x2d, p["wbg"], p["wbd"], p["wo"],
               p["ln1_g"], p["ln1_b"], tm)
    y = _moe(h, p["wr"], p["br"], p["weg"], p["weu"], p["wed"], p["ln2_g"], p["ln2_b"], tm)
    return (y.reshape(B, T, D_MODEL), s_fin, k_new.reshape(B, T, DSA_KV_HEADS, DSA_HEAD_DIM),
            v_new.reshape(B, T, DSA_KV_HEADS, DSA_HEAD_DIM), ki_new)


def kernel(x_prompt, x_sample, state_gla, cache_k, cache_v, cache_k_idx, w_in, w_gla_gate_up, b_gla_gate,
           gla_norm_g, gla_norm_b, w_branch_gla, w_branch_dsa, w_out, ln1_g, ln1_b, w_router_group,
           b_router_group, w_router_expert, b_router_expert, w_expert_gate, w_expert_up, w_expert_down,
           ln2_g, ln2_b):
    B = x_prompt.shape[0]
    l = 0
    p = _prep_weights(w_in[l], w_gla_gate_up[l], b_gla_gate[l], gla_norm_g[l], gla_norm_b[l], w_branch_gla[l],
                      w_branch_dsa[l], w_out[l], ln1_g[l], ln1_b[l], w_router_group[l], b_router_group[l],
                      w_router_expert[l], b_router_expert[l], w_expert_gate[l], w_expert_up[l],
                      w_expert_down[l], ln2_g[l], ln2_b[l])
    s0 = jnp.zeros((B, GLA_HEADS, GLA_DK, GLA_DV), jnp.float32)
    ek = jnp.zeros((B, 0, DSA_KV_HEADS, DSA_HEAD_DIM), x_prompt.dtype)
    eki = jnp.zeros((B, 0, IDX_DIM), x_prompt.dtype)
    yp, sp, kp, vp, kip = _trunk_layer(x_prompt, s0, ek, ek, eki, p)
    ys, ss, ksn, vsn, kisn = _trunk_layer(x_sample, state_gla[l], cache_k[l], cache_v[l], cache_k_idx[l], p)
    stack = lambda a: a[None]
    return (yp, ys, stack(sp), stack(kp), stack(vp), stack(kip), stack(ss), stack(ksn), stack(vsn), stack(kisn))
```

```python
import functools
import math

import jax
import jax.numpy as jnp
from jax import lax
from jax.experimental import pallas as pl
from jax.experimental.pallas import tpu as pltpu

D_MODEL = 1024
CHUNK = 64
ROPE_THETA = 10000.0
LN_EPS = 1e-5
GLA_HEADS = 4
GLA_DK = D_MODEL // 2 // GLA_HEADS
GLA_DV = D_MODEL // GLA_HEADS
GLA_GATE_RANK = 16
GLA_TAU = 16.0
DSA_HEADS = 8
DSA_KV_HEADS = 2
DSA_HEAD_DIM = D_MODEL // DSA_HEADS
IDX_HEADS = 8
IDX_DIM = 64
TOPK_MAX = 256
IDX_W_SCALE = (IDX_HEADS ** -0.5) * (IDX_DIM ** -0.5)
N_GROUPS = 4
EXPERTS_PER_GROUP = 4
N_EXPERTS = N_GROUPS * EXPERTS_PER_GROUP
D_EXPERT = 256
TOP_K_IN_GROUP = 2
DEPTH = 1
ALPHA = (2 * DEPTH) ** 0.25

LANES = 128
GLA_SUB = 16
VMEM_LIMIT = 56 * 1024 * 1024
NEG_BIG = -1e30

_GK = GLA_HEADS * GLA_DK
_GV = GLA_HEADS * GLA_DV
_DQ = DSA_HEADS * DSA_HEAD_DIM
_DKV = DSA_KV_HEADS * DSA_HEAD_DIM
_IQ = IDX_HEADS * IDX_DIM
_C_GQ = 0
_C_GK = _C_GQ + _GK
_C_GV = _C_GK + _GK
_C_GG = _C_GV + _GV
_C_DQ = _C_GG + _GV
_C_DK = _C_DQ + _DQ
_C_DV = _C_DK + _DKV
_C_IQ = _C_DV + _DKV
_C_GA = _C_IQ + _IQ
_C_GB = _C_GA + D_MODEL
_C_MISC = _C_GB + D_MODEL
_C_END = _C_MISC + LANES
_MISC_GR = IDX_DIM
_MISC_IW = IDX_DIM + GLA_GATE_RANK


def _bf16(a):
    return a.astype(jnp.bfloat16)


def _dot(a, b):
    return jnp.dot(a, b, preferred_element_type=jnp.float32)


def _dot_nt(a, b):
    return lax.dot_general(a, b, (((1,), (1,)), ((), ())), preferred_element_type=jnp.float32)


def _sigmoid(x):
    return 1.0 / (1.0 + jnp.exp(-x))


def _layer_norm(x, g, b):
    mu = jnp.mean(x, axis=-1, keepdims=True)
    xc = x - mu
    var = jnp.mean(xc * xc, axis=-1, keepdims=True)
    return xc * lax.rsqrt(var + LN_EPS) * g + b


def _rope_full(z, cos, sin):
    return z * cos + pltpu.roll(z, DSA_HEAD_DIM // 2, axis=1) * sin


def _rope_pair(z, cos, sin, first_half):
    q = IDX_DIM // 2
    partner = jnp.where(first_half, pltpu.roll(z, LANES - q, axis=1), pltpu.roll(z, q, axis=1))
    return z * cos + partner * sin


def _in_proj_kernel(x_ref, w_ref, wup_ref, bup_ref, cos_ref, sin_ref, cosi_ref, sini_ref,
                    gq_ref, gk_ref, gv_ref, sgg_ref, lf_ref, dq_ref, dk_ref, dv_ref, iq_ref,
                    sga_ref, sgb_ref, misc_ref):
    xb = _bf16(x_ref[...])

    def proj(c0, width):
        return _dot(xb, w_ref[:, c0:c0 + width])

    gq_ref[...] = proj(_C_GQ, _GK) * (GLA_DK ** -0.5)
    gk_ref[...] = proj(_C_GK, _GK)
    gv_ref[...] = proj(_C_GV, _GV)
    gg = proj(_C_GG, _GV)
    sgg_ref[...] = gg * _sigmoid(gg)
    sga_ref[...] = _sigmoid(proj(_C_GA, D_MODEL))
    sgb_ref[...] = _sigmoid(proj(_C_GB, D_MODEL))
    dv_ref[...] = proj(_C_DV, _DKV)

    cos = cos_ref[...]
    sin = sin_ref[...]
    for h in range(DSA_HEADS):
        z = proj(_C_DQ + h * DSA_HEAD_DIM, DSA_HEAD_DIM)
        dq_ref[:, h * DSA_HEAD_DIM:(h + 1) * DSA_HEAD_DIM] = _rope_full(z, cos, sin)
    for h in range(DSA_KV_HEADS):
        z = proj(_C_DK + h * DSA_HEAD_DIM, DSA_HEAD_DIM)
        dk_ref[:, h * DSA_HEAD_DIM:(h + 1) * DSA_HEAD_DIM] = _rope_full(z, cos, sin)

    cosi = cosi_ref[...]
    sini = sini_ref[...]
    lane = lax.broadcasted_iota(jnp.int32, cosi.shape, 1)
    first_half = (lane & (IDX_DIM - 1)) < (IDX_DIM // 2)
    for p in range(_IQ // LANES):
        z = proj(_C_IQ + p * LANES, LANES)
        iq_ref[:, p * LANES:(p + 1) * LANES] = _rope_pair(z, cosi, sini, first_half)

    misc = proj(_C_MISC, LANES)
    pre = _dot(_bf16(misc), wup_ref[...]) + bup_ref[...]
    lf_ref[...] = (jnp.minimum(pre, 0.0) - jnp.log(1.0 + jnp.exp(-jnp.abs(pre)))) * (1.0 / GLA_TAU)
    roped = _rope_pair(misc, cosi, sini, first_half)
    misc_ref[...] = jnp.where(lane < IDX_DIM, roped, misc * IDX_W_SCALE)


def _rope_tables(T, past, rows):
    pos = (past + jnp.arange(T, dtype=jnp.int32)).astype(jnp.float32)

    def tables(dim):
        half = dim // 2
        inv = ROPE_THETA ** (-jnp.arange(half, dtype=jnp.float32) / half)
        ang = pos[:, None] * inv[None, :]
        c, s = jnp.cos(ang), jnp.sin(ang)
        reps = LANES // dim
        cos = jnp.tile(jnp.concatenate([c, c], axis=1), (1, reps))
        sin = jnp.tile(jnp.concatenate([-s, s], axis=1), (1, reps))
        if rows > T:
            cos = jnp.tile(cos, (rows // T, 1))
            sin = jnp.tile(sin, (rows // T, 1))
        return cos, sin

    return tables(DSA_HEAD_DIM) + tables(IDX_DIM)


def _prep_w_in(w_in, w_up, b_up):
    sizes = (_GK, _GK, _GV, _GV, GLA_GATE_RANK, _DQ, _DKV, _DKV, _IQ, IDX_DIM, IDX_HEADS, D_MODEL, D_MODEL)
    offs = [0]
    for s in sizes:
        offs.append(offs[-1] + s)
    gq, gk, gv, gg, gr, dq, dk, dv, iq, ik, iw, ga, gb = (w_in[:, offs[i]:offs[i + 1]] for i in range(len(sizes)))
    pad = jnp.zeros((D_MODEL, LANES - IDX_DIM - GLA_GATE_RANK - IDX_HEADS), w_in.dtype)
    w = jnp.concatenate([gq, gk, gv, gg, dq, dk, dv, iq, ga, gb, ik, gr, iw, pad], axis=1)
    wup = jnp.zeros((LANES, _GK), w_up.dtype).at[_MISC_GR:_MISC_GR + GLA_GATE_RANK].set(w_up)
    return _bf16(w), _bf16(wup), b_up.reshape(1, _GK)


def _in_proj(x2d, w, wup, bup, T, past, tm):
    n = x2d.shape[0]
    rows = max(T, tm)
    cos, sin, cosi, sini = _rope_tables(T, past, rows)
    nt = rows // tm
    tok = lambda i: (i, 0)
    tab = lambda i: (i % nt, 0)
    whole = lambda i: (0, 0)
    widths = (_GK, _GK, _GV, _GV, _GK, _DQ, _DKV, _DKV, _IQ, D_MODEL, D_MODEL, LANES)
    return pl.pallas_call(
        _in_proj_kernel,
        grid=(n // tm,),
        in_specs=[pl.BlockSpec((tm, D_MODEL), tok),
                  pl.BlockSpec((D_MODEL, _C_END), whole),
                  pl.BlockSpec((LANES, _GK), whole),
                  pl.BlockSpec((1, _GK), whole),
                  pl.BlockSpec((tm, LANES), tab), pl.BlockSpec((tm, LANES), tab),
                  pl.BlockSpec((tm, LANES), tab), pl.BlockSpec((tm, LANES), tab)],
        out_specs=[pl.BlockSpec((tm, wd), tok) for wd in widths],
        out_shape=[jax.ShapeDtypeStruct((n, wd), jnp.float32) for wd in widths],
        compiler_params=pltpu.CompilerParams(dimension_semantics=("arbitrary",), vmem_limit_bytes=VMEM_LIMIT),
        name="in_proj",
    )(x2d, w, wup, bup, cos, sin, cosi, sini)


def _split3(a):
    hi = _bf16(a)
    r1 = a - hi.astype(jnp.float32)
    mid = _bf16(r1)
    lo = _bf16(r1 - mid.astype(jnp.float32))
    return hi, mid, lo


def _gla_kernel(q_ref, k_ref, lf_ref, v_ref, sgg_ref, s0_ref, g_ref, b_ref, o_ref, s_ref, a_ref, *, c):
    ci = pl.program_id(1)

    @pl.when(ci == 0)
    def _():
        s_ref[...] = s0_ref[...]

    nsub = c // GLA_SUB
    row = lax.broadcasted_iota(jnp.int32, (c, c), 0)
    col = lax.broadcasted_iota(jnp.int32, (c, c), 1)
    tri = _bf16(jnp.where(row >= col, 1.0, 0.0))
    srow = lax.broadcasted_iota(jnp.int32, (GLA_SUB, GLA_SUB), 0)
    scol = lax.broadcasted_iota(jnp.int32, (GLA_SUB, GLA_SUB), 1)
    gam = g_ref[...]
    bet = b_ref[...]

    for h in range(GLA_HEADS):
        ks = slice(h * GLA_DK, (h + 1) * GLA_DK)
        vs = slice(h * GLA_DV, (h + 1) * GLA_DV)
        q = q_ref[:, ks]
        k = k_ref[:, ks]
        v = v_ref[:, vs]
        hi, mid, lo = _split3(lf_ref[:, ks])
        b = _dot(tri, hi) + _dot(tri, mid) + _dot(tri, lo)
        s = s_ref[h]
        o = _dot(_bf16(q * jnp.exp(b)), _bf16(s))

        for bi in range(nsub):
            r0 = bi * GLA_SUB
            rs = slice(r0, r0 + GLA_SUB)
            qi = q[rs]
            bq = b[rs]
            ki = k[rs]
            d = jnp.zeros((GLA_SUB, GLA_SUB), jnp.float32)
            for j in range(GLA_SUB):
                f = jnp.exp(jnp.minimum(bq - bq[j:j + 1], 0.0)) * qi * ki[j:j + 1]
                d = jnp.where(scol == j, jnp.sum(f, axis=-1, keepdims=True), d)
            a_ref[rs, r0:r0 + GLA_SUB] = jnp.where(srow >= scol, d, 0.0)
            if bi > 0:
                ref_b = b[r0:r0 + 1]
                qt = qi * jnp.exp(bq - ref_b)
                kt = k[:r0] * jnp.exp(ref_b - b[:r0])
                a_ref[rs, :r0] = _dot_nt(_bf16(qt), _bf16(kt))
            if bi < nsub - 1:
                a_ref[rs, r0 + GLA_SUB:] = jnp.zeros((GLA_SUB, c - r0 - GLA_SUB), jnp.float32)
        o = o + _dot(_bf16(a_ref[...]), _bf16(v))

        bt = b.T
        bl = bt[:, c - 1:c]
        kdt = k.T * jnp.exp(bl - bt)
        s_ref[h] = jnp.exp(bl) * s + _dot(_bf16(kdt), _bf16(v))

        o_ref[:, vs] = _layer_norm(o, gam, bet) * sgg_ref[:, vs]


def _gla(q, k, lf, v, sgg, s0, gam, bet):
    B, T, _ = q.shape
    c = min(CHUNK, T)
    tokk = lambda b, i: (b, i, 0)
    st = lambda b, i: (b, 0, 0, 0)
    whole = lambda b, i: (0, 0)
    return pl.pallas_call(
        functools.partial(_gla_kernel, c=c),
        grid=(B, T // c),
        in_specs=[pl.BlockSpec((None, c, _GK), tokk), pl.BlockSpec((None, c, _GK), tokk),
                  pl.BlockSpec((None, c, _GK), tokk), pl.BlockSpec((None, c, _GV), tokk),
                  pl.BlockSpec((None, c, _GV), tokk),
                  pl.BlockSpec((None, GLA_HEADS, GLA_DK, GLA_DV), st),
                  pl.BlockSpec((1, GLA_DV), whole), pl.BlockSpec((1, GLA_DV), whole)],
        out_specs=[pl.BlockSpec((None, c, _GV), tokk),
                   pl.BlockSpec((None, GLA_HEADS, GLA_DK, GLA_DV), st)],
        out_shape=[jax.ShapeDtypeStruct((B, T, _GV), jnp.float32),
                   jax.ShapeDtypeStruct((B, GLA_HEADS, GLA_DK, GLA_DV), jnp.float32)],
        scratch_shapes=[pltpu.VMEM((c, c), jnp.float32)],
        compiler_params=pltpu.CompilerParams(dimension_semantics=("arbitrary", "arbitrary"),
                                             vmem_limit_bytes=VMEM_LIMIT),
        name="gla",
    )(q, k, lf, v, sgg, s0, gam, bet)


def _sortable_key(s):
    bits = pltpu.bitcast(s, jnp.int32)
    key = bits ^ ((bits >> 31) & 0x7FFFFFFF)
    return jnp.where(key == -1, 0, key)


def _dsa_kernel(qd_ref, qi_ref, wt_ref, k_ref, vt_ref, ki_ref, o_ref,
                key_ref, tie_ref, bias_ref, m_ref, acc_ref, lg_ref, *, tq, tk, past, seq_len, topk):
    t0 = pl.program_id(1) * tq
    total = past + seq_len
    qpos = past + t0 + lax.broadcasted_iota(jnp.int32, (1, tq), 1)
    limit = jnp.minimum(((qpos >> 6) + 1) * CHUNK, total)
    lim_max = jnp.minimum(((past + t0 + tq - 1) // CHUNK + 1) * CHUNK, total)
    nk = (lim_max + tk - 1) // tk
    lp = key_ref.shape[0] * tk
    key_iota = lax.broadcasted_iota(jnp.int32, (tk, tq), 0)

    qis = [_bf16(qi_ref[:, h * IDX_DIM:(h + 1) * IDX_DIM]) for h in range(IDX_HEADS)]
    neg_inf = jnp.float32(-jnp.inf)

    def score_tile(kt, carry):
        kit = ki_ref[pl.ds(pl.multiple_of(kt * tk, tk), tk), :]
        s = None
        for h in range(IDX_HEADS):
            term = wt_ref[h:h + 1, :] * jnp.maximum(_dot_nt(kit, qis[h]), 0.0)
            s = term if s is None else s + term
        s = jnp.where(key_iota < limit - kt * tk, s, neg_inf)
        key_ref[kt] = _sortable_key(s)
        return carry

    lax.fori_loop(0, nk, score_tile, 0)

    def count_ge(ref, cand):
        cb = jnp.broadcast_to(cand, (_CNT_ROWS, tq))

        def body(kt, acc):
            tile = ref[kt]
            for j in range(tk // _CNT_ROWS):
                acc = acc + jnp.where(tile[j * _CNT_ROWS:(j + 1) * _CNT_ROWS] >= cb, 1, 0)
            return acc

        acc = lax.fori_loop(0, nk, body, jnp.zeros((_CNT_ROWS, tq), jnp.int32))
        return jnp.sum(acc, axis=0, keepdims=True)

    def value_bit(i, thr):
        cand = thr ^ jnp.left_shift(jnp.int32(1), 31 - i)
        return jnp.where(count_ge(key_ref, cand) >= topk, cand, thr)

    int_min = jnp.int32(-2 ** 31)
    thr = lax.fori_loop(0, 32, value_bit, jnp.full((1, tq), int_min, jnp.int32))

    need = topk - count_ge(key_ref, thr + 1)

    def tie_tile(kt, carry):
        tie_ref[kt] = jnp.where(key_ref[kt] == thr, (lp - kt * tk) - key_iota, 0)
        return carry

    lax.fori_loop(0, nk, tie_tile, 0)
    nbits = lp.bit_length()

    def tie_bit(i, t2):
        cand = t2 | jnp.left_shift(jnp.int32(1), nbits - 1 - i)
        return jnp.where(count_ge(tie_ref, cand) >= need, cand, t2)

    t2 = lax.fori_loop(0, nbits, tie_bit, jnp.zeros((1, tq), jnp.int32))
    t2 = jnp.maximum(t2, 1)

    def select_tile(kt, carry):
        sel = ((key_ref[kt] > thr) | (tie_ref[kt] >= t2)) & (key_iota < limit - kt * tk)
        bias_ref[kt] = jnp.where(sel, 0.0, NEG_BIG)
        return carry

    lax.fori_loop(0, nk, select_tile, 0)

    c = (DSA_HEAD_DIM ** -0.5) * math.log2(math.e)
    rep = DSA_HEADS // DSA_KV_HEADS
    m_ref[...] = jnp.full(m_ref.shape, NEG_BIG, jnp.float32)
    acc_ref[...] = jnp.zeros(acc_ref.shape, jnp.float32)
    qs = [_bf16(qd_ref[:, h * DSA_HEAD_DIM:(h + 1) * DSA_HEAD_DIM]) for h in range(DSA_HEADS)]

    def attend_tile(kt, carry):
        rows = pl.ds(pl.multiple_of(kt * tk, tk), tk)
        bias = bias_ref[kt]

        def logits(h):
            g = h // rep
            return _dot_nt(k_ref[rows, g * DSA_HEAD_DIM:(g + 1) * DSA_HEAD_DIM], qs[h])

        for h in range(_LG_BUFS - 1):
            lg_ref[h] = logits(h)
        for h in range(DSA_HEADS):
            if h + _LG_BUFS - 1 < DSA_HEADS:
                lg_ref[(h + _LG_BUFS - 1) % _LG_BUFS] = logits(h + _LG_BUFS - 1)
            lb = lg_ref[h % _LG_BUFS] + bias
            m_old = m_ref[h]
            m_new = jnp.maximum(m_old, jnp.max(lb, axis=0, keepdims=True))
            p = jnp.exp2((lb - m_new) * c)
            a = jnp.exp2((m_old - m_new) * c)
            vtg = vt_ref[h // rep, kt]
            acc_ref[h] = a * acc_ref[h] + _dot(vtg, _bf16(p))
            m_ref[h] = m_new
        return carry

    lax.fori_loop(0, nk, attend_tile, 0)
    for h in range(DSA_HEADS):
        acc = acc_ref[h]
        o_t = acc[:DSA_HEAD_DIM] / acc[DSA_HEAD_DIM:DSA_HEAD_DIM + 1]
        o_ref[:, h * DSA_HEAD_DIM:(h + 1) * DSA_HEAD_DIM] = o_t.T


DSA_KEY_TILE = 512
_LG_BUFS = 4
_CNT_ROWS = 32
_VT_ROWS = DSA_HEAD_DIM + 16


def _dsa(qd, qi, wt, k_all, vt, ki_all, past, seq_len, tq, tk):
    B, tp, _ = qd.shape
    lp = k_all.shape[1]
    nkt = lp // tk
    topk = min(TOPK_MAX, (past + seq_len) // 4)
    tokk = lambda b, i: (b, i, 0)
    kv = lambda b, i: (b, 0, 0)
    return pl.pallas_call(
        functools.partial(_dsa_kernel, tq=tq, tk=tk, past=past, seq_len=seq_len, topk=topk),
        grid=(B, tp // tq),
        in_specs=[pl.BlockSpec((None, tq, _DQ), tokk), pl.BlockSpec((None, tq, _IQ), tokk),
                  pl.BlockSpec((None, IDX_HEADS, tq), lambda b, i: (b, 0, i)),
                  pl.BlockSpec((None, lp, _DKV), kv),
                  pl.BlockSpec((None, DSA_KV_HEADS, nkt, _VT_ROWS, tk), lambda b, i: (b, 0, 0, 0, 0)),
                  pl.BlockSpec((None, lp, IDX_DIM), kv)],
        out_specs=pl.BlockSpec((None, tq, _DQ), tokk),
        out_shape=jax.ShapeDtypeStruct((B, tp, _DQ), jnp.float32),
        scratch_shapes=[pltpu.VMEM((nkt, tk, tq), jnp.int32), pltpu.VMEM((nkt, tk, tq), jnp.int32),
                        pltpu.VMEM((nkt, tk, tq), jnp.float32),
                        pltpu.VMEM((DSA_HEADS, 1, tq), jnp.float32),
                        pltpu.VMEM((DSA_HEADS, _VT_ROWS, tq), jnp.float32),
                        pltpu.VMEM((_LG_BUFS, tk, tq), jnp.float32)],
        compiler_params=pltpu.CompilerParams(dimension_semantics=("arbitrary", "arbitrary"),
                                             vmem_limit_bytes=VMEM_LIMIT),
        name="dsa",
    )(qd, qi, wt, k_all, vt, ki_all)


def _merge_kernel(og_ref, od_ref, sga_ref, sgb_ref, x_ref, wg_ref, wd_ref, wo_ref, g_ref, b_ref, h_ref):
    m = sga_ref[...] * _dot(_bf16(og_ref[...]), wg_ref[...]) + sgb_ref[...] * _dot(_bf16(od_ref[...]), wd_ref[...])
    mix = _dot(_bf16(m), wo_ref[...])
    h_ref[...] = _layer_norm(ALPHA * x_ref[...] + mix, g_ref[...], b_ref[...])


def _merge(og, od, sga, sgb, x2d, wg, wd, wo, g, b, tm):
    n = x2d.shape[0]
    tok = lambda i: (i, 0)
    whole = lambda i: (0, 0)
    act = pl.BlockSpec((tm, D_MODEL), tok)
    wsp = pl.BlockSpec((D_MODEL, D_MODEL), whole)
    vec = pl.BlockSpec((1, D_MODEL), whole)
    return pl.pallas_call(
        _merge_kernel,
        grid=(n // tm,),
        in_specs=[act, act, act, act, act, wsp, wsp, wsp, vec, vec],
        out_specs=act,
        out_shape=jax.ShapeDtypeStruct((n, D_MODEL), jnp.float32),
        compiler_params=pltpu.CompilerParams(dimension_semantics=("arbitrary",), vmem_limit_bytes=VMEM_LIMIT),
        name="merge",
    )(og, od, sga, sgb, x2d, wg, wd, wo, g, b)


def _first_lane_where(mask, lane):
    return jnp.min(jnp.where(mask, lane, LANES), axis=-1, keepdims=True)


def _route(logits):
    lane = lax.broadcasted_iota(jnp.int32, logits.shape, 1)
    is_g = lane < N_GROUPS
    gl = jnp.where(is_g, logits, NEG_BIG)
    gmax = jnp.max(gl, axis=-1, keepdims=True)
    g_sel = _first_lane_where(is_g & (gl == gmax), lane)
    g_prob = 1.0 / jnp.sum(jnp.where(is_g, jnp.exp(gl - gmax), 0.0), axis=-1, keepdims=True)
    e_lane = lane - N_GROUPS
    in_grp = (e_lane >= 0) & (e_lane < N_EXPERTS) & ((e_lane >> 2) == g_sel)
    el = jnp.where(in_grp, logits, NEG_BIG)
    v1 = jnp.max(el, axis=-1, keepdims=True)
    i1 = _first_lane_where(in_grp & (el == v1), lane)
    rest = in_grp & (lane != i1)
    el2 = jnp.where(rest, logits, NEG_BIG)
    v2 = jnp.max(el2, axis=-1, keepdims=True)
    i2 = _first_lane_where(rest & (el2 == v2), lane)
    e2 = jnp.exp(v2 - v1)
    w1 = g_prob / (1.0 + e2)
    w2 = g_prob * e2 / (1.0 + e2)
    return jnp.where(lane == i1, w1, 0.0) + jnp.where(lane == i2, w2, 0.0)


def _moe_kernel(h_ref, wr_ref, br_ref, wg_ref, wu_ref, wd_ref, g_ref, b_ref, y_ref, gate_ref, acc_ref):
    e = pl.program_id(1)
    h = h_ref[...]

    @pl.when(e == 0)
    def _():
        logits = jnp.dot(h, wr_ref[...], preferred_element_type=jnp.float32,
                         precision=lax.Precision.HIGHEST) + br_ref[...]
        gate_ref[...] = _route(logits)
        acc_ref[...] = jnp.zeros(acc_ref.shape, jnp.float32)

    hb = _bf16(h)
    lane = lax.broadcasted_iota(jnp.int32, gate_ref.shape, 1)
    gcol = jnp.sum(jnp.where(lane == e + N_GROUPS, gate_ref[...], 0.0), axis=-1, keepdims=True)
    a = _dot(hb, wg_ref[...])
    hid = a * _sigmoid(a) * _dot(hb, wu_ref[...]) * gcol
    acc_ref[...] += _dot(_bf16(hid), wd_ref[...])

    @pl.when(e == N_EXPERTS - 1)
    def _():
        y_ref[...] = _layer_norm(ALPHA * h + acc_ref[...], g_ref[...], b_ref[...])


def _moe(h2d, wr, br, wg, wu, wd, g, b, tm):
    n = h2d.shape[0]
    tok = lambda i, e: (i, 0)
    whole = lambda i, e: (0, 0)
    per_e = lambda i, e: (e, 0, 0)
    return pl.pallas_call(
        _moe_kernel,
        grid=(n // tm, N_EXPERTS),
        in_specs=[pl.BlockSpec((tm, D_MODEL), tok),
                  pl.BlockSpec((D_MODEL, LANES), whole), pl.BlockSpec((1, LANES), whole),
                  pl.BlockSpec((None, D_MODEL, D_EXPERT), per_e),
                  pl.BlockSpec((None, D_MODEL, D_EXPERT), per_e),
                  pl.BlockSpec((None, D_EXPERT, D_MODEL), per_e),
                  pl.BlockSpec((1, D_MODEL), whole), pl.BlockSpec((1, D_MODEL), whole)],
        out_specs=pl.BlockSpec((tm, D_MODEL), tok),
        out_shape=jax.ShapeDtypeStruct((n, D_MODEL), jnp.float32),
        scratch_shapes=[pltpu.VMEM((tm, LANES), jnp.float32), pltpu.VMEM((tm, D_MODEL), jnp.float32)],
        compiler_params=pltpu.CompilerParams(dimension_semantics=("arbitrary", "arbitrary"),
                                             vmem_limit_bytes=VMEM_LIMIT),
        name="moe",
    )(h2d, wr, br, wg, wu, wd, g, b)


def _prep_weights(w_in, w_up, b_up, gla_g, gla_b, w_bg, w_bd, w_o, ln1_g, ln1_b, w_rg, b_rg, w_re, b_re,
                  w_eg, w_eu, w_ed, ln2_g, ln2_b):
    w, wup, bup = _prep_w_in(w_in, w_up, b_up)
    w_re2 = jnp.transpose(w_re, (1, 0, 2)).reshape(D_MODEL, N_EXPERTS)
    rpad = jnp.zeros((D_MODEL, LANES - N_GROUPS - N_EXPERTS), w_rg.dtype)
    wr = jnp.concatenate([w_rg, w_re2, rpad], axis=1)
    br = jnp.concatenate([b_rg, b_re.reshape(N_EXPERTS), jnp.zeros((LANES - N_GROUPS - N_EXPERTS,), b_rg.dtype)])
    row = lambda a: a.reshape(1, -1)
    return dict(w=w, wup=wup, bup=bup, gla_g=row(gla_g), gla_b=row(gla_b), wbg=_bf16(w_bg), wbd=_bf16(w_bd),
                wo=_bf16(w_o), ln1_g=row(ln1_g), ln1_b=row(ln1_b), wr=wr, br=row(br), weg=_bf16(w_eg),
                weu=_bf16(w_eu), wed=_bf16(w_ed), ln2_g=row(ln2_g), ln2_b=row(ln2_b))


def _token_tile(n):
    for tm in (512, 256, 128):
        if n % tm == 0:
            return tm
    raise ValueError(f"token count {n} is not a multiple of 128")


def _trunk_layer(x, s0, past_k, past_v, past_ki, p):
    B, T, _ = x.shape
    past = past_k.shape[1]
    n = B * T
    tm = _token_tile(n)
    assert T % tm == 0 or tm % T == 0
    x2d = x.reshape(n, D_MODEL)
    gq, gk, gv, sgg, lf, dq, dk, dv, iq, sga, sgb, misc = _in_proj(x2d, p["w"], p["wup"], p["bup"], T, past, tm)
    seq = lambda a: a.reshape(B, T, a.shape[-1])

    o_gla, s_fin = _gla(seq(gq), seq(gk), seq(lf), seq(gv), seq(sgg), s0, p["gla_g"], p["gla_b"])

    k_new, v_new, ki_new = seq(dk), seq(dv), seq(misc)[:, :, :IDX_DIM]
    total = past + T
    tk = DSA_KEY_TILE
    lp = -(-total // tk) * tk

    def keys(past_rows, new_rows):
        rows = jnp.concatenate([past_rows.reshape(B, past, new_rows.shape[-1]), new_rows], axis=1)
        return _bf16(jnp.pad(rows, ((0, 0), (0, lp - total), (0, 0))))

    v_all = keys(past_v, v_new).reshape(B, lp // tk, tk, DSA_KV_HEADS, DSA_HEAD_DIM)
    ones = jnp.ones((B, lp // tk, tk, DSA_KV_HEADS, 1), jnp.bfloat16)
    zpad = jnp.zeros((B, lp // tk, tk, DSA_KV_HEADS, _VT_ROWS - DSA_HEAD_DIM - 1), jnp.bfloat16)
    vt = jnp.transpose(jnp.concatenate([v_all, ones, zpad], axis=-1), (0, 3, 1, 4, 2))

    tq = 256 if T % 256 == 0 else LANES
    tp = -(-T // tq) * tq
    qpad = lambda a: jnp.pad(a, ((0, 0), (0, tp - T), (0, 0)))
    wt = jnp.transpose(qpad(seq(misc)[:, :, _MISC_IW:_MISC_IW + IDX_HEADS]), (0, 2, 1))
    o_dsa = _dsa(qpad(seq(dq)), qpad(seq(iq)), wt, keys(past_k, k_new), vt, keys(past_ki, ki_new),
                 past, T, tq, tk)[:, :T]

    h = _merge(o_gla.reshape(n, _GV), o_dsa.reshape(n, _DQ), sga, sgb, x2d, p["wbg"], p["wbd"], p["wo"],
               p["ln1_g"], p["ln1_b"], tm)
    y = _moe(h, p["wr"], p["br"], p["weg"], p["weu"], p["wed"], p["ln2_g"], p["ln2_b"], tm)
    return (y.reshape(B, T, D_MODEL), s_fin, k_new.reshape(B, T, DSA_KV_HEADS, DSA_HEAD_DIM),
            v_new.reshape(B, T, DSA_KV_HEADS, DSA_HEAD_DIM), ki_new)


def kernel(x_prompt, x_sample, state_gla, cache_k, cache_v, cache_k_idx, w_in, w_gla_gate_up, b_gla_gate,
           gla_norm_g, gla_norm_b, w_branch_gla, w_branch_dsa, w_out, ln1_g, ln1_b, w_router_group,
           b_router_group, w_router_expert, b_router_expert, w_expert_gate, w_expert_up, w_expert_down,
           ln2_g, ln2_b):
    B = x_prompt.shape[0]
    l = 0
    p = _prep_weights(w_in[l], w_gla_gate_up[l], b_gla_gate[l], gla_norm_g[l], gla_norm_b[l], w_branch_gla[l],
                      w_branch_dsa[l], w_out[l], ln1_g[l], ln1_b[l], w_router_group[l], b_router_group[l],
                      w_router_expert[l], b_router_expert[l], w_expert_gate[l], w_expert_up[l],
                      w_expert_down[l], ln2_g[l], ln2_b[l])
    s0 = jnp.zeros((B, GLA_HEADS, GLA_DK, GLA_DV), jnp.float32)
    ek = jnp.zeros((B, 0, DSA_KV_HEADS, DSA_HEAD_DIM), x_prompt.dtype)
    eki = jnp.zeros((B, 0, IDX_DIM), x_prompt.dtype)
    yp, sp, kp, vp, kip = _trunk_layer(x_prompt, s0, ek, ek, eki, p)
    ys, ss, ksn, vsn, kisn = _trunk_layer(x_sample, state_gla[l], cache_k[l], cache_v[l], cache_k_idx[l], p)
    stack = lambda a: a[None]
    return (yp, ys, stack(sp), stack(kp), stack(vp), stack(kip), stack(ss), stack(ksn), stack(vsn), stack(kisn))
```

```python
import functools
import math

import jax
import jax.numpy as jnp
from jax import lax
from jax.experimental import pallas as pl
from jax.experimental.pallas import tpu as pltpu

D_MODEL = 1024
CHUNK = 64
ROPE_THETA = 10000.0
LN_EPS = 1e-5
GLA_HEADS = 4
GLA_DK = D_MODEL // 2 // GLA_HEADS
GLA_DV = D_MODEL // GLA_HEADS
GLA_GATE_RANK = 16
GLA_TAU = 16.0
DSA_HEADS = 8
DSA_KV_HEADS = 2
DSA_HEAD_DIM = D_MODEL // DSA_HEADS
IDX_HEADS = 8
IDX_DIM = 64
TOPK_MAX = 256
IDX_W_SCALE = (IDX_HEADS ** -0.5) * (IDX_DIM ** -0.5)
N_GROUPS = 4
EXPERTS_PER_GROUP = 4
N_EXPERTS = N_GROUPS * EXPERTS_PER_GROUP
D_EXPERT = 256
TOP_K_IN_GROUP = 2
DEPTH = 1
ALPHA = (2 * DEPTH) ** 0.25

LANES = 128
GLA_SUB = 16
GLA_FACTORED_MAX_DECAY = 60.0
VMEM_LIMIT = 56 * 1024 * 1024
NEG_BIG = -1e30
DSA_Q_DTYPE = jnp.bfloat16

_GK = GLA_HEADS * GLA_DK
_GV = GLA_HEADS * GLA_DV
_DQ = DSA_HEADS * DSA_HEAD_DIM
_DKV = DSA_KV_HEADS * DSA_HEAD_DIM
_IQ = IDX_HEADS * IDX_DIM
_C_GQ = 0
_C_GK = _C_GQ + _GK
_C_GV = _C_GK + _GK
_C_GG = _C_GV + _GV
_C_DQ = _C_GG + _GV
_C_DK = _C_DQ + _DQ
_C_DV = _C_DK + _DKV
_C_IQ = _C_DV + _DKV
_C_GA = _C_IQ + _IQ
_C_GB = _C_GA + D_MODEL
_C_MISC = _C_GB + D_MODEL
_C_END = _C_MISC + LANES
_MISC_GR = IDX_DIM
_MISC_IW = IDX_DIM + GLA_GATE_RANK


def _bf16(a):
    return a.astype(jnp.bfloat16)


def _dot(a, b):
    return jnp.dot(a, b, preferred_element_type=jnp.float32)


def _dot_nt(a, b):
    return lax.dot_general(a, b, (((1,), (1,)), ((), ())), preferred_element_type=jnp.float32)


def _sigmoid(x):
    return 1.0 / (1.0 + jnp.exp(-x))


def _layer_norm(x, g, b):
    mu = jnp.mean(x, axis=-1, keepdims=True)
    xc = x - mu
    var = jnp.mean(xc * xc, axis=-1, keepdims=True)
    return xc * lax.rsqrt(var + LN_EPS) * g + b


def _rope_full(z, cos, sin):
    return z * cos + pltpu.roll(z, DSA_HEAD_DIM // 2, axis=1) * sin


def _rope_pair(z, cos, sin, first_half):
    q = IDX_DIM // 2
    partner = jnp.where(first_half, pltpu.roll(z, LANES - q, axis=1), pltpu.roll(z, q, axis=1))
    return z * cos + partner * sin


def _in_proj_kernel(x_ref, w_ref, wup_ref, bup_ref, cos_ref, sin_ref, cosi_ref, sini_ref,
                    gq_ref, gk_ref, gv_ref, sgg_ref, lf_ref, dq_ref, dk_ref, dv_ref, iq_ref,
                    sga_ref, sgb_ref, misc_ref):
    xb = _bf16(x_ref[...])

    def proj(c0, width):
        return _dot(xb, w_ref[:, c0:c0 + width])

    gq_ref[...] = proj(_C_GQ, _GK) * (GLA_DK ** -0.5)
    gk_ref[...] = proj(_C_GK, _GK)
    gv_ref[...] = proj(_C_GV, _GV)
    gg = proj(_C_GG, _GV)
    sgg_ref[...] = gg * _sigmoid(gg)
    sga_ref[...] = _sigmoid(proj(_C_GA, D_MODEL))
    sgb_ref[...] = _sigmoid(proj(_C_GB, D_MODEL))
    dv_ref[...] = proj(_C_DV, _DKV)

    cos = cos_ref[...]
    sin = sin_ref[...]
    for h in range(DSA_HEADS):
        z = proj(_C_DQ + h * DSA_HEAD_DIM, DSA_HEAD_DIM)
        dq_ref[:, h * DSA_HEAD_DIM:(h + 1) * DSA_HEAD_DIM] = _rope_full(z, cos, sin).astype(dq_ref.dtype)
    for h in range(DSA_KV_HEADS):
        z = proj(_C_DK + h * DSA_HEAD_DIM, DSA_HEAD_DIM)
        dk_ref[:, h * DSA_HEAD_DIM:(h + 1) * DSA_HEAD_DIM] = _rope_full(z, cos, sin)

    cosi = cosi_ref[...]
    sini = sini_ref[...]
    lane = lax.broadcasted_iota(jnp.int32, cosi.shape, 1)
    first_half = (lane & (IDX_DIM - 1)) < (IDX_DIM // 2)
    for p in range(_IQ // LANES):
        z = proj(_C_IQ + p * LANES, LANES)
        iq_ref[:, p * LANES:(p + 1) * LANES] = _rope_pair(z, cosi, sini, first_half).astype(iq_ref.dtype)

    misc = proj(_C_MISC, LANES)
    pre = _dot(_bf16(misc), wup_ref[...]) + bup_ref[...]
    lf_ref[...] = (jnp.minimum(pre, 0.0) - jnp.log(1.0 + jnp.exp(-jnp.abs(pre)))) * (1.0 / GLA_TAU)
    roped = _rope_pair(misc, cosi, sini, first_half)
    misc_ref[...] = jnp.where(lane < IDX_DIM, roped, misc * IDX_W_SCALE)


def _rope_tables(T, past, rows):
    pos = (past + jnp.arange(T, dtype=jnp.int32)).astype(jnp.float32)

    def tables(dim):
        half = dim // 2
        inv = ROPE_THETA ** (-jnp.arange(half, dtype=jnp.float32) / half)
        ang = pos[:, None] * inv[None, :]
        c, s = jnp.cos(ang), jnp.sin(ang)
        reps = LANES // dim
        cos = jnp.tile(jnp.concatenate([c, c], axis=1), (1, reps))
        sin = jnp.tile(jnp.concatenate([-s, s], axis=1), (1, reps))
        if rows > T:
            cos = jnp.tile(cos, (rows // T, 1))
            sin = jnp.tile(sin, (rows // T, 1))
        return cos, sin

    return tables(DSA_HEAD_DIM) + tables(IDX_DIM)


def _prep_w_in(w_in, w_up, b_up):
    sizes = (_GK, _GK, _GV, _GV, GLA_GATE_RANK, _DQ, _DKV, _DKV, _IQ, IDX_DIM, IDX_HEADS, D_MODEL, D_MODEL)
    offs = [0]
    for s in sizes:
        offs.append(offs[-1] + s)
    gq, gk, gv, gg, gr, dq, dk, dv, iq, ik, iw, ga, gb = (w_in[:, offs[i]:offs[i + 1]] for i in range(len(sizes)))
    pad = jnp.zeros((D_MODEL, LANES - IDX_DIM - GLA_GATE_RANK - IDX_HEADS), w_in.dtype)
    w = jnp.concatenate([gq, gk, gv, gg, dq, dk, dv, iq, ga, gb, ik, gr, iw, pad], axis=1)
    wup = jnp.zeros((LANES, _GK), w_up.dtype).at[_MISC_GR:_MISC_GR + GLA_GATE_RANK].set(w_up)
    return _bf16(w), _bf16(wup), b_up.reshape(1, _GK)


def _in_proj(x2d, w, wup, bup, T, past, tm):
    n = x2d.shape[0]
    rows = max(T, tm)
    cos, sin, cosi, sini = _rope_tables(T, past, rows)
    nt = rows // tm
    tok = lambda i: (i, 0)
    tab = lambda i: (i % nt, 0)
    whole = lambda i: (0, 0)
    widths = (_GK, _GK, _GV, _GV, _GK, _DQ, _DKV, _DKV, _IQ, D_MODEL, D_MODEL, LANES)
    f32 = jnp.float32
    dtypes = (f32, f32, f32, f32, f32, DSA_Q_DTYPE, f32, f32, DSA_Q_DTYPE, f32, f32, f32)
    return pl.pallas_call(
        _in_proj_kernel,
        grid=(n // tm,),
        in_specs=[pl.BlockSpec((tm, D_MODEL), tok),
                  pl.BlockSpec((D_MODEL, _C_END), whole),
                  pl.BlockSpec((LANES, _GK), whole),
                  pl.BlockSpec((1, _GK), whole),
                  pl.BlockSpec((tm, LANES), tab), pl.BlockSpec((tm, LANES), tab),
                  pl.BlockSpec((tm, LANES), tab), pl.BlockSpec((tm, LANES), tab)],
        out_specs=[pl.BlockSpec((tm, wd), tok) for wd in widths],
        out_shape=[jax.ShapeDtypeStruct((n, wd), dt) for wd, dt in zip(widths, dtypes)],
        compiler_params=pltpu.CompilerParams(dimension_semantics=("arbitrary",), vmem_limit_bytes=VMEM_LIMIT),
        name="in_proj",
    )(x2d, w, wup, bup, cos, sin, cosi, sini)


def _split3(a):
    hi = _bf16(a)
    r1 = a - hi.astype(jnp.float32)
    mid = _bf16(r1)
    lo = _bf16(r1 - mid.astype(jnp.float32))
    return hi, mid, lo


def _gla_kernel(q_ref, k_ref, lf_ref, v_ref, sgg_ref, s0_ref, g_ref, b_ref, o_ref, s_ref, a_ref, *, c):
    ci = pl.program_id(1)

    @pl.when(ci == 0)
    def _():
        s_ref[...] = s0_ref[...]

    nsub = c // GLA_SUB
    row = lax.broadcasted_iota(jnp.int32, (c, c), 0)
    col = lax.broadcasted_iota(jnp.int32, (c, c), 1)
    causal = row >= col
    tri = _bf16(jnp.where(causal, 1.0, 0.0))
    srow = lax.broadcasted_iota(jnp.int32, (GLA_SUB, GLA_SUB), 0)
    scol = lax.broadcasted_iota(jnp.int32, (GLA_SUB, GLA_SUB), 1)
    gam = g_ref[...]
    bet = b_ref[...]

    lf = lf_ref[...]
    hi, mid, lo = _split3(lf)
    b_all = _dot(tri, hi) + _dot(tri, mid) + _dot(tri, lo)
    chunk_decay = jnp.max(-jnp.sum(lf, axis=0, keepdims=True))

    def intra_factored(q, k, b):
        b0 = b[0:1]
        a = _dot_nt(_bf16(q * jnp.exp(b - b0)), _bf16(k * jnp.exp(b0 - b)))
        return jnp.where(causal, a, 0.0)

    def intra_exact(q, k, b):
        for bi in range(nsub):
            r0 = bi * GLA_SUB
            rs = slice(r0, r0 + GLA_SUB)
            qi = q[rs]
            bq = b[rs]
            ki = k[rs]
            d = jnp.zeros((GLA_SUB, GLA_SUB), jnp.float32)
            for j in range(GLA_SUB):
                f = jnp.exp(jnp.minimum(bq - bq[j:j + 1], 0.0)) * qi * ki[j:j + 1]
                d = jnp.where(scol == j, jnp.sum(f, axis=-1, keepdims=True), d)
            a_ref[rs, r0:r0 + GLA_SUB] = jnp.where(srow >= scol, d, 0.0)
            if bi > 0:
                ref_b = b[r0:r0 + 1]
                qt = qi * jnp.exp(bq - ref_b)
                kt = k[:r0] * jnp.exp(ref_b - b[:r0])
                a_ref[rs, :r0] = _dot_nt(_bf16(qt), _bf16(kt))
            if bi < nsub - 1:
                a_ref[rs, r0 + GLA_SUB:] = jnp.zeros((GLA_SUB, c - r0 - GLA_SUB), jnp.float32)
        return a_ref[...]

    def heads(intra):
        for h in range(GLA_HEADS):
            ks = slice(h * GLA_DK, (h + 1) * GLA_DK)
            vs = slice(h * GLA_DV, (h + 1) * GLA_DV)
            q = q_ref[:, ks]
            k = k_ref[:, ks]
            v = _bf16(v_ref[:, vs])
            b = b_all[:, ks]
            s = s_ref[h]
            o = _dot(_bf16(q * jnp.exp(b)), _bf16(s)) + _dot(_bf16(intra(q, k, b)), v)

            bt = b.T
            bl = bt[:, c - 1:c]
            kdt = k.T * jnp.exp(bl - bt)
            s_ref[h] = jnp.exp(bl) * s + _dot(_bf16(kdt), v)

            o_ref[:, vs] = _layer_norm(o, gam, bet) * sgg_ref[:, vs]

    small_decay = chunk_decay <= GLA_FACTORED_MAX_DECAY

    @pl.when(small_decay)
    def _():
        heads(intra_factored)

    @pl.when(jnp.logical_not(small_decay))
    def _():
        heads(intra_exact)


def _gla(q, k, lf, v, sgg, s0, gam, bet):
    B, T, _ = q.shape
    c = min(CHUNK, T)
    tokk = lambda b, i: (b, i, 0)
    st = lambda b, i: (b, 0, 0, 0)
    whole = lambda b, i: (0, 0)
    return pl.pallas_call(
        functools.partial(_gla_kernel, c=c),
        grid=(B, T // c),
        in_specs=[pl.BlockSpec((None, c, _GK), tokk), pl.BlockSpec((None, c, _GK), tokk),
                  pl.BlockSpec((None, c, _GK), tokk), pl.BlockSpec((None, c, _GV), tokk),
                  pl.BlockSpec((None, c, _GV), tokk),
                  pl.BlockSpec((None, GLA_HEADS, GLA_DK, GLA_DV), st),
                  pl.BlockSpec((1, GLA_DV), whole), pl.BlockSpec((1, GLA_DV), whole)],
        out_specs=[pl.BlockSpec((None, c, _GV), tokk),
                   pl.BlockSpec((None, GLA_HEADS, GLA_DK, GLA_DV), st)],
        out_shape=[jax.ShapeDtypeStruct((B, T, _GV), jnp.float32),
                   jax.ShapeDtypeStruct((B, GLA_HEADS, GLA_DK, GLA_DV), jnp.float32)],
        scratch_shapes=[pltpu.VMEM((c, c), jnp.float32)],
        compiler_params=pltpu.CompilerParams(dimension_semantics=("arbitrary", "arbitrary"),
                                             vmem_limit_bytes=VMEM_LIMIT),
        name="gla",
    )(q, k, lf, v, sgg, s0, gam, bet)


def _sortable_key(s):
    bits = pltpu.bitcast(s, jnp.int32)
    key = bits ^ ((bits >> 31) & 0x7FFFFFFF)
    return jnp.where(key == -1, 0, key)


def _dsa_kernel(qd_ref, qi_ref, wt_ref, k_ref, vt_ref, ki_ref, o_ref,
                key_ref, tie_ref, bias_ref, m_ref, acc_ref, lg_ref, *, tq, tk, past, seq_len, topk):
    t0 = pl.program_id(1) * tq
    total = past + seq_len
    qpos = past + t0 + lax.broadcasted_iota(jnp.int32, (1, tq), 1)
    limit = jnp.minimum(((qpos >> 6) + 1) * CHUNK, total)
    lim_max = jnp.minimum(((past + t0 + tq - 1) // CHUNK + 1) * CHUNK, total)
    nk = (lim_max + tk - 1) // tk
    lp = key_ref.shape[0] * tk
    key_iota = lax.broadcasted_iota(jnp.int32, (tk, tq), 0)

    qis = [_bf16(qi_ref[:, h * IDX_DIM:(h + 1) * IDX_DIM]) for h in range(IDX_HEADS)]
    neg_inf = jnp.float32(-jnp.inf)

    def score_tile(kt, carry):
        kit = ki_ref[pl.ds(pl.multiple_of(kt * tk, tk), tk), :]
        s = None
        for h in range(IDX_HEADS):
            term = wt_ref[h:h + 1, :] * jnp.maximum(_dot_nt(kit, qis[h]), 0.0)
            s = term if s is None else s + term
        s = jnp.where(key_iota < limit - kt * tk, s, neg_inf)
        key_ref[kt] = _sortable_key(s)
        return carry

    lax.fori_loop(0, nk, score_tile, 0)

    def count_ge(ref, cand):
        cb = jnp.broadcast_to(cand, (_CNT_ROWS, tq))

        def body(kt, acc):
            tile = ref[kt]
            for j in range(tk // _CNT_ROWS):
                acc = acc + jnp.where(tile[j * _CNT_ROWS:(j + 1) * _CNT_ROWS] >= cb, 1, 0)
            return acc

        acc = lax.fori_loop(0, nk, body, jnp.zeros((_CNT_ROWS, tq), jnp.int32))
        return jnp.sum(acc, axis=0, keepdims=True)

    def search_cond(state):
        i, _, _, open_lanes = state
        return (i < 32) & (open_lanes > 0)

    def search_bit(state):
        i, thr, settled, _ = state
        for j in range(_BITS_PER_CHECK):
            cand = thr ^ jnp.left_shift(jnp.int32(1), 31 - (i + j))
            cnt = count_ge(key_ref, cand)
            thr = jnp.where((settled > 0) | (cnt < topk), thr, cand)
            settled = jnp.where(cnt == topk, 1, settled)
        return i + _BITS_PER_CHECK, thr, settled, 1 - jnp.min(settled)

    int_min = jnp.int32(-2 ** 31)
    settled0 = jnp.where(limit <= topk, 1, 0)
    _, thr, settled, open_lanes = lax.while_loop(
        search_cond, search_bit,
        (jnp.int32(0), jnp.full((1, tq), int_min, jnp.int32), settled0, 1 - jnp.min(settled0)))
    settled = settled > 0

    @pl.when(open_lanes == 0)
    def _():
        def select_tile(kt, carry):
            sel = (key_ref[kt] >= thr) & (key_iota < limit - kt * tk)
            bias_ref[kt] = jnp.where(sel, 0.0, NEG_BIG)
            return carry

        lax.fori_loop(0, nk, select_tile, 0)

    @pl.when(open_lanes > 0)
    def _():
        need = topk - count_ge(key_ref, thr + 1)

        def tie_tile(kt, carry):
            tie_ref[kt] = jnp.where(key_ref[kt] == thr, (lp - kt * tk) - key_iota, 0)
            return carry

        lax.fori_loop(0, nk, tie_tile, 0)
        nbits = lp.bit_length()

        def tie_bit(i, t2):
            cand = t2 | jnp.left_shift(jnp.int32(1), nbits - 1 - i)
            return jnp.where(count_ge(tie_ref, cand) >= need, cand, t2)

        t2 = lax.fori_loop(0, nbits, tie_bit, jnp.zeros((1, tq), jnp.int32))
        t2 = jnp.where(settled, 1, jnp.maximum(t2, 1))

        def select_tile(kt, carry):
            sel = ((key_ref[kt] > thr) | (tie_ref[kt] >= t2)) & (key_iota < limit - kt * tk)
            bias_ref[kt] = jnp.where(sel, 0.0, NEG_BIG)
            return carry

        lax.fori_loop(0, nk, select_tile, 0)

    c = (DSA_HEAD_DIM ** -0.5) * math.log2(math.e)
    rep = DSA_HEADS // DSA_KV_HEADS
    m_ref[...] = jnp.full(m_ref.shape, NEG_BIG, jnp.float32)
    acc_ref[...] = jnp.zeros(acc_ref.shape, jnp.float32)
    qs = [_bf16(qd_ref[:, h * DSA_HEAD_DIM:(h + 1) * DSA_HEAD_DIM]) for h in range(DSA_HEADS)]

    def attend_tile(kt, carry):
        rows = pl.ds(pl.multiple_of(kt * tk, tk), tk)
        bias = bias_ref[kt]

        def logits(h):
            g = h // rep
            return _dot_nt(k_ref[rows, g * DSA_HEAD_DIM:(g + 1) * DSA_HEAD_DIM], qs[h])

        for h in range(_LG_BUFS - 1):
            lg_ref[h] = logits(h)
        for h in range(DSA_HEADS):
            if h + _LG_BUFS - 1 < DSA_HEADS:
                lg_ref[(h + _LG_BUFS - 1) % _LG_BUFS] = logits(h + _LG_BUFS - 1)
            lb = lg_ref[h % _LG_BUFS] + bias
            m_old = m_ref[h]
            m_new = jnp.maximum(m_old, jnp.max(lb, axis=0, keepdims=True))
            p = jnp.exp2((lb - m_new) * c)
            a = jnp.exp2((m_old - m_new) * c)
            vtg = vt_ref[h // rep, kt]
            acc_ref[h] = a * acc_ref[h] + _dot(vtg, _bf16(p))
            m_ref[h] = m_new
        return carry

    lax.fori_loop(0, nk, attend_tile, 0)
    for h in range(DSA_HEADS):
        acc = acc_ref[h]
        o_t = acc[:DSA_HEAD_DIM] / acc[DSA_HEAD_DIM:DSA_HEAD_DIM + 1]
        o_ref[:, h * DSA_HEAD_DIM:(h + 1) * DSA_HEAD_DIM] = o_t.T


DSA_KEY_TILE = 512
_BITS_PER_CHECK = 4
_LG_BUFS = 4
_CNT_ROWS = 32
_VT_ROWS = DSA_HEAD_DIM + 16


def _dsa(qd, qi, wt, k_all, vt, ki_all, past, seq_len, tq, tk):
    B, tp, _ = qd.shape
    lp = k_all.shape[1]
    nkt = lp // tk
    topk = min(TOPK_MAX, (past + seq_len) // 4)
    tokk = lambda b, i: (b, i, 0)
    kv = lambda b, i: (b, 0, 0)
    return pl.pallas_call(
        functools.partial(_dsa_kernel, tq=tq, tk=tk, past=past, seq_len=seq_len, topk=topk),
        grid=(B, tp // tq),
        in_specs=[pl.BlockSpec((None, tq, _DQ), tokk), pl.BlockSpec((None, tq, _IQ), tokk),
                  pl.BlockSpec((None, IDX_HEADS, tq), lambda b, i: (b, 0, i)),
                  pl.BlockSpec((None, lp, _DKV), kv),
                  pl.BlockSpec((None, DSA_KV_HEADS, nkt, _VT_ROWS, tk), lambda b, i: (b, 0, 0, 0, 0)),
                  pl.BlockSpec((None, lp, IDX_DIM), kv)],
        out_specs=pl.BlockSpec((None, tq, _DQ), tokk),
        out_shape=jax.ShapeDtypeStruct((B, tp, _DQ), jnp.float32),
        scratch_shapes=[pltpu.VMEM((nkt, tk, tq), jnp.int32), pltpu.VMEM((nkt, tk, tq), jnp.int32),
                        pltpu.VMEM((nkt, tk, tq), jnp.float32),
                        pltpu.VMEM((DSA_HEADS, 1, tq), jnp.float32),
                        pltpu.VMEM((DSA_HEADS, _VT_ROWS, tq), jnp.float32),
                        pltpu.VMEM((_LG_BUFS, tk, tq), jnp.float32)],
        compiler_params=pltpu.CompilerParams(dimension_semantics=("arbitrary", "arbitrary"),
                                             vmem_limit_bytes=VMEM_LIMIT),
        name="dsa",
    )(qd, qi, wt, k_all, vt, ki_all)


def _merge_kernel(og_ref, od_ref, sga_ref, sgb_ref, x_ref, wg_ref, wd_ref, wo_ref, g_ref, b_ref, h_ref):
    m = sga_ref[...] * _dot(_bf16(og_ref[...]), wg_ref[...]) + sgb_ref[...] * _dot(_bf16(od_ref[...]), wd_ref[...])
    mix = _dot(_bf16(m), wo_ref[...])
    h_ref[...] = _layer_norm(ALPHA * x_ref[...] + mix, g_ref[...], b_ref[...])


def _merge(og, od, sga, sgb, x2d, wg, wd, wo, g, b, tm):
    n = x2d.shape[0]
    tok = lambda i: (i, 0)
    whole = lambda i: (0, 0)
    act = pl.BlockSpec((tm, D_MODEL), tok)
    wsp = pl.BlockSpec((D_MODEL, D_MODEL), whole)
    vec = pl.BlockSpec((1, D_MODEL), whole)
    return pl.pallas_call(
        _merge_kernel,
        grid=(n // tm,),
        in_specs=[act, act, act, act, act, wsp, wsp, wsp, vec, vec],
        out_specs=act,
        out_shape=jax.ShapeDtypeStruct((n, D_MODEL), jnp.float32),
        compiler_params=pltpu.CompilerParams(dimension_semantics=("arbitrary",), vmem_limit_bytes=VMEM_LIMIT),
        name="merge",
    )(og, od, sga, sgb, x2d, wg, wd, wo, g, b)


def _first_lane_where(mask, lane):
    return jnp.min(jnp.where(mask, lane, LANES), axis=-1, keepdims=True)


def _route(logits):
    lane = lax.broadcasted_iota(jnp.int32, logits.shape, 1)
    is_g = lane < N_GROUPS
    gl = jnp.where(is_g, logits, NEG_BIG)
    gmax = jnp.max(gl, axis=-1, keepdims=True)
    g_sel = _first_lane_where(is_g & (gl == gmax), lane)
    g_prob = 1.0 / jnp.sum(jnp.where(is_g, jnp.exp(gl - gmax), 0.0), axis=-1, keepdims=True)
    e_lane = lane - N_GROUPS
    in_grp = (e_lane >= 0) & (e_lane < N_EXPERTS) & ((e_lane >> 2) == g_sel)
    el = jnp.where(in_grp, logits, NEG_BIG)
    v1 = jnp.max(el, axis=-1, keepdims=True)
    i1 = _first_lane_where(in_grp & (el == v1), lane)
    rest = in_grp & (lane != i1)
    el2 = jnp.where(rest, logits, NEG_BIG)
    v2 = jnp.max(el2, axis=-1, keepdims=True)
    i2 = _first_lane_where(rest & (el2 == v2), lane)
    e2 = jnp.exp(v2 - v1)
    w1 = g_prob / (1.0 + e2)
    w2 = g_prob * e2 / (1.0 + e2)
    return jnp.where(lane == i1, w1, 0.0) + jnp.where(lane == i2, w2, 0.0)


def _moe_kernel(h_ref, wr_ref, br_ref, wg_ref, wu_ref, wd_ref, g_ref, b_ref, y_ref, gate_ref, acc_ref):
    e = pl.program_id(1)
    h = h_ref[...]

    hb = _bf16(h)

    @pl.when(e == 0)
    def _():
        h_lo = _bf16(h - hb.astype(jnp.float32))
        logits = _dot(hb, wr_ref[0]) + (_dot(h_lo, wr_ref[0]) + _dot(hb, wr_ref[1])) + br_ref[...]
        gate_ref[...] = _route(logits)
        acc_ref[...] = jnp.zeros(acc_ref.shape, jnp.float32)

    lane = lax.broadcasted_iota(jnp.int32, gate_ref.shape, 1)
    gcol = jnp.sum(jnp.where(lane == e + N_GROUPS, gate_ref[...], 0.0), axis=-1, keepdims=True)
    a = _dot(hb, wg_ref[...])
    hid = a * _sigmoid(a) * _dot(hb, wu_ref[...]) * gcol
    acc_ref[...] += _dot(_bf16(hid), wd_ref[...])

    @pl.when(e == N_EXPERTS - 1)
    def _():
        y_ref[...] = _layer_norm(ALPHA * h + acc_ref[...], g_ref[...], b_ref[...])


def _moe(h2d, wr, br, wg, wu, wd, g, b, tm):
    n = h2d.shape[0]
    tok = lambda i, e: (i, 0)
    whole = lambda i, e: (0, 0)
    per_e = lambda i, e: (e, 0, 0)
    return pl.pallas_call(
        _moe_kernel,
        grid=(n // tm, N_EXPERTS),
        in_specs=[pl.BlockSpec((tm, D_MODEL), tok),
                  pl.BlockSpec((2, D_MODEL, LANES), lambda i, e: (0, 0, 0)), pl.BlockSpec((1, LANES), whole),
                  pl.BlockSpec((None, D_MODEL, D_EXPERT), per_e),
                  pl.BlockSpec((None, D_MODEL, D_EXPERT), per_e),
                  pl.BlockSpec((None, D_EXPERT, D_MODEL), per_e),
                  pl.BlockSpec((1, D_MODEL), whole), pl.BlockSpec((1, D_MODEL), whole)],
        out_specs=pl.BlockSpec((tm, D_MODEL), tok),
        out_shape=jax.ShapeDtypeStruct((n, D_MODEL), jnp.float32),
        scratch_shapes=[pltpu.VMEM((tm, LANES), jnp.float32), pltpu.VMEM((tm, D_MODEL), jnp.float32)],
        compiler_params=pltpu.CompilerParams(dimension_semantics=("arbitrary", "arbitrary"),
                                             vmem_limit_bytes=VMEM_LIMIT),
        name="moe",
    )(h2d, wr, br, wg, wu, wd, g, b)


def _prep_weights(w_in, w_up, b_up, gla_g, gla_b, w_bg, w_bd, w_o, ln1_g, ln1_b, w_rg, b_rg, w_re, b_re,
                  w_eg, w_eu, w_ed, ln2_g, ln2_b):
    w, wup, bup = _prep_w_in(w_in, w_up, b_up)
    w_re2 = jnp.transpose(w_re, (1, 0, 2)).reshape(D_MODEL, N_EXPERTS)
    rpad = jnp.zeros((D_MODEL, LANES - N_GROUPS - N_EXPERTS), w_rg.dtype)
    wr = jnp.concatenate([w_rg, w_re2, rpad], axis=1)
    wr_hi = _bf16(wr)
    wr = jnp.stack([wr_hi, _bf16(wr - wr_hi.astype(jnp.float32))])
    br =jnp.concatenate([b_rg, b_re.reshape(N_EXPERTS), jnp.zeros((LANES - N_GROUPS - N_EXPERTS,), b_rg.dtype)])
    row = lambda a: a.reshape(1, -1)
    return dict(w=w, wup=wup, bup=bup, gla_g=row(gla_g), gla_b=row(gla_b), wbg=_bf16(w_bg), wbd=_bf16(w_bd),
                wo=_bf16(w_o), ln1_g=row(ln1_g), ln1_b=row(ln1_b), wr=wr, br=row(br), weg=_bf16(w_eg),
                weu=_bf16(w_eu), wed=_bf16(w_ed), ln2_g=row(ln2_g), ln2_b=row(ln2_b))


MOE_TOKEN_TILE = 1024


def _token_tile(n):
    for tm in (512, 256, 128):
        if n % tm == 0:
            return tm
    raise ValueError(f"token count {n} is not a multiple of 128")


def _trunk_layer(x, s0, past_k, past_v, past_ki, p):
    B, T, _ = x.shape
    past = past_k.shape[1]
    n = B * T
    tm = _token_tile(n)
    assert T % tm == 0 or tm % T == 0
    x2d = x.reshape(n, D_MODEL)
    gq, gk, gv, sgg, lf, dq, dk, dv, iq, sga, sgb, misc = _in_proj(x2d, p["w"], p["wup"], p["bup"], T, past, tm)
    seq = lambda a: a.reshape(B, T, a.shape[-1])

    o_gla, s_fin = _gla(seq(gq), seq(gk), seq(lf), seq(gv), seq(sgg), s0, p["gla_g"], p["gla_b"])

    k_new, v_new, ki_new = seq(dk), seq(dv), seq(misc)[:, :, :IDX_DIM]
    total = past + T
    tk = DSA_KEY_TILE
    lp = -(-total // tk) * tk

    def keys(past_rows, new_rows):
        rows = jnp.concatenate([past_rows.reshape(B, past, new_rows.shape[-1]), new_rows], axis=1)
        return _bf16(jnp.pad(rows, ((0, 0), (0, lp - total), (0, 0))))

    v_all = keys(past_v, v_new).reshape(B, lp // tk, tk, DSA_KV_HEADS, DSA_HEAD_DIM)
    ones = jnp.ones((B, lp // tk, tk, DSA_KV_HEADS, 1), jnp.bfloat16)
    zpad = jnp.zeros((B, lp // tk, tk, DSA_KV_HEADS, _VT_ROWS - DSA_HEAD_DIM - 1), jnp.bfloat16)
    vt = jnp.transpose(jnp.concatenate([v_all, ones, zpad], axis=-1), (0, 3, 1, 4, 2))

    tq = 256 if T % 256 == 0 else LANES
    tp = -(-T // tq) * tq
    qpad = lambda a: jnp.pad(a, ((0, 0), (0, tp - T), (0, 0)))
    wt = jnp.transpose(qpad(seq(misc)[:, :, _MISC_IW:_MISC_IW + IDX_HEADS]), (0, 2, 1))
    o_dsa = _dsa(qpad(seq(dq)), qpad(seq(iq)), wt, keys(past_k, k_new), vt, keys(past_ki, ki_new),
                 past, T, tq, tk)[:, :T]

    h = _merge(o_gla.reshape(n, _GV), o_dsa.reshape(n, _DQ), sga, sgb, x2d, p["wbg"], p["wbd"], p["wo"],
               p["ln1_g"], p["ln1_b"], tm)
    tm_moe = MOE_TOKEN_TILE if n % MOE_TOKEN_TILE == 0 else tm
    y = _moe(h, p["wr"], p["br"], p["weg"], p["weu"], p["wed"], p["ln2_g"], p["ln2_b"], tm_moe)
    return (y.reshape(B, T, D_MODEL), s_fin, k_new.reshape(B, T, DSA_KV_HEADS, DSA_HEAD_DIM),
            v_new.reshape(B, T, DSA_KV_HEADS, DSA_HEAD_DIM), ki_new)


def kernel(x_prompt, x_sample, state_gla, cache_k, cache_v, cache_k_idx, w_in, w_gla_gate_up, b_gla_gate,
           gla_norm_g, gla_norm_b, w_branch_gla, w_branch_dsa, w_out, ln1_g, ln1_b, w_router_group,
           b_router_group, w_router_expert, b_router_expert, w_expert_gate, w_expert_up, w_expert_down,
           ln2_g, ln2_b):
    B = x_prompt.shape[0]
    l = 0
    p = _prep_weights(w_in[l], w_gla_gate_up[l], b_gla_gate[l], gla_norm_g[l], gla_norm_b[l], w_branch_gla[l],
                      w_branch_dsa[l], w_out[l], ln1_g[l], ln1_b[l], w_router_group[l], b_router_group[l],
                      w_router_expert[l], b_router_expert[l], w_expert_gate[l], w_expert_up[l],
                      w_expert_down[l], ln2_g[l], ln2_b[l])
    s0 = jnp.zeros((B, GLA_HEADS, GLA_DK, GLA_DV), jnp.float32)
    ek = jnp.zeros((B, 0, DSA_KV_HEADS, DSA_HEAD_DIM), x_prompt.dtype)
    eki = jnp.zeros((B, 0, IDX_DIM), x_prompt.dtype)
    yp, sp, kp, vp, kip = _trunk_layer(x_prompt, s0, ek, ek, eki, p)
    ys, ss, ksn, vsn, kisn = _trunk_layer(x_sample, state_gla[l], cache_k[l], cache_v[l], cache_k_idx[l], p)
    stack = lambda a: a[None]
    return (yp, ys, stack(sp), stack(kp), stack(vp), stack(kip), stack(ss), stack(ksn), stack(vsn), stack(kisn))
```

```python
import functools
import math

import jax
import jax.numpy as jnp
from jax import lax
from jax.experimental import pallas as pl
from jax.experimental.pallas import tpu as pltpu

D_MODEL = 1024
CHUNK = 64
ROPE_THETA = 10000.0
LN_EPS = 1e-5
GLA_HEADS = 4
GLA_DK = D_MODEL // 2 // GLA_HEADS
GLA_DV = D_MODEL // GLA_HEADS
GLA_GATE_RANK = 16
GLA_TAU = 16.0
DSA_HEADS = 8
DSA_KV_HEADS = 2
DSA_HEAD_DIM = D_MODEL // DSA_HEADS
IDX_HEADS = 8
IDX_DIM = 64
TOPK_MAX = 256
IDX_W_SCALE = (IDX_HEADS ** -0.5) * (IDX_DIM ** -0.5)
N_GROUPS = 4
EXPERTS_PER_GROUP = 4
N_EXPERTS = N_GROUPS * EXPERTS_PER_GROUP
D_EXPERT = 256
TOP_K_IN_GROUP = 2
DEPTH = 1
ALPHA = (2 * DEPTH) ** 0.25

LANES = 128
GLA_SUB = 16
GLA_FACTORED_MAX_DECAY = 60.0
VMEM_LIMIT = 56 * 1024 * 1024
NEG_BIG = -1e30
DSA_Q_DTYPE = jnp.bfloat16
DSA_Q_SCALE = (DSA_HEAD_DIM ** -0.5) * math.log2(math.e)

_GK = GLA_HEADS * GLA_DK
_GV = GLA_HEADS * GLA_DV
_DQ = DSA_HEADS * DSA_HEAD_DIM
_DKV = DSA_KV_HEADS * DSA_HEAD_DIM
_IQ = IDX_HEADS * IDX_DIM
_C_GQ = 0
_C_GK = _C_GQ + _GK
_C_GV = _C_GK + _GK
_C_GG = _C_GV + _GV
_C_DQ = _C_GG + _GV
_C_DK = _C_DQ + _DQ
_C_DV = _C_DK + _DKV
_C_IQ = _C_DV + _DKV
_C_GA = _C_IQ + _IQ
_C_GB = _C_GA + D_MODEL
_C_MISC = _C_GB + D_MODEL
_C_END = _C_MISC + LANES
_MISC_GR = IDX_DIM
_MISC_IW = IDX_DIM + GLA_GATE_RANK


def _bf16(a):
    return a.astype(jnp.bfloat16)


def _dot(a, b):
    return jnp.dot(a, b, preferred_element_type=jnp.float32)


def _dot_nt(a, b):
    return lax.dot_general(a, b, (((1,), (1,)), ((), ())), preferred_element_type=jnp.float32)


def _sigmoid(x):
    return 1.0 / (1.0 + jnp.exp(-x))


def _layer_norm(x, g, b):
    mu = jnp.mean(x, axis=-1, keepdims=True)
    xc = x - mu
    var = jnp.mean(xc * xc, axis=-1, keepdims=True)
    return xc * lax.rsqrt(var + LN_EPS) * g + b


def _rope_full(z, cos, sin):
    return z * cos + pltpu.roll(z, DSA_HEAD_DIM // 2, axis=1) * sin


def _rope_pair(z, cos, sin, first_half):
    q = IDX_DIM // 2
    partner = jnp.where(first_half, pltpu.roll(z, LANES - q, axis=1), pltpu.roll(z, q, axis=1))
    return z * cos + partner * sin


def _in_proj_kernel(x_ref, w_ref, wup_ref, bup_ref, cos_ref, sin_ref, cosi_ref, sini_ref,
                    gq_ref, gk_ref, gv_ref, sgg_ref, lf_ref, dq_ref, dk_ref, dv_ref, iq_ref,
                    sga_ref, sgb_ref, misc_ref):
    xb = _bf16(x_ref[...])

    def proj(c0, width):
        return _dot(xb, w_ref[:, c0:c0 + width])

    gq_ref[...] = proj(_C_GQ, _GK) * (GLA_DK ** -0.5)
    gk_ref[...] = proj(_C_GK, _GK)
    gv_ref[...] = proj(_C_GV, _GV)
    gg = proj(_C_GG, _GV)
    sgg_ref[...] = gg * _sigmoid(gg)
    sga_ref[...] = _sigmoid(proj(_C_GA, D_MODEL))
    sgb_ref[...] = _sigmoid(proj(_C_GB, D_MODEL))
    dv_ref[...] = proj(_C_DV, _DKV)

    cos = cos_ref[...]
    sin = sin_ref[...]
    for h in range(DSA_HEADS):
        z = proj(_C_DQ + h * DSA_HEAD_DIM, DSA_HEAD_DIM)
        dq_ref[:, h * DSA_HEAD_DIM:(h + 1) * DSA_HEAD_DIM] = (_rope_full(z, cos, sin) * DSA_Q_SCALE).astype(dq_ref.dtype)
    for h in range(DSA_KV_HEADS):
        z = proj(_C_DK + h * DSA_HEAD_DIM, DSA_HEAD_DIM)
        dk_ref[:, h * DSA_HEAD_DIM:(h + 1) * DSA_HEAD_DIM] = _rope_full(z, cos, sin)

    cosi = cosi_ref[...]
    sini = sini_ref[...]
    lane = lax.broadcasted_iota(jnp.int32, cosi.shape, 1)
    first_half = (lane & (IDX_DIM - 1)) < (IDX_DIM // 2)
    for p in range(_IQ // LANES):
        z = proj(_C_IQ + p * LANES, LANES)
        iq_ref[:, p * LANES:(p + 1) * LANES] = _rope_pair(z, cosi, sini, first_half).astype(iq_ref.dtype)

    misc = proj(_C_MISC, LANES)
    pre = _dot(_bf16(misc), wup_ref[...]) + bup_ref[...]
    lf_ref[...] = (jnp.minimum(pre, 0.0) - jnp.log(1.0 + jnp.exp(-jnp.abs(pre)))) * (1.0 / GLA_TAU)
    roped = _rope_pair(misc, cosi, sini, first_half)
    misc_ref[...] = jnp.where(lane < IDX_DIM, roped, misc * IDX_W_SCALE)


def _rope_tables(T, past, rows):
    pos = (past + jnp.arange(T, dtype=jnp.int32)).astype(jnp.float32)

    def tables(dim):
        half = dim // 2
        inv = ROPE_THETA ** (-jnp.arange(half, dtype=jnp.float32) / half)
        ang = pos[:, None] * inv[None, :]
        c, s = jnp.cos(ang), jnp.sin(ang)
        reps = LANES // dim
        cos = jnp.tile(jnp.concatenate([c, c], axis=1), (1, reps))
        sin = jnp.tile(jnp.concatenate([-s, s], axis=1), (1, reps))
        if rows > T:
            cos = jnp.tile(cos, (rows // T, 1))
            sin = jnp.tile(sin, (rows // T, 1))
        return cos, sin

    return tables(DSA_HEAD_DIM) + tables(IDX_DIM)


def _prep_w_in(w_in, w_up, b_up):
    sizes = (_GK, _GK, _GV, _GV, GLA_GATE_RANK, _DQ, _DKV, _DKV, _IQ, IDX_DIM, IDX_HEADS, D_MODEL, D_MODEL)
    offs = [0]
    for s in sizes:
        offs.append(offs[-1] + s)
    gq, gk, gv, gg, gr, dq, dk, dv, iq, ik, iw, ga, gb = (w_in[:, offs[i]:offs[i + 1]] for i in range(len(sizes)))
    pad = jnp.zeros((D_MODEL, LANES - IDX_DIM - GLA_GATE_RANK - IDX_HEADS), w_in.dtype)
    w = jnp.concatenate([gq, gk, gv, gg, dq, dk, dv, iq, ga, gb, ik, gr, iw, pad], axis=1)
    wup = jnp.zeros((LANES, _GK), w_up.dtype).at[_MISC_GR:_MISC_GR + GLA_GATE_RANK].set(w_up)
    return _bf16(w), _bf16(wup), b_up.reshape(1, _GK)


def _in_proj(x2d, w, wup, bup, T, past, tm):
    n = x2d.shape[0]
    rows = max(T, tm)
    cos, sin, cosi, sini = _rope_tables(T, past, rows)
    nt = rows // tm
    tok = lambda i: (i, 0)
    tab = lambda i: (i % nt, 0)
    whole = lambda i: (0, 0)
    widths = (_GK, _GK, _GV, _GV, _GK, _DQ, _DKV, _DKV, _IQ, D_MODEL, D_MODEL, LANES)
    f32 = jnp.float32
    dtypes = (f32, f32, f32, f32, f32, DSA_Q_DTYPE, f32, f32, DSA_Q_DTYPE, f32, f32, f32)
    return pl.pallas_call(
        _in_proj_kernel,
        grid=(n // tm,),
        in_specs=[pl.BlockSpec((tm, D_MODEL), tok),
                  pl.BlockSpec((D_MODEL, _C_END), whole),
                  pl.BlockSpec((LANES, _GK), whole),
                  pl.BlockSpec((1, _GK), whole),
                  pl.BlockSpec((tm, LANES), tab), pl.BlockSpec((tm, LANES), tab),
                  pl.BlockSpec((tm, LANES), tab), pl.BlockSpec((tm, LANES), tab)],
        out_specs=[pl.BlockSpec((tm, wd), tok) for wd in widths],
        out_shape=[jax.ShapeDtypeStruct((n, wd), dt) for wd, dt in zip(widths, dtypes)],
        compiler_params=pltpu.CompilerParams(dimension_semantics=("arbitrary",), vmem_limit_bytes=VMEM_LIMIT),
        name="in_proj",
    )(x2d, w, wup, bup, cos, sin, cosi, sini)


def _split3(a):
    hi = _bf16(a)
    r1 = a - hi.astype(jnp.float32)
    mid = _bf16(r1)
    lo = _bf16(r1 - mid.astype(jnp.float32))
    return hi, mid, lo


def _gla_kernel(q_ref, k_ref, lf_ref, v_ref, sgg_ref, s0_ref, g_ref, b_ref, o_ref, s_ref, a_ref, *, c):
    ci = pl.program_id(1)

    @pl.when(ci == 0)
    def _():
        s_ref[...] = s0_ref[...]

    nsub = c // GLA_SUB
    row = lax.broadcasted_iota(jnp.int32, (c, c), 0)
    col = lax.broadcasted_iota(jnp.int32, (c, c), 1)
    causal = row >= col
    tri = _bf16(jnp.where(causal, 1.0, 0.0))
    srow = lax.broadcasted_iota(jnp.int32, (GLA_SUB, GLA_SUB), 0)
    scol = lax.broadcasted_iota(jnp.int32, (GLA_SUB, GLA_SUB), 1)
    gam = g_ref[...]
    bet = b_ref[...]

    lf = lf_ref[...]
    hi, mid, lo = _split3(lf)
    b_all = _dot(tri, hi) + _dot(tri, mid) + _dot(tri, lo)
    chunk_decay = jnp.max(-jnp.sum(lf, axis=0, keepdims=True))

    def intra_factored(q, k, b):
        b0 = b[0:1]
        a = _dot_nt(_bf16(q * jnp.exp(b - b0)), _bf16(k * jnp.exp(b0 - b)))
        return jnp.where(causal, a, 0.0)

    def intra_exact(q, k, b):
        for bi in range(nsub):
            r0 = bi * GLA_SUB
            rs = slice(r0, r0 + GLA_SUB)
            qi = q[rs]
            bq = b[rs]
            ki = k[rs]
            d = jnp.zeros((GLA_SUB, GLA_SUB), jnp.float32)
            for j in range(GLA_SUB):
                f = jnp.exp(jnp.minimum(bq - bq[j:j + 1], 0.0)) * qi * ki[j:j + 1]
                d = jnp.where(scol == j, jnp.sum(f, axis=-1, keepdims=True), d)
            a_ref[rs, r0:r0 + GLA_SUB] = jnp.where(srow >= scol, d, 0.0)
            if bi > 0:
                ref_b = b[r0:r0 + 1]
                qt = qi * jnp.exp(bq - ref_b)
                kt = k[:r0] * jnp.exp(ref_b - b[:r0])
                a_ref[rs, :r0] = _dot_nt(_bf16(qt), _bf16(kt))
            if bi < nsub - 1:
                a_ref[rs, r0 + GLA_SUB:] = jnp.zeros((GLA_SUB, c - r0 - GLA_SUB), jnp.float32)
        return a_ref[...]

    def heads(intra):
        for h in range(GLA_HEADS):
            ks = slice(h * GLA_DK, (h + 1) * GLA_DK)
            vs = slice(h * GLA_DV, (h + 1) * GLA_DV)
            q = q_ref[:, ks]
            k = k_ref[:, ks]
            v = _bf16(v_ref[:, vs])
            b = b_all[:, ks]
            s = s_ref[h]
            o = _dot(_bf16(q * jnp.exp(b)), _bf16(s)) + _dot(_bf16(intra(q, k, b)), v)

            bt = b.T
            bl = bt[:, c - 1:c]
            kdt = k.T * jnp.exp(bl - bt)
            s_ref[h] = jnp.exp(bl) * s + _dot(_bf16(kdt), v)

            o_ref[:, vs] = _layer_norm(o, gam, bet) * sgg_ref[:, vs]

    small_decay = chunk_decay <= GLA_FACTORED_MAX_DECAY

    @pl.when(small_decay)
    def _():
        heads(intra_factored)

    @pl.when(jnp.logical_not(small_decay))
    def _():
        heads(intra_exact)


def _gla(q, k, lf, v, sgg, s0, gam, bet):
    B, T, _ = q.shape
    c = min(CHUNK, T)
    tokk = lambda b, i: (b, i, 0)
    st = lambda b, i: (b, 0, 0, 0)
    whole = lambda b, i: (0, 0)
    return pl.pallas_call(
        functools.partial(_gla_kernel, c=c),
        grid=(B, T // c),
        in_specs=[pl.BlockSpec((None, c, _GK), tokk), pl.BlockSpec((None, c, _GK), tokk),
                  pl.BlockSpec((None, c, _GK), tokk), pl.BlockSpec((None, c, _GV), tokk),
                  pl.BlockSpec((None, c, _GV), tokk),
                  pl.BlockSpec((None, GLA_HEADS, GLA_DK, GLA_DV), st),
                  pl.BlockSpec((1, GLA_DV), whole), pl.BlockSpec((1, GLA_DV), whole)],
        out_specs=[pl.BlockSpec((None, c, _GV), tokk),
                   pl.BlockSpec((None, GLA_HEADS, GLA_DK, GLA_DV), st)],
        out_shape=[jax.ShapeDtypeStruct((B, T, _GV), jnp.float32),
                   jax.ShapeDtypeStruct((B, GLA_HEADS, GLA_DK, GLA_DV), jnp.float32)],
        scratch_shapes=[pltpu.VMEM((c, c), jnp.float32)],
        compiler_params=pltpu.CompilerParams(dimension_semantics=("arbitrary", "arbitrary"),
                                             vmem_limit_bytes=VMEM_LIMIT),
        name="gla",
    )(q, k, lf, v, sgg, s0, gam, bet)


def _sortable_key(s):
    bits = pltpu.bitcast(s, jnp.int32)
    key = bits ^ ((bits >> 31) & 0x7FFFFFFF)
    return jnp.where(key == -1, 0, key)


def _dsa_kernel(qd_ref, qi_ref, wt_ref, k_ref, vt_ref, ki_ref, o_ref,
                key_ref, bias_ref, m_ref, acc_ref, lg_ref, *, tq, tk, past, seq_len, topk):
    t0 = pl.program_id(1) * tq
    total = past + seq_len
    qpos = past + t0 + lax.broadcasted_iota(jnp.int32, (1, tq), 1)
    limit = jnp.minimum(((qpos >> 6) + 1) * CHUNK, total)
    lim_max = jnp.minimum(((past + t0 + tq - 1) // CHUNK + 1) * CHUNK, total)
    nk = (lim_max + tk - 1) // tk
    lp = key_ref.shape[0] * tk
    key_iota = lax.broadcasted_iota(jnp.int32, (tk, tq), 0)

    qis = [_bf16(qi_ref[:, h * IDX_DIM:(h + 1) * IDX_DIM]) for h in range(IDX_HEADS)]
    neg_inf = jnp.float32(-jnp.inf)

    def score_tile(kt, carry):
        kit = ki_ref[pl.ds(pl.multiple_of(kt * tk, tk), tk), :]
        s = None
        for h in range(IDX_HEADS):
            term = wt_ref[h:h + 1, :] * jnp.maximum(_dot_nt(kit, qis[h]), 0.0)
            s = term if s is None else s + term
        s = jnp.where(key_iota < limit - kt * tk, s, neg_inf)
        key_ref[kt] = _sortable_key(s)
        return carry

    lax.fori_loop(0, nk, score_tile, 0)

    def count_ge(ref, cand):
        cb = jnp.broadcast_to(cand, (_CNT_ROWS, tq))

        def body(kt, acc):
            tile = ref[kt]
            for j in range(tk // _CNT_ROWS):
                acc = acc + jnp.where(tile[j * _CNT_ROWS:(j + 1) * _CNT_ROWS] >= cb, 1, 0)
            return acc

        acc = lax.fori_loop(0, nk, body, jnp.zeros((_CNT_ROWS, tq), jnp.int32))
        return jnp.sum(acc, axis=0, keepdims=True)

    n_nonneg = count_ge(key_ref, jnp.zeros((1, tq), jnp.int32))
    n_pos = count_ge(key_ref, jnp.ones((1, tq), jnp.int32))
    zero_tie = (n_nonneg > topk) & (n_pos < topk)
    int_min = jnp.int32(-2 ** 31)
    thr0 = jnp.where(n_pos == topk, 1, jnp.where(n_nonneg >= topk, 0, int_min))
    settled0 = jnp.where((limit <= topk) | (n_nonneg == topk) | (n_pos == topk) | zero_tie, 1, 0)

    def search_cond(state):
        i, _, _, open_lanes = state
        return (i < 31) & (open_lanes > 0)

    def search_bit(state):
        i, thr, settled, _ = state
        for j in range(_BITS_PER_CHECK):
            bit = 30 - (i + j)
            cand = thr ^ jnp.where(bit >= 0, jnp.left_shift(jnp.int32(1), jnp.maximum(bit, 0)), 0)
            cnt = count_ge(key_ref, cand)
            thr = jnp.where((settled > 0) | (cnt < topk), thr, cand)
            settled = jnp.where((cnt == topk) & (bit >= 0), 1, settled)
        return i + _BITS_PER_CHECK, thr, settled, 1 - jnp.min(settled)

    _, thr, settled, open_lanes = lax.while_loop(
        search_cond, search_bit, (jnp.int32(0), thr0, settled0, 1 - jnp.min(settled0)))

    n_above = lax.cond(open_lanes > 0, lambda: count_ge(key_ref, thr + 1), lambda: jnp.zeros((1, tq), jnp.int32))
    tied = zero_tie | (settled == 0)
    need = jnp.where(zero_tie, topk - n_pos, topk - n_above)
    need = jnp.where(tied, need, 2 * lp).astype(jnp.float32)
    any_tied = jnp.max(jnp.where(tied, 1, 0))

    @pl.when(any_tied == 0)
    def _():
        def select_tile(kt, carry):
            sel = (key_ref[kt] >= thr) & (key_iota < limit - kt * tk)
            bias_ref[kt] = jnp.where(sel, 0.0, NEG_BIG)
            return carry

        lax.fori_loop(0, nk, select_tile, 0)

    @pl.when(any_tied > 0)
    def _():
        krow = lax.broadcasted_iota(jnp.int32, (tk, tk), 0)
        kcol = lax.broadcasted_iota(jnp.int32, (tk, tk), 1)
        prefix_ones = _bf16(jnp.where(krow >= kcol, 1.0, 0.0))

        def select_tile(kt, seen):
            key = key_ref[kt]
            eq = key == thr
            rank = _dot(prefix_ones, _bf16(jnp.where(eq, 1.0, 0.0))) + seen
            sel = ((key > thr) | (eq & (rank <= need))) & (key_iota < limit - kt * tk)
            bias_ref[kt] = jnp.where(sel, 0.0, NEG_BIG)
            return rank[tk - 1:tk]

        lax.fori_loop(0, nk, select_tile, jnp.zeros((1, tq), jnp.float32))

    rep = DSA_HEADS // DSA_KV_HEADS
    m_ref[...] = jnp.full(m_ref.shape, NEG_BIG, jnp.float32)
    acc_ref[...] = jnp.zeros(acc_ref.shape, jnp.float32)
    qs = [_bf16(qd_ref[:, h * DSA_HEAD_DIM:(h + 1) * DSA_HEAD_DIM]) for h in range(DSA_HEADS)]

    def attend_tile(kt, carry):
        rows = pl.ds(pl.multiple_of(kt * tk, tk), tk)
        bias = bias_ref[kt]

        def logits(h):
            g = h // rep
            return _dot_nt(k_ref[rows, g * DSA_HEAD_DIM:(g + 1) * DSA_HEAD_DIM], qs[h])

        for h in range(_LG_BUFS - 1):
            lg_ref[h] = logits(h)
        for h in range(DSA_HEADS):
            if h + _LG_BUFS - 1 < DSA_HEADS:
                lg_ref[(h + _LG_BUFS - 1) % _LG_BUFS] = logits(h + _LG_BUFS - 1)
            lb = lg_ref[h % _LG_BUFS] + bias
            m_old = m_ref[h]
            m_new = jnp.maximum(m_old, jnp.max(lb, axis=0, keepdims=True))
            p = jnp.exp2(lb - m_new)
            a = jnp.exp2(m_old - m_new)
            vtg = vt_ref[kt, h // rep]
            acc_ref[h] = a * acc_ref[h] + _dot(vtg, _bf16(p))
            m_ref[h] = m_new
        return carry

    lax.fori_loop(0, nk, attend_tile, 0)
    for h in range(DSA_HEADS):
        acc = acc_ref[h]
        o_t = acc[:DSA_HEAD_DIM] / acc[DSA_HEAD_DIM:DSA_HEAD_DIM + 1]
        o_ref[:, h * DSA_HEAD_DIM:(h + 1) * DSA_HEAD_DIM] = o_t.T


DSA_KEY_TILE = 512
_BITS_PER_CHECK = 4
_LG_BUFS = 4
_CNT_ROWS = 32
_VT_ROWS = DSA_HEAD_DIM + 16


def _dsa(qd, qi, wt, k_all, vt, ki_all, past, seq_len, tq, tk):
    B, tp, _ = qd.shape
    lp = k_all.shape[1]
    nkt = lp // tk
    topk = min(TOPK_MAX, (past + seq_len) // 4)
    tokk = lambda b, i: (b, i, 0)
    kv = lambda b, i: (b, 0, 0)
    return pl.pallas_call(
        functools.partial(_dsa_kernel, tq=tq, tk=tk, past=past, seq_len=seq_len, topk=topk),
        grid=(B, tp // tq),
        in_specs=[pl.BlockSpec((None, tq, _DQ), tokk), pl.BlockSpec((None, tq, _IQ), tokk),
                  pl.BlockSpec((None, IDX_HEADS, tq), lambda b, i: (b, 0, i)),
                  pl.BlockSpec((None, lp, _DKV), kv),
                  pl.BlockSpec((None, nkt, DSA_KV_HEADS, _VT_ROWS, tk), lambda b, i: (b, 0, 0, 0, 0)),
                  pl.BlockSpec((None, lp, IDX_DIM), kv)],
        out_specs=pl.BlockSpec((None, tq, _DQ), tokk),
        out_shape=jax.ShapeDtypeStruct((B, tp, _DQ), jnp.float32),
        scratch_shapes=[pltpu.VMEM((nkt, tk, tq), jnp.int32),
                        pltpu.VMEM((nkt, tk, tq), jnp.float32),
                        pltpu.VMEM((DSA_HEADS, 1, tq), jnp.float32),
                        pltpu.VMEM((DSA_HEADS, _VT_ROWS, tq), jnp.float32),
                        pltpu.VMEM((_LG_BUFS, tk, tq), jnp.float32)],
        compiler_params=pltpu.CompilerParams(dimension_semantics=("arbitrary", "arbitrary"),
                                             vmem_limit_bytes=VMEM_LIMIT),
        name="dsa",
    )(qd, qi, wt, k_all, vt, ki_all)


def _merge_kernel(og_ref, od_ref, sga_ref, sgb_ref, x_ref, wg_ref, wd_ref, wo_ref, g_ref, b_ref, h_ref):
    m = sga_ref[...] * _dot(_bf16(og_ref[...]), wg_ref[...]) + sgb_ref[...] * _dot(_bf16(od_ref[...]), wd_ref[...])
    mix = _dot(_bf16(m), wo_ref[...])
    h_ref[...] = _layer_norm(ALPHA * x_ref[...] + mix, g_ref[...], b_ref[...])


def _merge(og, od, sga, sgb, x2d, wg, wd, wo, g, b, tm):
    n = x2d.shape[0]
    tok = lambda i: (i, 0)
    whole = lambda i: (0, 0)
    act = pl.BlockSpec((tm, D_MODEL), tok)
    wsp = pl.BlockSpec((D_MODEL, D_MODEL), whole)
    vec = pl.BlockSpec((1, D_MODEL), whole)
    return pl.pallas_call(
        _merge_kernel,
        grid=(n // tm,),
        in_specs=[act, act, act, act, act, wsp, wsp, wsp, vec, vec],
        out_specs=act,
        out_shape=jax.ShapeDtypeStruct((n, D_MODEL), jnp.float32),
        compiler_params=pltpu.CompilerParams(dimension_semantics=("arbitrary",), vmem_limit_bytes=VMEM_LIMIT),
        name="merge",
    )(og, od, sga, sgb, x2d, wg, wd, wo, g, b)


def _first_lane_where(mask, lane):
    return jnp.min(jnp.where(mask, lane, LANES), axis=-1, keepdims=True)


def _route(logits):
    lane = lax.broadcasted_iota(jnp.int32, logits.shape, 1)
    is_g = lane < N_GROUPS
    gl = jnp.where(is_g, logits, NEG_BIG)
    gmax = jnp.max(gl, axis=-1, keepdims=True)
    g_sel = _first_lane_where(is_g & (gl == gmax), lane)
    g_prob = 1.0 / jnp.sum(jnp.where(is_g, jnp.exp(gl - gmax), 0.0), axis=-1, keepdims=True)
    e_lane = lane - N_GROUPS
    in_grp = (e_lane >= 0) & (e_lane < N_EXPERTS) & ((e_lane >> 2) == g_sel)
    el = jnp.where(in_grp, logits, NEG_BIG)
    v1 = jnp.max(el, axis=-1, keepdims=True)
    i1 = _first_lane_where(in_grp & (el == v1), lane)
    rest = in_grp & (lane != i1)
    el2 = jnp.where(rest, logits, NEG_BIG)
    v2 = jnp.max(el2, axis=-1, keepdims=True)
    i2 = _first_lane_where(rest & (el2 == v2), lane)
    e2 = jnp.exp(v2 - v1)
    w1 = g_prob / (1.0 + e2)
    w2 = g_prob * e2 / (1.0 + e2)
    return jnp.where(lane == i1, w1, 0.0) + jnp.where(lane == i2, w2, 0.0)


def _moe_kernel(h_ref, wr_ref, br_ref, wg_ref, wu_ref, wd_ref, g_ref, b_ref, y_ref, gate_ref, acc_ref):
    e = pl.program_id(1)
    h = h_ref[...]

    hb = _bf16(h)

    @pl.when(e == 0)
    def _():
        h_lo = _bf16(h - hb.astype(jnp.float32))
        logits = _dot(hb, wr_ref[0]) + (_dot(h_lo, wr_ref[0]) + _dot(hb, wr_ref[1])) + br_ref[...]
        gate_ref[...] = _route(logits)
        acc_ref[...] = jnp.zeros(acc_ref.shape, jnp.float32)

    lane = lax.broadcasted_iota(jnp.int32, gate_ref.shape, 1)
    gcol = jnp.sum(jnp.where(lane == e + N_GROUPS, gate_ref[...], 0.0), axis=-1, keepdims=True)
    a = _dot(hb, wg_ref[...])
    hid = a * _sigmoid(a) * _dot(hb, wu_ref[...]) * gcol
    acc_ref[...] += _dot(_bf16(hid), wd_ref[...])

    @pl.when(e == N_EXPERTS - 1)
    def _():
        y_ref[...] = _layer_norm(ALPHA * h + acc_ref[...], g_ref[...], b_ref[...])


def _moe(h2d, wr, br, wg, wu, wd, g, b, tm):
    n = h2d.shape[0]
    tok = lambda i, e: (i, 0)
    whole = lambda i, e: (0, 0)
    per_e = lambda i, e: (e, 0, 0)
    return pl.pallas_call(
        _moe_kernel,
        grid=(n // tm, N_EXPERTS),
        in_specs=[pl.BlockSpec((tm, D_MODEL), tok),
                  pl.BlockSpec((2, D_MODEL, LANES), lambda i, e: (0, 0, 0)), pl.BlockSpec((1, LANES), whole),
                  pl.BlockSpec((None, D_MODEL, D_EXPERT), per_e),
                  pl.BlockSpec((None, D_MODEL, D_EXPERT), per_e),
                  pl.BlockSpec((None, D_EXPERT, D_MODEL), per_e),
                  pl.BlockSpec((1, D_MODEL), whole), pl.BlockSpec((1, D_MODEL), whole)],
        out_specs=pl.BlockSpec((tm, D_MODEL), tok),
        out_shape=jax.ShapeDtypeStruct((n, D_MODEL), jnp.float32),
        scratch_shapes=[pltpu.VMEM((tm, LANES), jnp.float32), pltpu.VMEM((tm, D_MODEL), jnp.float32)],
        compiler_params=pltpu.CompilerParams(dimension_semantics=("arbitrary", "arbitrary"),
                                             vmem_limit_bytes=VMEM_LIMIT),
        name="moe",
    )(h2d, wr, br, wg, wu, wd, g, b)


def _prep_weights(w_in, w_up, b_up, gla_g, gla_b, w_bg, w_bd, w_o, ln1_g, ln1_b, w_rg, b_rg, w_re, b_re,
                  w_eg, w_eu, w_ed, ln2_g, ln2_b):
    w, wup, bup = _prep_w_in(w_in, w_up, b_up)
    w_re2 = jnp.transpose(w_re, (1, 0, 2)).reshape(D_MODEL, N_EXPERTS)
    rpad = jnp.zeros((D_MODEL, LANES - N_GROUPS - N_EXPERTS), w_rg.dtype)
    wr = jnp.concatenate([w_rg, w_re2, rpad], axis=1)
    wr_hi = _bf16(wr)
    wr = jnp.stack([wr_hi, _bf16(wr - wr_hi.astype(jnp.float32))])
    br =jnp.concatenate([b_rg, b_re.reshape(N_EXPERTS), jnp.zeros((LANES - N_GROUPS - N_EXPERTS,), b_rg.dtype)])
    row = lambda a: a.reshape(1, -1)
    return dict(w=w, wup=wup, bup=bup, gla_g=row(gla_g), gla_b=row(gla_b), wbg=_bf16(w_bg), wbd=_bf16(w_bd),
                wo=_bf16(w_o), ln1_g=row(ln1_g), ln1_b=row(ln1_b), wr=wr, br=row(br), weg=_bf16(w_eg),
                weu=_bf16(w_eu), wed=_bf16(w_ed), ln2_g=row(ln2_g), ln2_b=row(ln2_b))


MOE_TOKEN_TILE = 1024


def _token_tile(n):
    for tm in (512, 256, 128):
        if n % tm == 0:
            return tm
    raise ValueError(f"token count {n} is not a multiple of 128")


def _trunk_layer(x, s0, past_k, past_v, past_ki, p):
    B, T, _ = x.shape
    past = past_k.shape[1]
    n = B * T
    tm = _token_tile(n)
    assert T % tm == 0 or tm % T == 0
    x2d = x.reshape(n, D_MODEL)
    gq, gk, gv, sgg, lf, dq, dk, dv, iq, sga, sgb, misc = _in_proj(x2d, p["w"], p["wup"], p["bup"], T, past, tm)
    seq = lambda a: a.reshape(B, T, a.shape[-1])

    o_gla, s_fin = _gla(seq(gq), seq(gk), seq(lf), seq(gv), seq(sgg), s0, p["gla_g"], p["gla_b"])

    k_new, v_new, ki_new = seq(dk), seq(dv), seq(misc)[:, :, :IDX_DIM]
    total = past + T
    tk = DSA_KEY_TILE
    lp = -(-total // tk) * tk

    def keys(past_rows, new_rows):
        rows = jnp.concatenate([past_rows.reshape(B, past, new_rows.shape[-1]), new_rows], axis=1)
        return _bf16(jnp.pad(rows, ((0, 0), (0, lp - total), (0, 0))))

    nkt = lp // tk
    vt = jnp.transpose(keys(past_v, v_new).reshape(B, nkt, tk, _DKV), (0, 1, 3, 2))
    vt = vt.reshape(B, nkt, DSA_KV_HEADS, DSA_HEAD_DIM, tk)
    ones = jnp.ones((B, nkt, DSA_KV_HEADS, 1, tk), jnp.bfloat16)
    zpad = jnp.zeros((B, nkt, DSA_KV_HEADS, _VT_ROWS - DSA_HEAD_DIM - 1, tk), jnp.bfloat16)
    vt = jnp.concatenate([vt, ones, zpad], axis=3)

    tq = 256 if T % 256 == 0 else LANES
    tp = -(-T // tq) * tq
    qpad = lambda a: jnp.pad(a, ((0, 0), (0, tp - T), (0, 0)))
    wt = jnp.transpose(qpad(seq(misc)[:, :, _MISC_IW:_MISC_IW + IDX_HEADS]), (0, 2, 1))
    o_dsa = _dsa(qpad(seq(dq)), qpad(seq(iq)), wt, keys(past_k, k_new), vt, keys(past_ki, ki_new),
                 past, T, tq, tk)[:, :T]

    h = _merge(o_gla.reshape(n, _GV), o_dsa.reshape(n, _DQ), sga, sgb, x2d, p["wbg"], p["wbd"], p["wo"],
               p["ln1_g"], p["ln1_b"], tm)
    tm_moe = MOE_TOKEN_TILE if n % MOE_TOKEN_TILE == 0 else tm
    y = _moe(h, p["wr"], p["br"], p["weg"], p["weu"], p["wed"], p["ln2_g"], p["ln2_b"], tm_moe)
    return (y.reshape(B, T, D_MODEL), s_fin, k_new.reshape(B, T, DSA_KV_HEADS, DSA_HEAD_DIM),
            v_new.reshape(B, T, DSA_KV_HEADS, DSA_HEAD_DIM), ki_new)


def kernel(x_prompt, x_sample, state_gla, cache_k, cache_v, cache_k_idx, w_in, w_gla_gate_up, b_gla_gate,
           gla_norm_g, gla_norm_b, w_branch_gla, w_branch_dsa, w_out, ln1_g, ln1_b, w_router_group,
           b_router_group, w_router_expert, b_router_expert, w_expert_gate, w_expert_up, w_expert_down,
           ln2_g, ln2_b):
    B = x_prompt.shape[0]
    l = 0
    p = _prep_weights(w_in[l], w_gla_gate_up[l], b_gla_gate[l], gla_norm_g[l], gla_norm_b[l], w_branch_gla[l],
                      w_branch_dsa[l], w_out[l], ln1_g[l], ln1_b[l], w_router_group[l], b_router_group[l],
                      w_router_expert[l], b_router_expert[l], w_expert_gate[l], w_expert_up[l],
                      w_expert_down[l], ln2_g[l], ln2_b[l])
    s0 = jnp.zeros((B, GLA_HEADS, GLA_DK, GLA_DV), jnp.float32)
    ek = jnp.zeros((B, 0, DSA_KV_HEADS, DSA_HEAD_DIM), x_prompt.dtype)
    eki = jnp.zeros((B, 0, IDX_DIM), x_prompt.dtype)
    yp, sp, kp, vp, kip = _trunk_layer(x_prompt, s0, ek, ek, eki, p)
    ys, ss, ksn, vsn, kisn = _trunk_layer(x_sample, state_gla[l], cache_k[l], cache_v[l], cache_k_idx[l], p)
    stack = lambda a: a[None]
    return (yp, ys, stack(sp), stack(kp), stack(vp), stack(kip), stack(ss), stack(ksn), stack(vsn), stack(kisn))
```

```python
import functools
import math

import jax
import jax.numpy as jnp
from jax import lax
from jax.experimental import pallas as pl
from jax.experimental.pallas import tpu as pltpu

D_MODEL = 1024
CHUNK = 64
ROPE_THETA = 10000.0
LN_EPS = 1e-5
GLA_HEADS = 4
GLA_DK = D_MODEL // 2 // GLA_HEADS
GLA_DV = D_MODEL // GLA_HEADS
GLA_GATE_RANK = 16
GLA_TAU = 16.0
DSA_HEADS = 8
DSA_KV_HEADS = 2
DSA_HEAD_DIM = D_MODEL // DSA_HEADS
IDX_HEADS = 8
IDX_DIM = 64
TOPK_MAX = 256
IDX_W_SCALE = (IDX_HEADS ** -0.5) * (IDX_DIM ** -0.5)
N_GROUPS = 4
EXPERTS_PER_GROUP = 4
N_EXPERTS = N_GROUPS * EXPERTS_PER_GROUP
D_EXPERT = 256
TOP_K_IN_GROUP = 2
DEPTH = 1
ALPHA = (2 * DEPTH) ** 0.25

LANES = 128
GLA_SUB = 16
GLA_FACTORED_MAX_DECAY = 60.0
VMEM_LIMIT = 56 * 1024 * 1024
NEG_BIG = -1e30
MXU_DTYPE = jnp.bfloat16
DSA_Q_DTYPE = MXU_DTYPE
DSA_Q_SCALE = (DSA_HEAD_DIM ** -0.5) * math.log2(math.e)

_GK = GLA_HEADS * GLA_DK
_GV = GLA_HEADS * GLA_DV
_DQ = DSA_HEADS * DSA_HEAD_DIM
_DKV = DSA_KV_HEADS * DSA_HEAD_DIM
_IQ = IDX_HEADS * IDX_DIM
_C_GQ = 0
_C_GK = _C_GQ + _GK
_C_GV = _C_GK + _GK
_C_GG = _C_GV + _GV
_C_DQ = _C_GG + _GV
_C_DK = _C_DQ + _DQ
_C_DV = _C_DK + _DKV
_C_IQ = _C_DV + _DKV
_C_MISC = _C_IQ + _IQ
_C_GA = _C_MISC + LANES
_C_GB = _C_GA + D_MODEL
_C_END = _C_GB + D_MODEL
_MISC_GR = IDX_DIM
_MISC_IW = IDX_DIM + GLA_GATE_RANK


def _bf16(a):
    return a.astype(jnp.bfloat16)


def _dot(a, b):
    return jnp.dot(a, b, preferred_element_type=jnp.float32)


def _dot_nt(a, b):
    return lax.dot_general(a, b, (((1,), (1,)), ((), ())), preferred_element_type=jnp.float32)


def _sigmoid(x):
    return 1.0 / (1.0 + jnp.exp(-x))


def _layer_norm(x, g, b):
    mu = jnp.mean(x, axis=-1, keepdims=True)
    xc = x - mu
    var = jnp.mean(xc * xc, axis=-1, keepdims=True)
    return xc * lax.rsqrt(var + LN_EPS) * g + b


def _rope_full(z, cos, sin):
    return z * cos + pltpu.roll(z, DSA_HEAD_DIM // 2, axis=1) * sin


def _rope_pair(z, cos, sin, first_half):
    q = IDX_DIM // 2
    partner = jnp.where(first_half, pltpu.roll(z, LANES - q, axis=1), pltpu.roll(z, q, axis=1))
    return z * cos + partner * sin


def _in_proj_kernel(x_ref, w_ref, wup_ref, bup_ref, cos_ref, sin_ref, cosi_ref, sini_ref,
                    gq_ref, gk_ref, gv_ref, sgg_ref, lf_ref, dq_ref, dk_ref, dv_ref, iq_ref,
                    sga_ref, sgb_ref, misc_ref):
    xb = _bf16(x_ref[...])

    def proj(c0, width):
        return _dot(xb, w_ref[:, c0:c0 + width])

    gq_ref[...] = proj(_C_GQ, _GK) * (GLA_DK ** -0.5)
    gk_ref[...] = proj(_C_GK, _GK)
    gv_ref[...] = proj(_C_GV, _GV).astype(gv_ref.dtype)
    gg = proj(_C_GG, _GV)
    sgg_ref[...] = gg * _sigmoid(gg)
    sga_ref[...] = _sigmoid(proj(_C_GA, D_MODEL))
    sgb_ref[...] = _sigmoid(proj(_C_GB, D_MODEL))

    cos = cos_ref[...]
    sin = sin_ref[...]
    zq = proj(_C_DQ, _DQ)
    for h in range(DSA_HEADS):
        hs = slice(h * DSA_HEAD_DIM, (h + 1) * DSA_HEAD_DIM)
        dq_ref[:, hs] = (_rope_full(zq[:, hs], cos, sin) * DSA_Q_SCALE).astype(dq_ref.dtype)
    zkv = proj(_C_DK, 2 * _DKV)
    for h in range(DSA_KV_HEADS):
        hs = slice(h * DSA_HEAD_DIM, (h + 1) * DSA_HEAD_DIM)
        dk_ref[:, hs] = _rope_full(zkv[:, hs], cos, sin)
    dv_ref[...] = zkv[:, _DKV:]

    cosi = cosi_ref[...]
    sini = sini_ref[...]
    lane = lax.broadcasted_iota(jnp.int32, cosi.shape, 1)
    first_half = (lane & (IDX_DIM - 1)) < (IDX_DIM // 2)
    zi = proj(_C_IQ, _IQ + LANES)
    for p in range(_IQ // LANES):
        ps = slice(p * LANES, (p + 1) * LANES)
        iq_ref[:, ps] = _rope_pair(zi[:, ps], cosi, sini, first_half).astype(iq_ref.dtype)

    misc = zi[:, _IQ:]
    pre = _dot(_bf16(misc), wup_ref[...]) + bup_ref[...]
    lf_ref[...] = (jnp.minimum(pre, 0.0) - jnp.log(1.0 + jnp.exp(-jnp.abs(pre)))) * (1.0 / GLA_TAU)
    roped = _rope_pair(misc, cosi, sini, first_half)
    misc_ref[...] = jnp.where(lane < IDX_DIM, roped, misc * IDX_W_SCALE)


def _rope_tables(T, past, rows):
    pos = (past + jnp.arange(T, dtype=jnp.int32)).astype(jnp.float32)

    def tables(dim):
        half = dim // 2
        inv = ROPE_THETA ** (-jnp.arange(half, dtype=jnp.float32) / half)
        ang = pos[:, None] * inv[None, :]
        c, s = jnp.cos(ang), jnp.sin(ang)
        reps = LANES // dim
        cos = jnp.tile(jnp.concatenate([c, c], axis=1), (1, reps))
        sin = jnp.tile(jnp.concatenate([-s, s], axis=1), (1, reps))
        if rows > T:
            cos = jnp.tile(cos, (rows // T, 1))
            sin = jnp.tile(sin, (rows // T, 1))
        return cos, sin

    return tables(DSA_HEAD_DIM) + tables(IDX_DIM)


def _prep_w_in(w_in, w_up, b_up):
    sizes = (_GK, _GK, _GV, _GV, GLA_GATE_RANK, _DQ, _DKV, _DKV, _IQ, IDX_DIM, IDX_HEADS, D_MODEL, D_MODEL)
    offs = [0]
    for s in sizes:
        offs.append(offs[-1] + s)
    gq, gk, gv, gg, gr, dq, dk, dv, iq, ik, iw, ga, gb = (w_in[:, offs[i]:offs[i + 1]] for i in range(len(sizes)))
    pad = jnp.zeros((D_MODEL, LANES - IDX_DIM - GLA_GATE_RANK - IDX_HEADS), w_in.dtype)
    w = jnp.concatenate([gq, gk, gv, gg, dq, dk, dv, iq, ik, gr, iw, pad, ga, gb], axis=1)
    wup = jnp.zeros((LANES, _GK), w_up.dtype).at[_MISC_GR:_MISC_GR + GLA_GATE_RANK].set(w_up)
    return _bf16(w), _bf16(wup), b_up.reshape(1, _GK)


def _in_proj(x2d, w, wup, bup, T, past, tm):
    n = x2d.shape[0]
    rows = max(T, tm)
    cos, sin, cosi, sini = _rope_tables(T, past, rows)
    nt = rows // tm
    tok = lambda i: (i, 0)
    tab = lambda i: (i % nt, 0)
    whole = lambda i: (0, 0)
    widths = (_GK, _GK, _GV, _GV, _GK, _DQ, _DKV, _DKV, _IQ, D_MODEL, D_MODEL, LANES)
    f32 = jnp.float32
    dtypes = (f32, f32, MXU_DTYPE, f32, f32, DSA_Q_DTYPE, f32, f32, DSA_Q_DTYPE, f32, f32, f32)
    return pl.pallas_call(
        _in_proj_kernel,
        grid=(n // tm,),
        in_specs=[pl.BlockSpec((tm, D_MODEL), tok),
                  pl.BlockSpec((D_MODEL, _C_END), whole),
                  pl.BlockSpec((LANES, _GK), whole),
                  pl.BlockSpec((1, _GK), whole),
                  pl.BlockSpec((tm, LANES), tab), pl.BlockSpec((tm, LANES), tab),
                  pl.BlockSpec((tm, LANES), tab), pl.BlockSpec((tm, LANES), tab)],
        out_specs=[pl.BlockSpec((tm, wd), tok) for wd in widths],
        out_shape=[jax.ShapeDtypeStruct((n, wd), dt) for wd, dt in zip(widths, dtypes)],
        compiler_params=pltpu.CompilerParams(dimension_semantics=("arbitrary",), vmem_limit_bytes=VMEM_LIMIT),
        name="in_proj",
    )(x2d, w, wup, bup, cos, sin, cosi, sini)


def _split3(a):
    hi = _bf16(a)
    r1 = a - hi.astype(jnp.float32)
    mid = _bf16(r1)
    lo = _bf16(r1 - mid.astype(jnp.float32))
    return hi, mid, lo


def _gla_kernel(q_ref, k_ref, lf_ref, v_ref, sgg_ref, s0_ref, g_ref, b_ref, o_ref, s_ref, a_ref, *, c):
    ci = pl.program_id(1)

    @pl.when(ci == 0)
    def _():
        s_ref[...] = s0_ref[...]

    nsub = c // GLA_SUB
    row = lax.broadcasted_iota(jnp.int32, (c, c), 0)
    col = lax.broadcasted_iota(jnp.int32, (c, c), 1)
    causal = row >= col
    tri = _bf16(jnp.where(causal, 1.0, 0.0))
    srow = lax.broadcasted_iota(jnp.int32, (GLA_SUB, GLA_SUB), 0)
    scol = lax.broadcasted_iota(jnp.int32, (GLA_SUB, GLA_SUB), 1)
    gam = g_ref[...]
    bet = b_ref[...]

    lf = lf_ref[...]
    hi, mid, lo = _split3(lf)
    b_all = _dot(tri, hi) + _dot(tri, mid) + _dot(tri, lo)
    chunk_decay = jnp.max(-jnp.sum(lf, axis=0, keepdims=True))

    def intra_factored(q, k, b):
        b0 = b[0:1]
        a = _dot_nt(_bf16(q * jnp.exp(b - b0)), _bf16(k * jnp.exp(b0 - b)))
        return jnp.where(causal, a, 0.0)

    def intra_exact(q, k, b):
        for bi in range(nsub):
            r0 = bi * GLA_SUB
            rs = slice(r0, r0 + GLA_SUB)
            qi = q[rs]
            bq = b[rs]
            ki = k[rs]
            d = jnp.zeros((GLA_SUB, GLA_SUB), jnp.float32)
            for j in range(GLA_SUB):
                f = jnp.exp(jnp.minimum(bq - bq[j:j + 1], 0.0)) * qi * ki[j:j + 1]
                d = jnp.where(scol == j, jnp.sum(f, axis=-1, keepdims=True), d)
            a_ref[rs, r0:r0 + GLA_SUB] = jnp.where(srow >= scol, d, 0.0)
            if bi > 0:
                ref_b = b[r0:r0 + 1]
                qt = qi * jnp.exp(bq - ref_b)
                kt = k[:r0] * jnp.exp(ref_b - b[:r0])
                a_ref[rs, :r0] = _dot_nt(_bf16(qt), _bf16(kt))
            if bi < nsub - 1:
                a_ref[rs, r0 + GLA_SUB:] = jnp.zeros((GLA_SUB, c - r0 - GLA_SUB), jnp.float32)
        return a_ref[...]

    def heads(intra):
        for h in range(GLA_HEADS):
            ks = slice(h * GLA_DK, (h + 1) * GLA_DK)
            vs = slice(h * GLA_DV, (h + 1) * GLA_DV)
            q = q_ref[:, ks]
            k = k_ref[:, ks]
            v = _bf16(v_ref[:, vs])
            b = b_all[:, ks]
            s = s_ref[h]
            o = _dot(_bf16(q * jnp.exp(b)), _bf16(s)) + _dot(_bf16(intra(q, k, b)), v)

            bt = b.T
            bl = bt[:, c - 1:c]
            kdt = k.T * jnp.exp(bl - bt)
            s_ref[h] = jnp.exp(bl) * s + _dot(_bf16(kdt), v)

            o_ref[:, vs] = (_layer_norm(o, gam, bet) * sgg_ref[:, vs]).astype(o_ref.dtype)

    small_decay = chunk_decay <= GLA_FACTORED_MAX_DECAY

    @pl.when(small_decay)
    def _():
        heads(intra_factored)

    @pl.when(jnp.logical_not(small_decay))
    def _():
        heads(intra_exact)


def _gla(q, k, lf, v, sgg, s0, gam, bet):
    B, T, _ = q.shape
    c = min(CHUNK, T)
    tokk = lambda b, i: (b, i, 0)
    st = lambda b, i: (b, 0, 0, 0)
    whole = lambda b, i: (0, 0)
    return pl.pallas_call(
        functools.partial(_gla_kernel, c=c),
        grid=(B, T // c),
        in_specs=[pl.BlockSpec((None, c, _GK), tokk), pl.BlockSpec((None, c, _GK), tokk),
                  pl.BlockSpec((None, c, _GK), tokk), pl.BlockSpec((None, c, _GV), tokk),
                  pl.BlockSpec((None, c, _GV), tokk),
                  pl.BlockSpec((None, GLA_HEADS, GLA_DK, GLA_DV), st),
                  pl.BlockSpec((1, GLA_DV), whole), pl.BlockSpec((1, GLA_DV), whole)],
        out_specs=[pl.BlockSpec((None, c, _GV), tokk),
                   pl.BlockSpec((None, GLA_HEADS, GLA_DK, GLA_DV), st)],
        out_shape=[jax.ShapeDtypeStruct((B, T, _GV), MXU_DTYPE),
                   jax.ShapeDtypeStruct((B, GLA_HEADS, GLA_DK, GLA_DV), jnp.float32)],
        scratch_shapes=[pltpu.VMEM((c, c), jnp.float32)],
        compiler_params=pltpu.CompilerParams(dimension_semantics=("arbitrary", "arbitrary"),
                                             vmem_limit_bytes=VMEM_LIMIT),
        name="gla",
    )(q, k, lf, v, sgg, s0, gam, bet)


def _sortable_key(s):
    bits = pltpu.bitcast(s, jnp.int32)
    key = bits ^ ((bits >> 31) & 0x7FFFFFFF)
    return jnp.where(key == -1, 0, key)


def _dsa_kernel(qd_ref, qi_ref, wt_ref, k_ref, vt_ref, ki_ref, o_ref,
                key_ref, hi_ref, lo_ref, bias_ref, m_ref, acc_ref, lg_ref, *, tq, tk, past, seq_len, topk):
    t0 = pl.program_id(1) * tq
    total = past + seq_len
    qpos = past + t0 + lax.broadcasted_iota(jnp.int32, (1, tq), 1)
    limit = jnp.minimum(((qpos >> 6) + 1) * CHUNK, total)
    lim_max = jnp.minimum(((past + t0 + tq - 1) // CHUNK + 1) * CHUNK, total)
    nk = (lim_max + tk - 1) // tk
    lp = key_ref.shape[0] * tk
    key_iota = lax.broadcasted_iota(jnp.int32, (tk, tq), 0)

    qis = [_bf16(qi_ref[:, h * IDX_DIM:(h + 1) * IDX_DIM]) for h in range(IDX_HEADS)]
    neg_inf = jnp.float32(-jnp.inf)

    def score_tile(kt, carry):
        kit = ki_ref[pl.ds(pl.multiple_of(kt * tk, tk), tk), :]
        s = None
        for h in range(IDX_HEADS):
            term = wt_ref[h:h + 1, :] * jnp.maximum(_dot_nt(kit, qis[h]), 0.0)
            s = term if s is None else s + term
        s = jnp.where(key_iota < limit - kt * tk, s, neg_inf)
        key = _sortable_key(s)
        key_ref[kt] = key
        hi_ref[kt] = (key >> 16).astype(jnp.int16)
        lo_ref[kt] = ((key & 0xFFFF) - _HALF).astype(jnp.int16)
        return carry

    lax.fori_loop(0, nk, score_tile, 0)

    def count_ge(ref, cand_off):
        cb = jnp.broadcast_to((cand_off - _HALF).astype(jnp.int16), (_CNT_ROWS, tq))

        def body(kt, acc):
            tile = ref[kt]
            for j in range(tk // _CNT_ROWS):
                acc = acc + jnp.where(tile[j * _CNT_ROWS:(j + 1) * _CNT_ROWS] >= cb, jnp.int16(1), jnp.int16(0))
            return acc

        acc = lax.fori_loop(0, nk, body, jnp.zeros((_CNT_ROWS, tq), jnp.int16))
        return jnp.sum(acc.astype(jnp.int32), axis=0, keepdims=True)

    def count_above(ref, off):
        return jnp.where(off >= 2 * _HALF - 1, 0, count_ge(ref, jnp.minimum(off + 1, 2 * _HALF - 1)))

    def radix_search(ref, target, settled):
        def cond(state):
            i, _, _, open_lanes = state
            return (i < 16) & (open_lanes > 0)

        def bits(state):
            i, off, done, _ = state
            for j in range(_BITS_PER_CHECK):
                cand = off | jnp.left_shift(jnp.int32(1), 15 - (i + j))
                cnt = count_ge(ref, cand)
                off = jnp.where((done > 0) | (cnt < target), off, cand)
                done = jnp.where(cnt == target, 1, done)
            return i + _BITS_PER_CHECK, off, done, 1 - jnp.min(done)

        _, off, done, open_lanes = lax.while_loop(
            cond, bits, (jnp.int32(0), jnp.zeros((1, tq), jnp.int32), settled, 1 - jnp.min(settled)))
        return off, done, open_lanes

    settled0 = jnp.where(limit <= topk, 1, 0)
    hi_off, settled1, open1 = radix_search(hi_ref, topk, settled0)
    hi16 = (hi_off - _HALF).astype(jnp.int16)

    n_above_hi = lax.cond(open1 > 0, lambda: count_above(hi_ref, hi_off), lambda: jnp.zeros((1, tq), jnp.int32))

    @pl.when(open1 > 0)
    def _():
        def band_tile(kt, carry):
            lo_ref[kt] = jnp.where(hi_ref[kt] == hi16, lo_ref[kt], jnp.int16(-_HALF))
            return carry

        lax.fori_loop(0, nk, band_tile, 0)

    need_lo = topk - n_above_hi
    lo_off, settled2, open2 = lax.cond(
        open1 > 0, lambda: radix_search(lo_ref, need_lo, settled1),
        lambda: (jnp.zeros((1, tq), jnp.int32), settled1, jnp.int32(0)))

    n_above_lo = lax.cond(open2 > 0, lambda: count_above(lo_ref, lo_off), lambda: jnp.zeros((1, tq), jnp.int32))
    int_min = jnp.int32(-2 ** 31)
    thr = jnp.where(settled0 > 0, int_min,
                    (hi_off - _HALF) * (2 * _HALF) + jnp.where(settled1 > 0, 0, lo_off))
    tied = settled2 == 0
    need = jnp.where(tied, need_lo - n_above_lo, 2 * lp).astype(jnp.float32)
    any_tied = open2

    @pl.when(any_tied == 0)
    def _():
        def select_tile(kt, carry):
            sel = (key_ref[kt] >= thr) & (key_iota < limit - kt * tk)
            bias_ref[kt] = jnp.where(sel, 0.0, NEG_BIG)
            return carry

        lax.fori_loop(0, nk, select_tile, 0)

    @pl.when(any_tied > 0)
    def _():
        krow = lax.broadcasted_iota(jnp.int32, (tk, tk), 0)
        kcol = lax.broadcasted_iota(jnp.int32, (tk, tk), 1)
        prefix_ones = _bf16(jnp.where(krow >= kcol, 1.0, 0.0))

        def select_tile(kt, seen):
            key = key_ref[kt]
            eq = key == thr
            rank = _dot(prefix_ones, _bf16(jnp.where(eq, 1.0, 0.0))) + seen
            sel = ((key > thr) | (eq & (rank <= need))) & (key_iota < limit - kt * tk)
            bias_ref[kt] = jnp.where(sel, 0.0, NEG_BIG)
            return rank[tk - 1:tk]

        lax.fori_loop(0, nk, select_tile, jnp.zeros((1, tq), jnp.float32))

    rep = DSA_HEADS // DSA_KV_HEADS
    m_ref[...] = jnp.full(m_ref.shape, NEG_BIG, jnp.float32)
    acc_ref[...] = jnp.zeros(acc_ref.shape, jnp.float32)
    qs = [_bf16(qd_ref[:, h * DSA_HEAD_DIM:(h + 1) * DSA_HEAD_DIM]) for h in range(DSA_HEADS)]

    def attend_tile(kt, carry):
        rows = pl.ds(pl.multiple_of(kt * tk, tk), tk)
        bias = bias_ref[kt]

        def logits(h):
            g = h // rep
            return _dot_nt(k_ref[rows, g * DSA_HEAD_DIM:(g + 1) * DSA_HEAD_DIM], qs[h])

        for h in range(_LG_BUFS - 1):
            lg_ref[h] = logits(h)
        for h in range(DSA_HEADS):
            if h + _LG_BUFS - 1 < DSA_HEADS:
                lg_ref[(h + _LG_BUFS - 1) % _LG_BUFS] = logits(h + _LG_BUFS - 1)
            lb = lg_ref[h % _LG_BUFS] + bias
            m_old = m_ref[h]
            m_new = jnp.maximum(m_old, jnp.max(lb, axis=0, keepdims=True))
            p = jnp.exp2(lb - m_new)
            a = jnp.exp2(m_old - m_new)
            vtg = vt_ref[kt, h // rep]
            acc_ref[h] = a * acc_ref[h] + _dot(vtg, _bf16(p))
            m_ref[h] = m_new
        return carry

    lax.fori_loop(0, nk, attend_tile, 0)
    for h in range(DSA_HEADS):
        acc = acc_ref[h]
        o_t = acc[:DSA_HEAD_DIM] / acc[DSA_HEAD_DIM:DSA_HEAD_DIM + 1]
        o_ref[:, h * DSA_HEAD_DIM:(h + 1) * DSA_HEAD_DIM] = o_t.T.astype(o_ref.dtype)


DSA_KEY_TILE = 512
_HALF = 1 << 15
_BITS_PER_CHECK = 4
_LG_BUFS = 4
_CNT_ROWS = 64
_VT_ROWS = DSA_HEAD_DIM + 16


def _dsa(qd, qi, wt, k_all, vt, ki_all, past, seq_len, tq, tk):
    B, tp, _ = qd.shape
    lp = k_all.shape[1]
    nkt = lp // tk
    topk = min(TOPK_MAX, (past + seq_len) // 4)
    tokk = lambda b, i: (b, i, 0)
    kv = lambda b, i: (b, 0, 0)
    return pl.pallas_call(
        functools.partial(_dsa_kernel, tq=tq, tk=tk, past=past, seq_len=seq_len, topk=topk),
        grid=(B, tp // tq),
        in_specs=[pl.BlockSpec((None, tq, _DQ), tokk), pl.BlockSpec((None, tq, _IQ), tokk),
                  pl.BlockSpec((None, IDX_HEADS, tq), lambda b, i: (b, 0, i)),
                  pl.BlockSpec((None, lp, _DKV), kv),
                  pl.BlockSpec((None, nkt, DSA_KV_HEADS, _VT_ROWS, tk), lambda b, i: (b, 0, 0, 0, 0)),
                  pl.BlockSpec((None, lp, IDX_DIM), kv)],
        out_specs=pl.BlockSpec((None, tq, _DQ), tokk),
        out_shape=jax.ShapeDtypeStruct((B, tp, _DQ), MXU_DTYPE),
        scratch_shapes=[pltpu.VMEM((nkt, tk, tq), jnp.int32),
                        pltpu.VMEM((nkt, tk, tq), jnp.int16), pltpu.VMEM((nkt, tk, tq), jnp.int16),
                        pltpu.VMEM((nkt, tk, tq), jnp.float32),
                        pltpu.VMEM((DSA_HEADS, 1, tq), jnp.float32),
                        pltpu.VMEM((DSA_HEADS, _VT_ROWS, tq), jnp.float32),
                        pltpu.VMEM((_LG_BUFS, tk, tq), jnp.float32)],
        compiler_params=pltpu.CompilerParams(dimension_semantics=("arbitrary", "arbitrary"),
                                             vmem_limit_bytes=VMEM_LIMIT),
        name="dsa",
    )(qd, qi, wt, k_all, vt, ki_all)


def _merge_kernel(og_ref, od_ref, sga_ref, sgb_ref, x_ref, wg_ref, wd_ref, wo_ref, g_ref, b_ref, h_ref):
    m = sga_ref[...] * _dot(_bf16(og_ref[...]), wg_ref[...]) + sgb_ref[...] * _dot(_bf16(od_ref[...]), wd_ref[...])
    mix = _dot(_bf16(m), wo_ref[...])
    h_ref[...] = _layer_norm(ALPHA * x_ref[...] + mix, g_ref[...], b_ref[...])


def _merge(og, od, sga, sgb, x2d, wg, wd, wo, g, b, tm):
    n = x2d.shape[0]
    tok = lambda i: (i, 0)
    whole = lambda i: (0, 0)
    act = pl.BlockSpec((tm, D_MODEL), tok)
    wsp = pl.BlockSpec((D_MODEL, D_MODEL), whole)
    vec = pl.BlockSpec((1, D_MODEL), whole)
    return pl.pallas_call(
        _merge_kernel,
        grid=(n // tm,),
        in_specs=[act, act, act, act, act, wsp, wsp, wsp, vec, vec],
        out_specs=act,
        out_shape=jax.ShapeDtypeStruct((n, D_MODEL), jnp.float32),
        compiler_params=pltpu.CompilerParams(dimension_semantics=("arbitrary",), vmem_limit_bytes=VMEM_LIMIT),
        name="merge",
    )(og, od, sga, sgb, x2d, wg, wd, wo, g, b)


def _first_lane_where(mask, lane):
    return jnp.min(jnp.where(mask, lane, LANES), axis=-1, keepdims=True)


def _route(logits):
    lane = lax.broadcasted_iota(jnp.int32, logits.shape, 1)
    is_g = lane < N_GROUPS
    gl = jnp.where(is_g, logits, NEG_BIG)
    gmax = jnp.max(gl, axis=-1, keepdims=True)
    g_sel = _first_lane_where(is_g & (gl == gmax), lane)
    g_prob = 1.0 / jnp.sum(jnp.where(is_g, jnp.exp(gl - gmax), 0.0), axis=-1, keepdims=True)
    e_lane = lane - N_GROUPS
    in_grp = (e_lane >= 0) & (e_lane < N_EXPERTS) & ((e_lane >> 2) == g_sel)
    el = jnp.where(in_grp, logits, NEG_BIG)
    v1 = jnp.max(el, axis=-1, keepdims=True)
    i1 = _first_lane_where(in_grp & (el == v1), lane)
    rest = in_grp & (lane != i1)
    el2 = jnp.where(rest, logits, NEG_BIG)
    v2 = jnp.max(el2, axis=-1, keepdims=True)
    i2 = _first_lane_where(rest & (el2 == v2), lane)
    e2 = jnp.exp(v2 - v1)
    w1 = g_prob / (1.0 + e2)
    w2 = g_prob * e2 / (1.0 + e2)
    return jnp.where(lane == i1, w1, 0.0) + jnp.where(lane == i2, w2, 0.0)


def _moe_kernel(h_ref, wr_ref, br_ref, wg_ref, wu_ref, wd_ref, g_ref, b_ref, y_ref, gate_ref, acc_ref):
    e = pl.program_id(1)
    h = h_ref[...]

    hb = _bf16(h)

    @pl.when(e == 0)
    def _():
        h_lo = _bf16(h - hb.astype(jnp.float32))
        logits = _dot(hb, wr_ref[0]) + (_dot(h_lo, wr_ref[0]) + _dot(hb, wr_ref[1])) + br_ref[...]
        gate_ref[...] = _route(logits)
        acc_ref[...] = jnp.zeros(acc_ref.shape, jnp.float32)

    lane = lax.broadcasted_iota(jnp.int32, gate_ref.shape, 1)
    gcol = jnp.sum(jnp.where(lane == e + N_GROUPS, gate_ref[...], 0.0), axis=-1, keepdims=True)
    a = _dot(hb, wg_ref[...])
    hid = a * _sigmoid(a) * _dot(hb, wu_ref[...]) * gcol
    acc_ref[...] += _dot(_bf16(hid), wd_ref[...])

    @pl.when(e == N_EXPERTS - 1)
    def _():
        y_ref[...] = _layer_norm(ALPHA * h + acc_ref[...], g_ref[...], b_ref[...])


def _moe(h2d, wr, br, wg, wu, wd, g, b, tm):
    n = h2d.shape[0]
    tok = lambda i, e: (i, 0)
    whole = lambda i, e: (0, 0)
    per_e = lambda i, e: (e, 0, 0)
    return pl.pallas_call(
        _moe_kernel,
        grid=(n // tm, N_EXPERTS),
        in_specs=[pl.BlockSpec((tm, D_MODEL), tok),
                  pl.BlockSpec((2, D_MODEL, LANES), lambda i, e: (0, 0, 0)), pl.BlockSpec((1, LANES), whole),
                  pl.BlockSpec((None, D_MODEL, D_EXPERT), per_e),
                  pl.BlockSpec((None, D_MODEL, D_EXPERT), per_e),
                  pl.BlockSpec((None, D_EXPERT, D_MODEL), per_e),
                  pl.BlockSpec((1, D_MODEL), whole), pl.BlockSpec((1, D_MODEL), whole)],
        out_specs=pl.BlockSpec((tm, D_MODEL), tok),
        out_shape=jax.ShapeDtypeStruct((n, D_MODEL), jnp.float32),
        scratch_shapes=[pltpu.VMEM((tm, LANES), jnp.float32), pltpu.VMEM((tm, D_MODEL), jnp.float32)],
        compiler_params=pltpu.CompilerParams(dimension_semantics=("arbitrary", "arbitrary"),
                                             vmem_limit_bytes=VMEM_LIMIT),
        name="moe",
    )(h2d, wr, br, wg, wu, wd, g, b)


def _prep_weights(w_in, w_up, b_up, gla_g, gla_b, w_bg, w_bd, w_o, ln1_g, ln1_b, w_rg, b_rg, w_re, b_re,
                  w_eg, w_eu, w_ed, ln2_g, ln2_b):
    w, wup, bup = _prep_w_in(w_in, w_up, b_up)
    w_re2 = jnp.transpose(w_re, (1, 0, 2)).reshape(D_MODEL, N_EXPERTS)
    rpad = jnp.zeros((D_MODEL, LANES - N_GROUPS - N_EXPERTS), w_rg.dtype)
    wr = jnp.concatenate([w_rg, w_re2, rpad], axis=1)
    wr_hi = _bf16(wr)
    wr = jnp.stack([wr_hi, _bf16(wr - wr_hi.astype(jnp.float32))])
    br =jnp.concatenate([b_rg, b_re.reshape(N_EXPERTS), jnp.zeros((LANES - N_GROUPS - N_EXPERTS,), b_rg.dtype)])
    row = lambda a: a.reshape(1, -1)
    return dict(w=w, wup=wup, bup=bup, gla_g=row(gla_g), gla_b=row(gla_b), wbg=_bf16(w_bg), wbd=_bf16(w_bd),
                wo=_bf16(w_o), ln1_g=row(ln1_g), ln1_b=row(ln1_b), wr=wr, br=row(br), weg=_bf16(w_eg),
                weu=_bf16(w_eu), wed=_bf16(w_ed), ln2_g=row(ln2_g), ln2_b=row(ln2_b))


MOE_TOKEN_TILE = 1024


def _token_tile(n):
    for tm in (512, 256, 128):
        if n % tm == 0:
            return tm
    raise ValueError(f"token count {n} is not a multiple of 128")


def _trunk_layer(x, s0, past_k, past_v, past_ki, p):
    B, T, _ = x.shape
    past = past_k.shape[1]
    n = B * T
    tm = _token_tile(n)
    assert T % tm == 0 or tm % T == 0
    x2d = x.reshape(n, D_MODEL)
    gq, gk, gv, sgg, lf, dq, dk, dv, iq, sga, sgb, misc = _in_proj(x2d, p["w"], p["wup"], p["bup"], T, past, tm)
    seq = lambda a: a.reshape(B, T, a.shape[-1])

    o_gla, s_fin = _gla(seq(gq), seq(gk), seq(lf), seq(gv), seq(sgg), s0, p["gla_g"], p["gla_b"])

    k_new, v_new, ki_new = seq(dk), seq(dv), seq(misc)[:, :, :IDX_DIM]
    total = past + T
    tk = DSA_KEY_TILE
    lp = -(-total // tk) * tk

    def keys(past_rows, new_rows):
        rows = jnp.concatenate([past_rows.reshape(B, past, new_rows.shape[-1]), new_rows], axis=1)
        return _bf16(jnp.pad(rows, ((0, 0), (0, lp - total), (0, 0))))

    nkt = lp // tk
    vt = jnp.transpose(keys(past_v, v_new).reshape(B, nkt, tk, _DKV), (0, 1, 3, 2))
    vt = vt.reshape(B, nkt, DSA_KV_HEADS, DSA_HEAD_DIM, tk)
    ones = jnp.ones((B, nkt, DSA_KV_HEADS, 1, tk), jnp.bfloat16)
    zpad = jnp.zeros((B, nkt, DSA_KV_HEADS, _VT_ROWS - DSA_HEAD_DIM - 1, tk), jnp.bfloat16)
    vt = jnp.concatenate([vt, ones, zpad], axis=3)

    tq = 256 if T % 256 == 0 else LANES
    tp = -(-T // tq) * tq
    qpad = lambda a: jnp.pad(a, ((0, 0), (0, tp - T), (0, 0)))
    wt = jnp.transpose(qpad(seq(misc)[:, :, _MISC_IW:_MISC_IW + IDX_HEADS]), (0, 2, 1))
    o_dsa = _dsa(qpad(seq(dq)), qpad(seq(iq)), wt, keys(past_k, k_new), vt, keys(past_ki, ki_new),
                 past, T, tq, tk)[:, :T]

    h = _merge(o_gla.reshape(n, _GV), o_dsa.reshape(n, _DQ), sga, sgb, x2d, p["wbg"], p["wbd"], p["wo"],
               p["ln1_g"], p["ln1_b"], tm)
    tm_moe = MOE_TOKEN_TILE if n % MOE_TOKEN_TILE == 0 else tm
    y = _moe(h, p["wr"], p["br"], p["weg"], p["weu"], p["wed"], p["ln2_g"], p["ln2_b"], tm_moe)
    return (y.reshape(B, T, D_MODEL), s_fin, k_new.reshape(B, T, DSA_KV_HEADS, DSA_HEAD_DIM),
            v_new.reshape(B, T, DSA_KV_HEADS, DSA_HEAD_DIM), ki_new)


def kernel(x_prompt, x_sample, state_gla, cache_k, cache_v, cache_k_idx, w_in, w_gla_gate_up, b_gla_gate,
           gla_norm_g, gla_norm_b, w_branch_gla, w_branch_dsa, w_out, ln1_g, ln1_b, w_router_group,
           b_router_group, w_router_expert, b_router_expert, w_expert_gate, w_expert_up, w_expert_down,
           ln2_g, ln2_b):
    B = x_prompt.shape[0]
    l = 0
    p = _prep_weights(w_in[l], w_gla_gate_up[l], b_gla_gate[l], gla_norm_g[l], gla_norm_b[l], w_branch_gla[l],
                      w_branch_dsa[l], w_out[l], ln1_g[l], ln1_b[l], w_router_group[l], b_router_group[l],
                      w_router_expert[l], b_router_expert[l], w_expert_gate[l], w_expert_up[l],
                      w_expert_down[l], ln2_g[l], ln2_b[l])
    s0 = jnp.zeros((B, GLA_HEADS, GLA_DK, GLA_DV), jnp.float32)
    ek = jnp.zeros((B, 0, DSA_KV_HEADS, DSA_HEAD_DIM), x_prompt.dtype)
    eki = jnp.zeros((B, 0, IDX_DIM), x_prompt.dtype)
    yp, sp, kp, vp, kip = _trunk_layer(x_prompt, s0, ek, ek, eki, p)
    ys, ss, ksn, vsn, kisn = _trunk_layer(x_sample, state_gla[l], cache_k[l], cache_v[l], cache_k_idx[l], p)
    stack = lambda a: a[None]
    return (yp, ys, stack(sp), stack(kp), stack(vp), stack(kip), stack(ss), stack(ksn), stack(vsn), stack(kisn))
```

```python
import functools
import math

import jax
import jax.numpy as jnp
from jax import lax
from jax.experimental import pallas as pl
from jax.experimental.pallas import tpu as pltpu

D_MODEL = 1024
CHUNK = 64
ROPE_THETA = 10000.0
LN_EPS = 1e-5
GLA_HEADS = 4
GLA_DK = D_MODEL // 2 // GLA_HEADS
GLA_DV = D_MODEL // GLA_HEADS
GLA_GATE_RANK = 16
GLA_TAU = 16.0
DSA_HEADS = 8
DSA_KV_HEADS = 2
DSA_HEAD_DIM = D_MODEL // DSA_HEADS
IDX_HEADS = 8
IDX_DIM = 64
TOPK_MAX = 256
IDX_W_SCALE = (IDX_HEADS ** -0.5) * (IDX_DIM ** -0.5)
N_GROUPS = 4
EXPERTS_PER_GROUP = 4
N_EXPERTS = N_GROUPS * EXPERTS_PER_GROUP
D_EXPERT = 256
TOP_K_IN_GROUP = 2
DEPTH = 1
ALPHA = (2 * DEPTH) ** 0.25

LANES = 128
GLA_SUB = 16
GLA_CHUNKS_PER_STEP = 4
GLA_FACTORED_MAX_DECAY = 60.0
VMEM_LIMIT = 56 * 1024 * 1024
NEG_BIG = -1e30
MXU_DTYPE = jnp.bfloat16
DSA_Q_DTYPE = MXU_DTYPE
DSA_Q_SCALE = (DSA_HEAD_DIM ** -0.5) * math.log2(math.e)

_GK = GLA_HEADS * GLA_DK
_GV = GLA_HEADS * GLA_DV
_DQ = DSA_HEADS * DSA_HEAD_DIM
_DKV = DSA_KV_HEADS * DSA_HEAD_DIM
_IQ = IDX_HEADS * IDX_DIM
_C_GQ = 0
_C_GK = _C_GQ + _GK
_C_GV = _C_GK + _GK
_C_GG = _C_GV + _GV
_C_DQ = _C_GG + _GV
_C_DK = _C_DQ + _DQ
_C_DV = _C_DK + _DKV
_C_IQ = _C_DV + _DKV
_C_MISC = _C_IQ + _IQ
_C_GA = _C_MISC + LANES
_C_GB = _C_GA + D_MODEL
_C_END = _C_GB + D_MODEL
_MISC_GR = IDX_DIM
_MISC_IW = IDX_DIM + GLA_GATE_RANK


def _bf16(a):
    return a.astype(jnp.bfloat16)


def _dot(a, b):
    return jnp.dot(a, b, preferred_element_type=jnp.float32)


def _dot_nt(a, b):
    return lax.dot_general(a, b, (((1,), (1,)), ((), ())), preferred_element_type=jnp.float32)


def _sigmoid(x):
    return 1.0 / (1.0 + jnp.exp(-x))


def _layer_norm(x, g, b):
    mu = jnp.mean(x, axis=-1, keepdims=True)
    xc = x - mu
    var = jnp.mean(xc * xc, axis=-1, keepdims=True)
    return xc * lax.rsqrt(var + LN_EPS) * g + b


def _rope_full(z, cos, sin):
    return z * cos + pltpu.roll(z, DSA_HEAD_DIM // 2, axis=1) * sin


def _rope_pair(z, cos, sin, first_half):
    q = IDX_DIM // 2
    partner = jnp.where(first_half, pltpu.roll(z, LANES - q, axis=1), pltpu.roll(z, q, axis=1))
    return z * cos + partner * sin


def _in_proj_kernel(x_ref, w_ref, wup_ref, bup_ref, cos_ref, sin_ref, cosi_ref, sini_ref,
                    gq_ref, gk_ref, gv_ref, sgg_ref, lf_ref, dq_ref, dk_ref, dv_ref, iq_ref,
                    sga_ref, sgb_ref, misc_ref):
    xb = _bf16(x_ref[...])

    def proj(c0, width):
        return _dot(xb, w_ref[:, c0:c0 + width])

    gq_ref[...] = proj(_C_GQ, _GK) * (GLA_DK ** -0.5)
    gk_ref[...] = proj(_C_GK, _GK)
    gv_ref[...] = proj(_C_GV, _GV).astype(gv_ref.dtype)
    gg = proj(_C_GG, _GV)
    sgg_ref[...] = gg * _sigmoid(gg)
    sga_ref[...] = _sigmoid(proj(_C_GA, D_MODEL))
    sgb_ref[...] = _sigmoid(proj(_C_GB, D_MODEL))

    cos = cos_ref[...]
    sin = sin_ref[...]
    zq = proj(_C_DQ, _DQ)
    for h in range(DSA_HEADS):
        hs = slice(h * DSA_HEAD_DIM, (h + 1) * DSA_HEAD_DIM)
        dq_ref[:, hs] = (_rope_full(zq[:, hs], cos, sin) * DSA_Q_SCALE).astype(dq_ref.dtype)
    zkv = proj(_C_DK, 2 * _DKV)
    for h in range(DSA_KV_HEADS):
        hs = slice(h * DSA_HEAD_DIM, (h + 1) * DSA_HEAD_DIM)
        dk_ref[:, h, :] = _rope_full(zkv[:, hs], cos, sin)
        dv_ref[:, h, :] = zkv[:, _DKV + h * DSA_HEAD_DIM:_DKV + (h + 1) * DSA_HEAD_DIM]

    cosi = cosi_ref[...]
    sini = sini_ref[...]
    lane = lax.broadcasted_iota(jnp.int32, cosi.shape, 1)
    first_half = (lane & (IDX_DIM - 1)) < (IDX_DIM // 2)
    zi = proj(_C_IQ, _IQ + LANES)
    for p in range(_IQ // LANES):
        ps = slice(p * LANES, (p + 1) * LANES)
        iq_ref[:, ps] = _rope_pair(zi[:, ps], cosi, sini, first_half).astype(iq_ref.dtype)

    misc = zi[:, _IQ:]
    pre = _dot(_bf16(misc), wup_ref[...]) + bup_ref[...]
    lf_ref[...] = (jnp.minimum(pre, 0.0) - jnp.log(1.0 + jnp.exp(-jnp.abs(pre)))) * (1.0 / GLA_TAU)
    roped = _rope_pair(misc, cosi, sini, first_half)
    misc_ref[...] = jnp.where(lane < IDX_DIM, roped, misc * IDX_W_SCALE)


def _rope_tables(T, past, rows):
    pos = (past + jnp.arange(T, dtype=jnp.int32)).astype(jnp.float32)

    def tables(dim):
        half = dim // 2
        inv = ROPE_THETA ** (-jnp.arange(half, dtype=jnp.float32) / half)
        ang = pos[:, None] * inv[None, :]
        c, s = jnp.cos(ang), jnp.sin(ang)
        reps = LANES // dim
        cos = jnp.tile(jnp.concatenate([c, c], axis=1), (1, reps))
        sin = jnp.tile(jnp.concatenate([-s, s], axis=1), (1, reps))
        if rows > T:
            cos = jnp.tile(cos, (rows // T, 1))
            sin = jnp.tile(sin, (rows // T, 1))
        return cos, sin

    return tables(DSA_HEAD_DIM) + tables(IDX_DIM)


def _prep_w_in(w_in, w_up, b_up):
    sizes = (_GK, _GK, _GV, _GV, GLA_GATE_RANK, _DQ, _DKV, _DKV, _IQ, IDX_DIM, IDX_HEADS, D_MODEL, D_MODEL)
    offs = [0]
    for s in sizes:
        offs.append(offs[-1] + s)
    gq, gk, gv, gg, gr, dq, dk, dv, iq, ik, iw, ga, gb = (w_in[:, offs[i]:offs[i + 1]] for i in range(len(sizes)))
    pad = jnp.zeros((D_MODEL, LANES - IDX_DIM - GLA_GATE_RANK - IDX_HEADS), w_in.dtype)
    w = jnp.concatenate([gq, gk, gv, gg, dq, dk, dv, iq, ik, gr, iw, pad, ga, gb], axis=1)
    wup = jnp.zeros((LANES, _GK), w_up.dtype).at[_MISC_GR:_MISC_GR + GLA_GATE_RANK].set(w_up)
    return _bf16(w), _bf16(wup), b_up.reshape(1, _GK)


def _in_proj(x2d, w, wup, bup, T, past, tm):
    n = x2d.shape[0]
    rows = max(T, tm)
    cos, sin, cosi, sini = _rope_tables(T, past, rows)
    nt = rows // tm
    tok = lambda i: (i, 0)
    tab = lambda i: (i % nt, 0)
    whole = lambda i: (0, 0)
    widths = (_GK, _GK, _GV, _GV, _GK, _DQ, _DKV, _DKV, _IQ, D_MODEL, D_MODEL, LANES)
    f32 = jnp.float32
    dtypes = (f32, f32, MXU_DTYPE, f32, f32, DSA_Q_DTYPE, f32, f32, DSA_Q_DTYPE, f32, f32, f32)
    kv_outs = (6, 7)
    kv_shape = (n, DSA_KV_HEADS, DSA_HEAD_DIM)
    kv_spec = pl.BlockSpec((tm, DSA_KV_HEADS, DSA_HEAD_DIM), lambda i: (i, 0, 0))
    return pl.pallas_call(
        _in_proj_kernel,
        grid=(n // tm,),
        in_specs=[pl.BlockSpec((tm, D_MODEL), tok),
                  pl.BlockSpec((D_MODEL, _C_END), whole),
                  pl.BlockSpec((LANES, _GK), whole),
                  pl.BlockSpec((1, _GK), whole),
                  pl.BlockSpec((tm, LANES), tab), pl.BlockSpec((tm, LANES), tab),
                  pl.BlockSpec((tm, LANES), tab), pl.BlockSpec((tm, LANES), tab)],
        out_specs=[kv_spec if i in kv_outs else pl.BlockSpec((tm, wd), tok) for i, wd in enumerate(widths)],
        out_shape=[jax.ShapeDtypeStruct(kv_shape if i in kv_outs else (n, wd), dt)
                   for i, (wd, dt) in enumerate(zip(widths, dtypes))],
        compiler_params=pltpu.CompilerParams(dimension_semantics=("arbitrary",), vmem_limit_bytes=VMEM_LIMIT),
        name="in_proj",
    )(x2d, w, wup, bup, cos, sin, cosi, sini)


def _split3(a):
    hi = _bf16(a)
    r1 = a - hi.astype(jnp.float32)
    mid = _bf16(r1)
    lo = _bf16(r1 - mid.astype(jnp.float32))
    return hi, mid, lo


def _gla_kernel(q_ref, k_ref, lf_ref, v_ref, sgg_ref, s0_ref, g_ref, b_ref, o_ref, s_ref, a_ref, *, c):
    ci = pl.program_id(1)

    @pl.when(ci == 0)
    def _():
        s_ref[...] = s0_ref[...]

    block = q_ref.shape[0]
    gam = g_ref[...]
    bet = b_ref[...]

    def causal_mask(n):
        return lax.broadcasted_iota(jnp.int32, (n, n), 0) >= lax.broadcasted_iota(jnp.int32, (n, n), 1)

    def log_decay(lf, causal):
        tri = _bf16(jnp.where(causal, 1.0, 0.0))
        hi, mid, lo = _split3(lf)
        return _dot(tri, hi) + _dot(tri, mid) + _dot(tri, lo)

    def intra_factored(q, k, b, causal):
        b0 = b[0:1]
        a = _dot_nt(_bf16(q * jnp.exp(b - b0)), _bf16(k * jnp.exp(b0 - b)))
        return jnp.where(causal, a, 0.0)

    def intra_exact(q, k, b, causal):
        nsub = q.shape[0] // GLA_SUB
        srow = lax.broadcasted_iota(jnp.int32, (GLA_SUB, GLA_SUB), 0)
        scol = lax.broadcasted_iota(jnp.int32, (GLA_SUB, GLA_SUB), 1)
        for bi in range(nsub):
            r0 = bi * GLA_SUB
            rs = slice(r0, r0 + GLA_SUB)
            qi = q[rs]
            bq = b[rs]
            ki = k[rs]
            d = jnp.zeros((GLA_SUB, GLA_SUB), jnp.float32)
            for j in range(GLA_SUB):
                f = jnp.exp(jnp.minimum(bq - bq[j:j + 1], 0.0)) * qi * ki[j:j + 1]
                d = jnp.where(scol == j, jnp.sum(f, axis=-1, keepdims=True), d)
            a_ref[rs, r0:r0 + GLA_SUB] = jnp.where(srow >= scol, d, 0.0)
            if bi > 0:
                ref_b = b[r0:r0 + 1]
                qt = qi * jnp.exp(bq - ref_b)
                kt = k[:r0] * jnp.exp(ref_b - b[:r0])
                a_ref[rs, :r0] = _dot_nt(_bf16(qt), _bf16(kt))
            if bi < nsub - 1:
                a_ref[rs, r0 + GLA_SUB:] = jnp.zeros((GLA_SUB, c - r0 - GLA_SUB), jnp.float32)
        return a_ref[...]

    def run(n, intra):
        causal = causal_mask(n)
        chunks = [slice(i * n, (i + 1) * n) for i in range(block // n)]
        bs = [log_decay(lf_ref[rows], causal) for rows in chunks]
        for h in range(GLA_HEADS):
            ks = slice(h * GLA_DK, (h + 1) * GLA_DK)
            vs = slice(h * GLA_DV, (h + 1) * GLA_DV)
            s = s_ref[h]
            for rows, b_all in zip(chunks, bs):
                q = q_ref[rows, ks]
                k = k_ref[rows, ks]
                v = _bf16(v_ref[rows, vs])
                b = b_all[:, ks]
                o = _dot(_bf16(q * jnp.exp(b)), _bf16(s)) + _dot(_bf16(intra(q, k, b, causal)), v)

                bt = b.T
                bl = bt[:, n - 1:n]
                kdt = k.T * jnp.exp(bl - bt)
                s = jnp.exp(bl) * s + _dot(_bf16(kdt), v)

                o_ref[rows, vs] = (_layer_norm(o, gam, bet) * sgg_ref[rows, vs]).astype(o_ref.dtype)
            s_ref[h] = s

    block_decay = jnp.max(-jnp.sum(lf_ref[...], axis=0, keepdims=True))
    small_decay = block_decay <= GLA_FACTORED_MAX_DECAY

    @pl.when(small_decay)
    def _():
        run(block, intra_factored)

    @pl.when(jnp.logical_not(small_decay))
    def _():
        run(c, intra_exact)


def _gla(q, k, lf, v, sgg, s0, gam, bet):
    B, T, _ = q.shape
    c = min(CHUNK, T)
    rows = c * GLA_CHUNKS_PER_STEP if T % (c * GLA_CHUNKS_PER_STEP) == 0 else c
    tokk = lambda b, i: (b, i, 0)
    st = lambda b, i: (b, 0, 0, 0)
    whole = lambda b, i: (0, 0)
    return pl.pallas_call(
        functools.partial(_gla_kernel, c=c),
        grid=(B, T // rows),
        in_specs=[pl.BlockSpec((None, rows, _GK), tokk), pl.BlockSpec((None, rows, _GK), tokk),
                  pl.BlockSpec((None, rows, _GK), tokk), pl.BlockSpec((None, rows, _GV), tokk),
                  pl.BlockSpec((None, rows, _GV), tokk),
                  pl.BlockSpec((None, GLA_HEADS, GLA_DK, GLA_DV), st),
                  pl.BlockSpec((1, GLA_DV), whole), pl.BlockSpec((1, GLA_DV), whole)],
        out_specs=[pl.BlockSpec((None, rows, _GV), tokk),
                   pl.BlockSpec((None, GLA_HEADS, GLA_DK, GLA_DV), st)],
        out_shape=[jax.ShapeDtypeStruct((B, T, _GV), MXU_DTYPE),
                   jax.ShapeDtypeStruct((B, GLA_HEADS, GLA_DK, GLA_DV), jnp.float32)],
        scratch_shapes=[pltpu.VMEM((c, c), jnp.float32)],
        compiler_params=pltpu.CompilerParams(dimension_semantics=("arbitrary", "arbitrary"),
                                             vmem_limit_bytes=VMEM_LIMIT),
        name="gla",
    )(q, k, lf, v, sgg, s0, gam, bet)


def _sortable_key(s):
    bits = pltpu.bitcast(s, jnp.int32)
    key = bits ^ ((bits >> 31) & 0x7FFFFFFF)
    return jnp.where(key == -1, 0, key)


def _dsa_kernel(qd_ref, qi_ref, wt_ref, k_ref, vt_ref, ki_ref, o_ref,
                sc_ref, hi_ref, lo_ref, bias_ref, m_ref, acc_ref, lg_ref, *, tq, tk, past, seq_len, topk):
    t0 = pl.program_id(1) * tq
    total = past + seq_len
    qpos = past + t0 + lax.broadcasted_iota(jnp.int32, (1, tq), 1)
    limit = jnp.minimum(((qpos >> 6) + 1) * CHUNK, total)
    lim_max = jnp.minimum(((past + t0 + tq - 1) // CHUNK + 1) * CHUNK, total)
    nk = (lim_max + tk - 1) // tk
    lp = sc_ref.shape[0] * tk
    key_iota = lax.broadcasted_iota(jnp.int32, (tk, tq), 0)

    qis = [_bf16(qi_ref[:, h * IDX_DIM:(h + 1) * IDX_DIM]) for h in range(IDX_HEADS)]
    neg_inf = jnp.float32(-jnp.inf)

    def score_tile(kt, carry):
        kit = ki_ref[pl.ds(pl.multiple_of(kt * tk, tk), tk), :]
        s = None
        for h in range(IDX_HEADS):
            term = wt_ref[h:h + 1, :] * jnp.maximum(_dot_nt(kit, qis[h]), 0.0)
            s = term if s is None else s + term
        s = jnp.where(key_iota < limit - kt * tk, s, neg_inf)
        sc_ref[kt] = s
        key = _sortable_key(s)
        hi_ref[kt] = (key >> 16).astype(jnp.int16)
        lo_ref[kt] = ((key & 0xFFFF) - _HALF).astype(jnp.int16)
        return carry

    lax.fori_loop(0, nk, score_tile, 0)

    def count_ge(ref, cand_off):
        cb = jnp.broadcast_to((cand_off - _HALF).astype(jnp.int16), (_CNT_ROWS, tq))

        def body(kt, acc):
            tile = ref[kt]
            for j in range(tk // _CNT_ROWS):
                acc = acc + jnp.where(tile[j * _CNT_ROWS:(j + 1) * _CNT_ROWS] >= cb, jnp.int16(1), jnp.int16(0))
            return acc

        acc = lax.fori_loop(0, nk, body, jnp.zeros((_CNT_ROWS, tq), jnp.int16))
        return jnp.sum(acc.astype(jnp.int32), axis=0, keepdims=True)

    def count_above(ref, off):
        return jnp.where(off >= 2 * _HALF - 1, 0, count_ge(ref, jnp.minimum(off + 1, 2 * _HALF - 1)))

    def radix_search(ref, target, settled):
        def cond(state):
            i, _, _, open_lanes = state
            return (i < 16) & (open_lanes > 0)

        def bits(state):
            i, off, done, _ = state
            for j in range(_BITS_PER_CHECK):
                cand = off | jnp.left_shift(jnp.int32(1), 15 - (i + j))
                cnt = count_ge(ref, cand)
                off = jnp.where((done > 0) | (cnt < target), off, cand)
                done = jnp.where(cnt == target, 1, done)
            return i + _BITS_PER_CHECK, off, done, 1 - jnp.min(done)

        _, off, done, open_lanes = lax.while_loop(
            cond, bits, (jnp.int32(0), jnp.zeros((1, tq), jnp.int32), settled, 1 - jnp.min(settled)))
        return off, done, open_lanes

    settled0 = jnp.where(limit <= topk, 1, 0)
    hi_off, settled1, open1 = radix_search(hi_ref, topk, settled0)
    hi16 = (hi_off - _HALF).astype(jnp.int16)

    n_above_hi = lax.cond(open1 > 0, lambda: count_above(hi_ref, hi_off), lambda: jnp.zeros((1, tq), jnp.int32))

    @pl.when(open1 > 0)
    def _():
        def band_tile(kt, carry):
            lo_ref[kt] = jnp.where(hi_ref[kt] == hi16, lo_ref[kt], jnp.int16(-_HALF))
            return carry

        lax.fori_loop(0, nk, band_tile, 0)

    need_lo = topk - n_above_hi
    lo_off, settled2, open2 = lax.cond(
        open1 > 0, lambda: radix_search(lo_ref, need_lo, settled1),
        lambda: (jnp.zeros((1, tq), jnp.int32), settled1, jnp.int32(0)))

    del settled2, open2
    thr_key = (hi_off - _HALF) * (2 * _HALF) + jnp.where(settled1 > 0, 0, lo_off)

    def key_to_score(key):
        t = pltpu.bitcast(key ^ ((key >> 31) & 0x7FFFFFFF), jnp.float32)
        return jnp.where(t != t, neg_inf, t)

    def count_scores(t):
        tb = jnp.broadcast_to(t, (_SCORE_CNT_ROWS, tq))

        def body(kt, accs):
            gt, ge = accs
            tile = sc_ref[kt]
            for j in range(tk // _SCORE_CNT_ROWS):
                rows = tile[j * _SCORE_CNT_ROWS:(j + 1) * _SCORE_CNT_ROWS]
                gt = gt + jnp.where(rows > tb, 1, 0)
                ge = ge + jnp.where(rows >= tb, 1, 0)
            return gt, ge

        zero = jnp.zeros((_SCORE_CNT_ROWS, tq), jnp.int32)
        gt, ge = lax.fori_loop(0, nk, body, (zero, zero))
        return jnp.sum(gt, axis=0, keepdims=True), jnp.sum(ge, axis=0, keepdims=True)

    def is_kth(n_gt, n_ge):
        return (settled0 > 0) | ((n_gt < topk) & (n_ge >= topk))

    def search_on_scores():
        int_min = jnp.int32(-2 ** 31)

        def bit(i, key):
            cand = key ^ jnp.left_shift(jnp.int32(1), 31 - i)
            _, n_ge = count_scores(key_to_score(cand))
            return jnp.where(n_ge >= topk, cand, key)

        t = key_to_score(lax.fori_loop(0, 32, bit, jnp.full((1, tq), int_min, jnp.int32)))
        return (t,) + count_scores(t)

    thr = jnp.where(settled0 > 0, neg_inf, key_to_score(thr_key))
    n_gt, n_ge = count_scores(thr)
    confirmed = jnp.min(jnp.where(is_kth(n_gt, n_ge), 1, 0))
    thr, n_gt, n_ge = lax.cond(confirmed > 0, lambda: (thr, n_gt, n_ge), search_on_scores)
    thr = jnp.where(settled0 > 0, neg_inf, thr)

    tied = (n_ge > topk) & (settled0 == 0)
    need = jnp.where(tied, topk - n_gt, 2 * lp).astype(jnp.float32)
    any_tied = jnp.max(jnp.where(tied, 1, 0))

    @pl.when(any_tied == 0)
    def _():
        def select_tile(kt, carry):
            sel = (sc_ref[kt] >= thr) & (key_iota < limit - kt * tk)
            bias_ref[kt] = jnp.where(sel, 0.0, NEG_BIG)
            return carry

        lax.fori_loop(0, nk, select_tile, 0)

    @pl.when(any_tied > 0)
    def _():
        krow = lax.broadcasted_iota(jnp.int32, (tk, tk), 0)
        kcol = lax.broadcasted_iota(jnp.int32, (tk, tk), 1)
        prefix_ones = _bf16(jnp.where(krow >= kcol, 1.0, 0.0))

        def select_tile(kt, seen):
            s = sc_ref[kt]
            eq = s == thr
            rank = _dot(prefix_ones, _bf16(jnp.where(eq, 1.0, 0.0))) + seen
            sel = ((s > thr) | (eq & (rank <= need))) & (key_iota < limit - kt * tk)
            bias_ref[kt] = jnp.where(sel, 0.0, NEG_BIG)
            return rank[tk - 1:tk]

        lax.fori_loop(0, nk, select_tile, jnp.zeros((1, tq), jnp.float32))

    rep = DSA_HEADS // DSA_KV_HEADS
    m_ref[...] = jnp.full(m_ref.shape, NEG_BIG, jnp.float32)
    acc_ref[...] = jnp.zeros(acc_ref.shape, jnp.float32)
    qs = [_bf16(qd_ref[:, h * DSA_HEAD_DIM:(h + 1) * DSA_HEAD_DIM]) for h in range(DSA_HEADS)]

    def attend_tile(kt, carry):
        rows = pl.ds(pl.multiple_of(kt * tk, tk), tk)
        bias = bias_ref[kt]

        def logits(h):
            g = h // rep
            return _dot_nt(k_ref[rows, g * DSA_HEAD_DIM:(g + 1) * DSA_HEAD_DIM], qs[h])

        for h in range(_LG_BUFS - 1):
            lg_ref[h] = logits(h)
        for h in range(DSA_HEADS):
            if h + _LG_BUFS - 1 < DSA_HEADS:
                lg_ref[(h + _LG_BUFS - 1) % _LG_BUFS] = logits(h + _LG_BUFS - 1)
            lb = lg_ref[h % _LG_BUFS] + bias
            m_old = m_ref[h]
            m_new = jnp.maximum(m_old, jnp.max(lb, axis=0, keepdims=True))
            p = jnp.exp2(lb - m_new)
            a = jnp.exp2(m_old - m_new)
            vtg = vt_ref[kt, h // rep]
            acc_ref[h] = a * acc_ref[h] + _dot(vtg, _bf16(p))
            m_ref[h] = m_new
        return carry

    lax.fori_loop(0, nk, attend_tile, 0)
    for h in range(DSA_HEADS):
        acc = acc_ref[h]
        o_t = acc[:DSA_HEAD_DIM] / acc[DSA_HEAD_DIM:DSA_HEAD_DIM + 1]
        o_ref[:, h * DSA_HEAD_DIM:(h + 1) * DSA_HEAD_DIM] = o_t.T.astype(o_ref.dtype)


DSA_KEY_TILE = 512
_HALF = 1 << 15
_BITS_PER_CHECK = 4
_LG_BUFS = 4
_SCORE_CNT_ROWS = 16
_CNT_ROWS = 64
_VT_ROWS = DSA_HEAD_DIM + 16


def _dsa(qd, qi, wt, k_all, vt, ki_all, past, seq_len, tq, tk):
    B, tp, _ = qd.shape
    lp = k_all.shape[1]
    nkt = lp // tk
    topk = min(TOPK_MAX, (past + seq_len) // 4)
    tokk = lambda b, i: (b, i, 0)
    kv = lambda b, i: (b, 0, 0)
    return pl.pallas_call(
        functools.partial(_dsa_kernel, tq=tq, tk=tk, past=past, seq_len=seq_len, topk=topk),
        grid=(B, tp // tq),
        in_specs=[pl.BlockSpec((None, tq, _DQ), tokk), pl.BlockSpec((None, tq, _IQ), tokk),
                  pl.BlockSpec((None, IDX_HEADS, tq), lambda b, i: (b, 0, i)),
                  pl.BlockSpec((None, lp, _DKV), kv),
                  pl.BlockSpec((None, nkt, DSA_KV_HEADS, _VT_ROWS, tk), lambda b, i: (b, 0, 0, 0, 0)),
                  pl.BlockSpec((None, lp, IDX_DIM), kv)],
        out_specs=pl.BlockSpec((None, tq, _DQ), tokk),
        out_shape=jax.ShapeDtypeStruct((B, tp, _DQ), MXU_DTYPE),
        scratch_shapes=[pltpu.VMEM((nkt, tk, tq), jnp.float32),
                        pltpu.VMEM((nkt, tk, tq), jnp.int16), pltpu.VMEM((nkt, tk, tq), jnp.int16),
                        pltpu.VMEM((nkt, tk, tq), jnp.float32),
                        pltpu.VMEM((DSA_HEADS, 1, tq), jnp.float32),
                        pltpu.VMEM((DSA_HEADS, _VT_ROWS, tq), jnp.float32),
                        pltpu.VMEM((_LG_BUFS, tk, tq), jnp.float32)],
        compiler_params=pltpu.CompilerParams(dimension_semantics=("arbitrary", "arbitrary"),
                                             vmem_limit_bytes=VMEM_LIMIT),
        name="dsa",
    )(qd, qi, wt, k_all, vt, ki_all)


def _merge_kernel(og_ref, od_ref, sga_ref, sgb_ref, x_ref, wg_ref, wd_ref, wo_ref, g_ref, b_ref, h_ref):
    m = sga_ref[...] * _dot(_bf16(og_ref[...]), wg_ref[...]) + sgb_ref[...] * _dot(_bf16(od_ref[...]), wd_ref[...])
    mix = _dot(_bf16(m), wo_ref[...])
    h_ref[...] = _layer_norm(ALPHA * x_ref[...] + mix, g_ref[...], b_ref[...])


def _merge(og, od, sga, sgb, x2d, wg, wd, wo, g, b, tm):
    n = x2d.shape[0]
    tok = lambda i: (i, 0)
    whole = lambda i: (0, 0)
    act = pl.BlockSpec((tm, D_MODEL), tok)
    wsp = pl.BlockSpec((D_MODEL, D_MODEL), whole)
    vec = pl.BlockSpec((1, D_MODEL), whole)
    return pl.pallas_call(
        _merge_kernel,
        grid=(n // tm,),
        in_specs=[act, act, act, act, act, wsp, wsp, wsp, vec, vec],
        out_specs=act,
        out_shape=jax.ShapeDtypeStruct((n, D_MODEL), jnp.float32),
        compiler_params=pltpu.CompilerParams(dimension_semantics=("arbitrary",), vmem_limit_bytes=VMEM_LIMIT),
        name="merge",
    )(og, od, sga, sgb, x2d, wg, wd, wo, g, b)


def _first_lane_where(mask, lane):
    return jnp.min(jnp.where(mask, lane, LANES), axis=-1, keepdims=True)


def _route(logits):
    lane = lax.broadcasted_iota(jnp.int32, logits.shape, 1)
    is_g = lane < N_GROUPS
    gl = jnp.where(is_g, logits, NEG_BIG)
    gmax = jnp.max(gl, axis=-1, keepdims=True)
    g_sel = _first_lane_where(is_g & (gl == gmax), lane)
    g_prob = 1.0 / jnp.sum(jnp.where(is_g, jnp.exp(gl - gmax), 0.0), axis=-1, keepdims=True)
    e_lane = lane - N_GROUPS
    in_grp = (e_lane >= 0) & (e_lane < N_EXPERTS) & ((e_lane >> 2) == g_sel)
    el = jnp.where(in_grp, logits, NEG_BIG)
    v1 = jnp.max(el, axis=-1, keepdims=True)
    i1 = _first_lane_where(in_grp & (el == v1), lane)
    rest = in_grp & (lane != i1)
    el2 = jnp.where(rest, logits, NEG_BIG)
    v2 = jnp.max(el2, axis=-1, keepdims=True)
    i2 = _first_lane_where(rest & (el2 == v2), lane)
    e2 = jnp.exp(v2 - v1)
    w1 = g_prob / (1.0 + e2)
    w2 = g_prob * e2 / (1.0 + e2)
    return jnp.where(lane == i1, w1, 0.0) + jnp.where(lane == i2, w2, 0.0)


def _moe_kernel(h_ref, wr_ref, br_ref, wg_ref, wu_ref, wd_ref, g_ref, b_ref, y_ref, gate_ref, acc_ref):
    e = pl.program_id(1)
    h = h_ref[...]

    hb = _bf16(h)

    @pl.when(e == 0)
    def _():
        h_lo = _bf16(h - hb.astype(jnp.float32))
        logits = _dot(hb, wr_ref[0]) + (_dot(h_lo, wr_ref[0]) + _dot(hb, wr_ref[1])) + br_ref[...]
        gate_ref[...] = _route(logits)
        acc_ref[...] = jnp.zeros(acc_ref.shape, jnp.float32)

    lane = lax.broadcasted_iota(jnp.int32, gate_ref.shape, 1)
    gcol = jnp.sum(jnp.where(lane == e + N_GROUPS, gate_ref[...], 0.0), axis=-1, keepdims=True)
    a = _dot(hb, wg_ref[...])
    hid = a * _sigmoid(a) * _dot(hb, wu_ref[...]) * gcol
    acc_ref[...] += _dot(_bf16(hid), wd_ref[...])

    @pl.when(e == N_EXPERTS - 1)
    def _():
        y_ref[...] = _layer_norm(ALPHA * h + acc_ref[...], g_ref[...], b_ref[...])


def _moe(h2d, wr, br, wg, wu, wd, g, b, tm):
    n = h2d.shape[0]
    tok = lambda i, e: (i, 0)
    whole = lambda i, e: (0, 0)
    per_e = lambda i, e: (e, 0, 0)
    return pl.pallas_call(
        _moe_kernel,
        grid=(n // tm, N_EXPERTS),
        in_specs=[pl.BlockSpec((tm, D_MODEL), tok),
                  pl.BlockSpec((2, D_MODEL, LANES), lambda i, e: (0, 0, 0)), pl.BlockSpec((1, LANES), whole),
                  pl.BlockSpec((None, D_MODEL, D_EXPERT), per_e),
                  pl.BlockSpec((None, D_MODEL, D_EXPERT), per_e),
                  pl.BlockSpec((None, D_EXPERT, D_MODEL), per_e),
                  pl.BlockSpec((1, D_MODEL), whole), pl.BlockSpec((1, D_MODEL), whole)],
        out_specs=pl.BlockSpec((tm, D_MODEL), tok),
        out_shape=jax.ShapeDtypeStruct((n, D_MODEL), jnp.float32),
        scratch_shapes=[pltpu.VMEM((tm, LANES), jnp.float32), pltpu.VMEM((tm, D_MODEL), jnp.float32)],
        compiler_params=pltpu.CompilerParams(dimension_semantics=("arbitrary", "arbitrary"),
                                             vmem_limit_bytes=VMEM_LIMIT),
        name="moe",
    )(h2d, wr, br, wg, wu, wd, g, b)


def _prep_weights(w_in, w_up, b_up, gla_g, gla_b, w_bg, w_bd, w_o, ln1_g, ln1_b, w_rg, b_rg, w_re, b_re,
                  w_eg, w_eu, w_ed, ln2_g, ln2_b):
    w, wup, bup = _prep_w_in(w_in, w_up, b_up)
    w_re2 = jnp.transpose(w_re, (1, 0, 2)).reshape(D_MODEL, N_EXPERTS)
    rpad = jnp.zeros((D_MODEL, LANES - N_GROUPS - N_EXPERTS), w_rg.dtype)
    wr = jnp.concatenate([w_rg, w_re2, rpad], axis=1)
    wr_hi = _bf16(wr)
    wr = jnp.stack([wr_hi, _bf16(wr - wr_hi.astype(jnp.float32))])
    br =jnp.concatenate([b_rg, b_re.reshape(N_EXPERTS), jnp.zeros((LANES - N_GROUPS - N_EXPERTS,), b_rg.dtype)])
    row = lambda a: a.reshape(1, -1)
    return dict(w=w, wup=wup, bup=bup, gla_g=row(gla_g), gla_b=row(gla_b), wbg=_bf16(w_bg), wbd=_bf16(w_bd),
                wo=_bf16(w_o), ln1_g=row(ln1_g), ln1_b=row(ln1_b), wr=wr, br=row(br), weg=_bf16(w_eg),
                weu=_bf16(w_eu), wed=_bf16(w_ed), ln2_g=row(ln2_g), ln2_b=row(ln2_b))


MOE_TOKEN_TILE = 1024


def _token_tile(n):
    for tm in (512, 256, 128):
        if n % tm == 0:
            return tm
    raise ValueError(f"token count {n} is not a multiple of 128")


def _trunk_layer(x, s0, past_k, past_v, past_ki, p):
    B, T, _ = x.shape
    past = past_k.shape[1]
    n = B * T
    tm = _token_tile(n)
    assert T % tm == 0 or tm % T == 0
    x2d = x.reshape(n, D_MODEL)
    gq, gk, gv, sgg, lf, dq, dk, dv, iq, sga, sgb, misc = _in_proj(x2d, p["w"], p["wup"], p["bup"], T, past, tm)
    seq = lambda a: a.reshape(B, T, a.shape[-1])

    o_gla, s_fin = _gla(seq(gq), seq(gk), seq(lf), seq(gv), seq(sgg), s0, p["gla_g"], p["gla_b"])

    kv_rows = lambda a: a.reshape(B, T, _DKV)
    k_new, v_new, ki_new = kv_rows(dk), kv_rows(dv), seq(misc)[:, :, :IDX_DIM]
    total = past + T
    tk = DSA_KEY_TILE
    lp = -(-total // tk) * tk

    def keys(past_rows, new_rows):
        rows = jnp.concatenate([past_rows.reshape(B, past, new_rows.shape[-1]), new_rows], axis=1)
        return _bf16(jnp.pad(rows, ((0, 0), (0, lp - total), (0, 0))))

    nkt = lp // tk
    vt = jnp.transpose(keys(past_v, v_new).reshape(B, nkt, tk, _DKV), (0, 1, 3, 2))
    vt = vt.reshape(B, nkt, DSA_KV_HEADS, DSA_HEAD_DIM, tk)
    ones = jnp.ones((B, nkt, DSA_KV_HEADS, 1, tk), jnp.bfloat16)
    zpad = jnp.zeros((B, nkt, DSA_KV_HEADS, _VT_ROWS - DSA_HEAD_DIM - 1, tk), jnp.bfloat16)
    vt = jnp.concatenate([vt, ones, zpad], axis=3)

    tq = 256 if T % 256 == 0 else LANES
    tp = -(-T // tq) * tq
    qpad = lambda a: jnp.pad(a, ((0, 0), (0, tp - T), (0, 0)))
    wt = jnp.transpose(qpad(seq(misc)[:, :, _MISC_IW:_MISC_IW + IDX_HEADS]), (0, 2, 1))
    o_dsa = _dsa(qpad(seq(dq)), qpad(seq(iq)), wt, keys(past_k, k_new), vt, keys(past_ki, ki_new),
                 past, T, tq, tk)[:, :T]

    h = _merge(o_gla.reshape(n, _GV), o_dsa.reshape(n, _DQ), sga, sgb, x2d, p["wbg"], p["wbd"], p["wo"],
               p["ln1_g"], p["ln1_b"], tm)
    tm_moe = MOE_TOKEN_TILE if n % MOE_TOKEN_TILE == 0 else tm
    y = _moe(h, p["wr"], p["br"], p["weg"], p["weu"], p["wed"], p["ln2_g"], p["ln2_b"], tm_moe)
    cache_rows = lambda a: a.reshape(B, T, DSA_KV_HEADS, DSA_HEAD_DIM)
    return y.reshape(B, T, D_MODEL), s_fin, cache_rows(dk), cache_rows(dv), ki_new


def kernel(x_prompt, x_sample, state_gla, cache_k, cache_v, cache_k_idx, w_in, w_gla_gate_up, b_gla_gate,
           gla_norm_g, gla_norm_b, w_branch_gla, w_branch_dsa, w_out, ln1_g, ln1_b, w_router_group,
           b_router_group, w_router_expert, b_router_expert, w_expert_gate, w_expert_up, w_expert_down,
           ln2_g, ln2_b):
    B = x_prompt.shape[0]
    l = 0
    p = _prep_weights(w_in[l], w_gla_gate_up[l], b_gla_gate[l], gla_norm_g[l], gla_norm_b[l], w_branch_gla[l],
                      w_branch_dsa[l], w_out[l], ln1_g[l], ln1_b[l], w_router_group[l], b_router_group[l],
                      w_router_expert[l], b_router_expert[l], w_expert_gate[l], w_expert_up[l],
                      w_expert_down[l], ln2_g[l], ln2_b[l])
    s0 = jnp.zeros((B, GLA_HEADS, GLA_DK, GLA_DV), jnp.float32)
    ek = jnp.zeros((B, 0, DSA_KV_HEADS, DSA_HEAD_DIM), x_prompt.dtype)
    eki = jnp.zeros((B, 0, IDX_DIM), x_prompt.dtype)
    yp, sp, kp, vp, kip = _trunk_layer(x_prompt, s0, ek, ek, eki, p)
    ys, ss, ksn, vsn, kisn = _trunk_layer(x_sample, state_gla[l], cache_k[l], cache_v[l], cache_k_idx[l], p)
    stack = lambda a: a[None]
    return (yp, ys, stack(sp), stack(kp), stack(vp), stack(kip), stack(ss), stack(ksn), stack(vsn), stack(kisn))
```

```python
import functools
import math

import jax
import jax.numpy as jnp
from jax import lax
from jax.experimental import pallas as pl
from jax.experimental.pallas import tpu as pltpu

D_MODEL = 1024
CHUNK = 64
ROPE_THETA = 10000.0
LN_EPS = 1e-5
GLA_HEADS = 4
GLA_DK = D_MODEL // 2 // GLA_HEADS
GLA_DV = D_MODEL // GLA_HEADS
GLA_GATE_RANK = 16
GLA_TAU = 16.0
DSA_HEADS = 8
DSA_KV_HEADS = 2
DSA_HEAD_DIM = D_MODEL // DSA_HEADS
IDX_HEADS = 8
IDX_DIM = 64
TOPK_MAX = 256
IDX_W_SCALE = (IDX_HEADS ** -0.5) * (IDX_DIM ** -0.5)
N_GROUPS = 4
EXPERTS_PER_GROUP = 4
N_EXPERTS = N_GROUPS * EXPERTS_PER_GROUP
D_EXPERT = 256
TOP_K_IN_GROUP = 2
DEPTH = 1
ALPHA = (2 * DEPTH) ** 0.25

LANES = 128
GLA_SUB = 16
GLA_CHUNKS_PER_STEP = 4
GLA_FACTORED_MAX_DECAY = 60.0
VMEM_LIMIT = 56 * 1024 * 1024
NEG_BIG = -1e30
MXU_DTYPE = jnp.bfloat16
DSA_Q_DTYPE = MXU_DTYPE
DSA_Q_SCALE = (DSA_HEAD_DIM ** -0.5) * math.log2(math.e)

_GK = GLA_HEADS * GLA_DK
_GV = GLA_HEADS * GLA_DV
_DQ = DSA_HEADS * DSA_HEAD_DIM
_DKV = DSA_KV_HEADS * DSA_HEAD_DIM
_IQ = IDX_HEADS * IDX_DIM
_C_GQ = 0
_C_GK = _C_GQ + _GK
_C_GV = _C_GK + _GK
_C_GG = _C_GV + _GV
_C_DQ = _C_GG + _GV
_C_DK = _C_DQ + _DQ
_C_DV = _C_DK + _DKV
_C_IQ = _C_DV + _DKV
_C_MISC = _C_IQ + _IQ
_C_GA = _C_MISC + LANES
_C_GB = _C_GA + D_MODEL
_C_END = _C_GB + D_MODEL
_MISC_GR = IDX_DIM
_MISC_IW = IDX_DIM + GLA_GATE_RANK


def _bf16(a):
    return a.astype(jnp.bfloat16)


def _dot(a, b):
    return jnp.dot(a, b, preferred_element_type=jnp.float32)


def _dot_nt(a, b):
    return lax.dot_general(a, b, (((1,), (1,)), ((), ())), preferred_element_type=jnp.float32)


def _sigmoid(x):
    return 1.0 / (1.0 + jnp.exp(-x))


def _layer_norm(x, g, b):
    mu = jnp.mean(x, axis=-1, keepdims=True)
    xc = x - mu
    var = jnp.mean(xc * xc, axis=-1, keepdims=True)
    return xc * lax.rsqrt(var + LN_EPS) * g + b


def _rope_full(z, cos, sin):
    return z * cos + pltpu.roll(z, DSA_HEAD_DIM // 2, axis=1) * sin


def _rope_pair(z, cos, sin, first_half):
    q = IDX_DIM // 2
    partner = jnp.where(first_half, pltpu.roll(z, LANES - q, axis=1), pltpu.roll(z, q, axis=1))
    return z * cos + partner * sin


def _in_proj_kernel(x_ref, w_ref, wup_ref, bup_ref, cos_ref, sin_ref, cosi_ref, sini_ref,
                    gq_ref, gk_ref, gv_ref, sgg_ref, lf_ref, dq_ref, dk_ref, dv_ref, iq_ref,
                    sga_ref, sgb_ref, misc_ref):
    xb = _bf16(x_ref[...])

    def proj(c0, width):
        return _dot(xb, w_ref[:, c0:c0 + width])

    gq_ref[...] = proj(_C_GQ, _GK) * (GLA_DK ** -0.5)
    gk_ref[...] = proj(_C_GK, _GK)
    gv_ref[...] = proj(_C_GV, _GV).astype(gv_ref.dtype)
    gg = proj(_C_GG, _GV)
    sgg_ref[...] = gg * _sigmoid(gg)
    sga_ref[...] = _sigmoid(proj(_C_GA, D_MODEL))
    sgb_ref[...] = _sigmoid(proj(_C_GB, D_MODEL))

    cos = cos_ref[...]
    sin = sin_ref[...]
    zq = proj(_C_DQ, _DQ)
    for h in range(DSA_HEADS):
        hs = slice(h * DSA_HEAD_DIM, (h + 1) * DSA_HEAD_DIM)
        dq_ref[:, hs] = (_rope_full(zq[:, hs], cos, sin) * DSA_Q_SCALE).astype(dq_ref.dtype)
    zkv = proj(_C_DK, 2 * _DKV)
    for h in range(DSA_KV_HEADS):
        hs = slice(h * DSA_HEAD_DIM, (h + 1) * DSA_HEAD_DIM)
        dk_ref[:, h, :] = _rope_full(zkv[:, hs], cos, sin)
        dv_ref[:, h, :] = zkv[:, _DKV + h * DSA_HEAD_DIM:_DKV + (h + 1) * DSA_HEAD_DIM]

    cosi = cosi_ref[...]
    sini = sini_ref[...]
    lane = lax.broadcasted_iota(jnp.int32, cosi.shape, 1)
    first_half = (lane & (IDX_DIM - 1)) < (IDX_DIM // 2)
    zi = proj(_C_IQ, _IQ + LANES)
    for p in range(_IQ // LANES):
        ps = slice(p * LANES, (p + 1) * LANES)
        iq_ref[:, ps] = _rope_pair(zi[:, ps], cosi, sini, first_half).astype(iq_ref.dtype)

    misc = zi[:, _IQ:]
    pre = _dot(_bf16(misc), wup_ref[...]) + bup_ref[...]
    lf_ref[...] = (jnp.minimum(pre, 0.0) - jnp.log(1.0 + jnp.exp(-jnp.abs(pre)))) * (1.0 / GLA_TAU)
    roped = _rope_pair(misc, cosi, sini, first_half)
    misc_ref[...] = jnp.where(lane < IDX_DIM, roped, misc * IDX_W_SCALE)


def _rope_tables(T, past, rows):
    pos = (past + jnp.arange(T, dtype=jnp.int32)).astype(jnp.float32)

    def tables(dim):
        half = dim // 2
        inv = ROPE_THETA ** (-jnp.arange(half, dtype=jnp.float32) / half)
        ang = pos[:, None] * inv[None, :]
        c, s = jnp.cos(ang), jnp.sin(ang)
        reps = LANES // dim
        cos = jnp.tile(jnp.concatenate([c, c], axis=1), (1, reps))
        sin = jnp.tile(jnp.concatenate([-s, s], axis=1), (1, reps))
        if rows > T:
            cos = jnp.tile(cos, (rows // T, 1))
            sin = jnp.tile(sin, (rows // T, 1))
        return cos, sin

    return tables(DSA_HEAD_DIM) + tables(IDX_DIM)


def _prep_w_in(w_in, w_up, b_up):
    sizes = (_GK, _GK, _GV, _GV, GLA_GATE_RANK, _DQ, _DKV, _DKV, _IQ, IDX_DIM, IDX_HEADS, D_MODEL, D_MODEL)
    offs = [0]
    for s in sizes:
        offs.append(offs[-1] + s)
    gq, gk, gv, gg, gr, dq, dk, dv, iq, ik, iw, ga, gb = (w_in[:, offs[i]:offs[i + 1]] for i in range(len(sizes)))
    pad = jnp.zeros((D_MODEL, LANES - IDX_DIM - GLA_GATE_RANK - IDX_HEADS), w_in.dtype)
    w = jnp.concatenate([gq, gk, gv, gg, dq, dk, dv, iq, ik, gr, iw, pad, ga, gb], axis=1)
    wup = jnp.zeros((LANES, _GK), w_up.dtype).at[_MISC_GR:_MISC_GR + GLA_GATE_RANK].set(w_up)
    return _bf16(w), _bf16(wup), b_up.reshape(1, _GK)


def _in_proj(x2d, w, wup, bup, T, past, tm):
    n = x2d.shape[0]
    rows = max(T, tm)
    cos, sin, cosi, sini = _rope_tables(T, past, rows)
    nt = rows // tm
    tok = lambda i: (i, 0)
    tab = lambda i: (i % nt, 0)
    whole = lambda i: (0, 0)
    widths = (_GK, _GK, _GV, _GV, _GK, _DQ, _DKV, _DKV, _IQ, D_MODEL, D_MODEL, LANES)
    f32 = jnp.float32
    dtypes = (f32, f32, MXU_DTYPE, f32, f32, DSA_Q_DTYPE, f32, f32, DSA_Q_DTYPE, f32, f32, f32)
    kv_outs = (6, 7)
    kv_shape = (n, DSA_KV_HEADS, DSA_HEAD_DIM)
    kv_spec = pl.BlockSpec((tm, DSA_KV_HEADS, DSA_HEAD_DIM), lambda i: (i, 0, 0))
    return pl.pallas_call(
        _in_proj_kernel,
        grid=(n // tm,),
        in_specs=[pl.BlockSpec((tm, D_MODEL), tok),
                  pl.BlockSpec((D_MODEL, _C_END), whole),
                  pl.BlockSpec((LANES, _GK), whole),
                  pl.BlockSpec((1, _GK), whole),
                  pl.BlockSpec((tm, LANES), tab), pl.BlockSpec((tm, LANES), tab),
                  pl.BlockSpec((tm, LANES), tab), pl.BlockSpec((tm, LANES), tab)],
        out_specs=[kv_spec if i in kv_outs else pl.BlockSpec((tm, wd), tok) for i, wd in enumerate(widths)],
        out_shape=[jax.ShapeDtypeStruct(kv_shape if i in kv_outs else (n, wd), dt)
                   for i, (wd, dt) in enumerate(zip(widths, dtypes))],
        compiler_params=pltpu.CompilerParams(dimension_semantics=("arbitrary",), vmem_limit_bytes=VMEM_LIMIT),
        name="in_proj",
    )(x2d, w, wup, bup, cos, sin, cosi, sini)


def _split3(a):
    hi = _bf16(a)
    r1 = a - hi.astype(jnp.float32)
    mid = _bf16(r1)
    lo = _bf16(r1 - mid.astype(jnp.float32))
    return hi, mid, lo


def _gla_kernel(q_ref, k_ref, lf_ref, v_ref, sgg_ref, s0_ref, g_ref, b_ref, o_ref, s_ref, a_ref, *, c):
    ci = pl.program_id(1)

    @pl.when(ci == 0)
    def _():
        s_ref[...] = s0_ref[...]

    block = q_ref.shape[0]
    gam = g_ref[...]
    bet = b_ref[...]

    def causal_mask(n):
        return lax.broadcasted_iota(jnp.int32, (n, n), 0) >= lax.broadcasted_iota(jnp.int32, (n, n), 1)

    def log_decay(lf, causal):
        tri = _bf16(jnp.where(causal, 1.0, 0.0))
        hi, mid, lo = _split3(lf)
        return _dot(tri, hi) + _dot(tri, mid) + _dot(tri, lo)

    def intra_factored(q, k, b, causal):
        b0 = b[0:1]
        a = _dot_nt(_bf16(q * jnp.exp(b - b0)), _bf16(k * jnp.exp(b0 - b)))
        return jnp.where(causal, a, 0.0)

    def intra_exact(q, k, b, causal):
        nsub = q.shape[0] // GLA_SUB
        srow = lax.broadcasted_iota(jnp.int32, (GLA_SUB, GLA_SUB), 0)
        scol = lax.broadcasted_iota(jnp.int32, (GLA_SUB, GLA_SUB), 1)
        for bi in range(nsub):
            r0 = bi * GLA_SUB
            rs = slice(r0, r0 + GLA_SUB)
            qi = q[rs]
            bq = b[rs]
            ki = k[rs]
            d = jnp.zeros((GLA_SUB, GLA_SUB), jnp.float32)
            for j in range(GLA_SUB):
                f = jnp.exp(jnp.minimum(bq - bq[j:j + 1], 0.0)) * qi * ki[j:j + 1]
                d = jnp.where(scol == j, jnp.sum(f, axis=-1, keepdims=True), d)
            a_ref[rs, r0:r0 + GLA_SUB] = jnp.where(srow >= scol, d, 0.0)
            if bi > 0:
                ref_b = b[r0:r0 + 1]
                qt = qi * jnp.exp(bq - ref_b)
                kt = k[:r0] * jnp.exp(ref_b - b[:r0])
                a_ref[rs, :r0] = _dot_nt(_bf16(qt), _bf16(kt))
            if bi < nsub - 1:
                a_ref[rs, r0 + GLA_SUB:] = jnp.zeros((GLA_SUB, c - r0 - GLA_SUB), jnp.float32)
        return a_ref[...]

    def run(n, intra):
        causal = causal_mask(n)
        chunks = [slice(i * n, (i + 1) * n) for i in range(block // n)]
        bs = [log_decay(lf_ref[rows], causal) for rows in chunks]
        for h in range(GLA_HEADS):
            ks = slice(h * GLA_DK, (h + 1) * GLA_DK)
            vs = slice(h * GLA_DV, (h + 1) * GLA_DV)
            s = s_ref[h]
            for rows, b_all in zip(chunks, bs):
                q = q_ref[rows, ks]
                k = k_ref[rows, ks]
                v = _bf16(v_ref[rows, vs])
                b = b_all[:, ks]
                o = _dot(_bf16(q * jnp.exp(b)), _bf16(s)) + _dot(_bf16(intra(q, k, b, causal)), v)

                bt = b.T
                bl = bt[:, n - 1:n]
                kdt = k.T * jnp.exp(bl - bt)
                s = jnp.exp(bl) * s + _dot(_bf16(kdt), v)

                o_ref[rows, vs] = (_layer_norm(o, gam, bet) * sgg_ref[rows, vs]).astype(o_ref.dtype)
            s_ref[h] = s

    block_decay = jnp.max(-jnp.sum(lf_ref[...], axis=0, keepdims=True))
    small_decay = block_decay <= GLA_FACTORED_MAX_DECAY

    @pl.when(small_decay)
    def _():
        run(block, intra_factored)

    @pl.when(jnp.logical_not(small_decay))
    def _():
        run(c, intra_exact)


def _gla(q, k, lf, v, sgg, s0, gam, bet):
    B, T, _ = q.shape
    c = min(CHUNK, T)
    rows = c * GLA_CHUNKS_PER_STEP if T % (c * GLA_CHUNKS_PER_STEP) == 0 else c
    tokk = lambda b, i: (b, i, 0)
    st = lambda b, i: (b, 0, 0, 0)
    whole = lambda b, i: (0, 0)
    return pl.pallas_call(
        functools.partial(_gla_kernel, c=c),
        grid=(B, T // rows),
        in_specs=[pl.BlockSpec((None, rows, _GK), tokk), pl.BlockSpec((None, rows, _GK), tokk),
                  pl.BlockSpec((None, rows, _GK), tokk), pl.BlockSpec((None, rows, _GV), tokk),
                  pl.BlockSpec((None, rows, _GV), tokk),
                  pl.BlockSpec((None, GLA_HEADS, GLA_DK, GLA_DV), st),
                  pl.BlockSpec((1, GLA_DV), whole), pl.BlockSpec((1, GLA_DV), whole)],
        out_specs=[pl.BlockSpec((None, rows, _GV), tokk),
                   pl.BlockSpec((None, GLA_HEADS, GLA_DK, GLA_DV), st)],
        out_shape=[jax.ShapeDtypeStruct((B, T, _GV), MXU_DTYPE),
                   jax.ShapeDtypeStruct((B, GLA_HEADS, GLA_DK, GLA_DV), jnp.float32)],
        scratch_shapes=[pltpu.VMEM((c, c), jnp.float32)],
        compiler_params=pltpu.CompilerParams(dimension_semantics=("arbitrary", "arbitrary"),
                                             vmem_limit_bytes=VMEM_LIMIT),
        name="gla",
    )(q, k, lf, v, sgg, s0, gam, bet)


def _sortable_key(s):
    bits = pltpu.bitcast(s, jnp.int32)
    key = bits ^ ((bits >> 31) & 0x7FFFFFFF)
    return jnp.where(key == -1, 0, key)


def _dsa_kernel(qd_ref, qi_ref, wt_ref, k_ref, vt_ref, ki_ref, o_ref,
                sc_ref, hi_ref, lo_ref, bias_ref, m_ref, acc_ref, lg_ref, *, tq, tk, past, seq_len, topk):
    t0 = pl.program_id(1) * tq
    total = past + seq_len
    qpos = past + t0 + lax.broadcasted_iota(jnp.int32, (1, tq), 1)
    limit = jnp.minimum(((qpos >> 6) + 1) * CHUNK, total)
    lim_max = jnp.minimum(((past + t0 + tq - 1) // CHUNK + 1) * CHUNK, total)
    nk = (lim_max + tk - 1) // tk
    lp = sc_ref.shape[0] * tk
    key_iota = lax.broadcasted_iota(jnp.int32, (tk, tq), 0)

    qis = [_bf16(qi_ref[:, h * IDX_DIM:(h + 1) * IDX_DIM]) for h in range(IDX_HEADS)]
    neg_inf = jnp.float32(-jnp.inf)

    def score_tile(kt, carry):
        kit = ki_ref[pl.ds(pl.multiple_of(kt * tk, tk), tk), :]
        s = None
        for h in range(IDX_HEADS):
            term = wt_ref[h:h + 1, :] * jnp.maximum(_dot_nt(kit, qis[h]), 0.0)
            s = term if s is None else s + term
        s = jnp.where(key_iota < limit - kt * tk, s, neg_inf)
        sc_ref[kt] = s
        key = _sortable_key(s)
        hi_ref[kt] = (key >> 16).astype(jnp.int16)
        lo_ref[kt] = ((key & 0xFFFF) - _HALF).astype(jnp.int16)
        return carry

    lax.fori_loop(0, nk, score_tile, 0)

    def count_ge(ref, cand_off):
        cb = jnp.broadcast_to((cand_off - _HALF).astype(jnp.int16), (_CNT_ROWS, tq))

        def body(kt, acc):
            tile = ref[kt]
            for j in range(tk // _CNT_ROWS):
                acc = acc + jnp.where(tile[j * _CNT_ROWS:(j + 1) * _CNT_ROWS] >= cb, jnp.int16(1), jnp.int16(0))
            return acc

        acc = lax.fori_loop(0, nk, body, jnp.zeros((_CNT_ROWS, tq), jnp.int16))
        return jnp.sum(acc.astype(jnp.int32), axis=0, keepdims=True)

    def count_above(ref, off):
        return jnp.where(off >= 2 * _HALF - 1, 0, count_ge(ref, jnp.minimum(off + 1, 2 * _HALF - 1)))

    def radix_search(ref, target, settled):
        def cond(state):
            i, _, _, open_lanes = state
            return (i < 16) & (open_lanes > 0)

        def bits(state):
            i, off, done, _ = state
            for j in range(_BITS_PER_CHECK):
                cand = off | jnp.left_shift(jnp.int32(1), 15 - (i + j))
                cnt = count_ge(ref, cand)
                off = jnp.where((done > 0) | (cnt < target), off, cand)
                done = jnp.where(cnt == target, 1, done)
            return i + _BITS_PER_CHECK, off, done, 1 - jnp.min(done)

        _, off, done, open_lanes = lax.while_loop(
            cond, bits, (jnp.int32(0), jnp.zeros((1, tq), jnp.int32), settled, 1 - jnp.min(settled)))
        return off, done, open_lanes

    settled0 = jnp.where(limit <= topk, 1, 0)
    hi_off, settled1, open1 = radix_search(hi_ref, topk, settled0)
    hi16 = (hi_off - _HALF).astype(jnp.int16)

    n_above_hi = lax.cond(open1 > 0, lambda: count_above(hi_ref, hi_off), lambda: jnp.zeros((1, tq), jnp.int32))

    @pl.when(open1 > 0)
    def _():
        def band_tile(kt, carry):
            lo_ref[kt] = jnp.where(hi_ref[kt] == hi16, lo_ref[kt], jnp.int16(-_HALF))
            return carry

        lax.fori_loop(0, nk, band_tile, 0)

    need_lo = topk - n_above_hi
    lo_off, settled2, open2 = lax.cond(
        open1 > 0, lambda: radix_search(lo_ref, need_lo, settled1),
        lambda: (jnp.zeros((1, tq), jnp.int32), settled1, jnp.int32(0)))

    del settled2, open2
    thr_key = (hi_off - _HALF) * (2 * _HALF) + jnp.where(settled1 > 0, 0, lo_off)

    def key_to_score(key):
        t = pltpu.bitcast(key ^ ((key >> 31) & 0x7FFFFFFF), jnp.float32)
        return jnp.where(t != t, neg_inf, t)

    def count_scores(t):
        tb = jnp.broadcast_to(t, (_SCORE_CNT_ROWS, tq))

        def body(kt, accs):
            gt, ge = accs
            tile = sc_ref[kt]
            for j in range(tk // _SCORE_CNT_ROWS):
                rows = tile[j * _SCORE_CNT_ROWS:(j + 1) * _SCORE_CNT_ROWS]
                gt = gt + jnp.where(rows > tb, 1, 0)
                ge = ge + jnp.where(rows >= tb, 1, 0)
            return gt, ge

        zero = jnp.zeros((_SCORE_CNT_ROWS, tq), jnp.int32)
        gt, ge = lax.fori_loop(0, nk, body, (zero, zero))
        return jnp.sum(gt, axis=0, keepdims=True), jnp.sum(ge, axis=0, keepdims=True)

    def is_kth(n_gt, n_ge):
        return (settled0 > 0) | ((n_gt <= topk) & (n_ge >= topk))

    def search_on_scores():
        int_min = jnp.int32(-2 ** 31)

        def bit(i, key):
            cand = key ^ jnp.left_shift(jnp.int32(1), 31 - i)
            _, n_ge = count_scores(key_to_score(cand))
            return jnp.where(n_ge >= topk, cand, key)

        t = key_to_score(lax.fori_loop(0, 32, bit, jnp.full((1, tq), int_min, jnp.int32)))
        return (t,) + count_scores(t)

    thr = jnp.where(settled0 > 0, neg_inf, key_to_score(thr_key))
    n_gt, n_ge = count_scores(thr)
    confirmed = jnp.min(jnp.where(is_kth(n_gt, n_ge), 1, 0))
    thr, n_gt, n_ge = lax.cond(confirmed > 0, lambda: (thr, n_gt, n_ge), search_on_scores)
    thr = jnp.where(settled0 > 0, neg_inf, thr)

    tied = (n_ge > topk) & (settled0 == 0)
    need = jnp.where(tied, topk - n_gt, 2 * lp).astype(jnp.float32)
    any_tied = jnp.max(jnp.where(tied, 1, 0))

    @pl.when(any_tied == 0)
    def _():
        def select_tile(kt, carry):
            sel = (sc_ref[kt] >= thr) & (key_iota < limit - kt * tk)
            bias_ref[kt] = jnp.where(sel, 0.0, NEG_BIG)
            return carry

        lax.fori_loop(0, nk, select_tile, 0)

    @pl.when(any_tied > 0)
    def _():
        krow = lax.broadcasted_iota(jnp.int32, (tk, tk), 0)
        kcol = lax.broadcasted_iota(jnp.int32, (tk, tk), 1)
        prefix_ones = _bf16(jnp.where(krow >= kcol, 1.0, 0.0))

        def select_tile(kt, seen):
            s = sc_ref[kt]
            eq = s == thr
            rank = _dot(prefix_ones, _bf16(jnp.where(eq, 1.0, 0.0))) + seen
            sel = ((s > thr) | (eq & (rank <= need))) & (key_iota < limit - kt * tk)
            bias_ref[kt] = jnp.where(sel, 0.0, NEG_BIG)
            return rank[tk - 1:tk]

        lax.fori_loop(0, nk, select_tile, jnp.zeros((1, tq), jnp.float32))

    rep = DSA_HEADS // DSA_KV_HEADS
    m_ref[...] = jnp.full(m_ref.shape, NEG_BIG, jnp.float32)
    acc_ref[...] = jnp.zeros(acc_ref.shape, jnp.float32)
    qs = [_bf16(qd_ref[:, h * DSA_HEAD_DIM:(h + 1) * DSA_HEAD_DIM]) for h in range(DSA_HEADS)]

    def attend_tile(kt, carry):
        rows = pl.ds(pl.multiple_of(kt * tk, tk), tk)
        bias = bias_ref[kt]

        def logits(h):
            g = h // rep
            return _dot_nt(k_ref[rows, g * DSA_HEAD_DIM:(g + 1) * DSA_HEAD_DIM], qs[h])

        for h in range(_LG_BUFS - 1):
            lg_ref[h] = logits(h)
        for h in range(DSA_HEADS):
            if h + _LG_BUFS - 1 < DSA_HEADS:
                lg_ref[(h + _LG_BUFS - 1) % _LG_BUFS] = logits(h + _LG_BUFS - 1)
            lb = lg_ref[h % _LG_BUFS] + bias
            m_old = m_ref[h]
            m_new = jnp.maximum(m_old, jnp.max(lb, axis=0, keepdims=True))
            p = jnp.exp2(lb - m_new)
            a = jnp.exp2(m_old - m_new)
            vtg = vt_ref[kt, h // rep]
            acc_ref[h] = a * acc_ref[h] + _dot(vtg, _bf16(p))
            m_ref[h] = m_new
        return carry

    lax.fori_loop(0, nk, attend_tile, 0)
    for h in range(DSA_HEADS):
        acc = acc_ref[h]
        o_t = acc[:DSA_HEAD_DIM] / acc[DSA_HEAD_DIM:DSA_HEAD_DIM + 1]
        o_ref[:, h * DSA_HEAD_DIM:(h + 1) * DSA_HEAD_DIM] = o_t.T.astype(o_ref.dtype)


DSA_KEY_TILE = 512
_HALF = 1 << 15
_BITS_PER_CHECK = 4
_LG_BUFS = 4
_SCORE_CNT_ROWS = 16
_CNT_ROWS = 64
_VT_ROWS = DSA_HEAD_DIM + 16


def _dsa(qd, qi, wt, k_all, vt, ki_all, past, seq_len, tq, tk):
    B, tp, _ = qd.shape
    lp = k_all.shape[1]
    nkt = lp // tk
    topk = min(TOPK_MAX, (past + seq_len) // 4)
    tokk = lambda b, i: (b, i, 0)
    kv = lambda b, i: (b, 0, 0)
    return pl.pallas_call(
        functools.partial(_dsa_kernel, tq=tq, tk=tk, past=past, seq_len=seq_len, topk=topk),
        grid=(B, tp // tq),
        in_specs=[pl.BlockSpec((None, tq, _DQ), tokk), pl.BlockSpec((None, tq, _IQ), tokk),
                  pl.BlockSpec((None, IDX_HEADS, tq), lambda b, i: (b, 0, i)),
                  pl.BlockSpec((None, lp, _DKV), kv),
                  pl.BlockSpec((None, nkt, DSA_KV_HEADS, _VT_ROWS, tk), lambda b, i: (b, 0, 0, 0, 0)),
                  pl.BlockSpec((None, lp, IDX_DIM), kv)],
        out_specs=pl.BlockSpec((None, tq, _DQ), tokk),
        out_shape=jax.ShapeDtypeStruct((B, tp, _DQ), MXU_DTYPE),
        scratch_shapes=[pltpu.VMEM((nkt, tk, tq), jnp.float32),
                        pltpu.VMEM((nkt, tk, tq), jnp.int16), pltpu.VMEM((nkt, tk, tq), jnp.int16),
                        pltpu.VMEM((nkt, tk, tq), jnp.float32),
                        pltpu.VMEM((DSA_HEADS, 1, tq), jnp.float32),
                        pltpu.VMEM((DSA_HEADS, _VT_ROWS, tq), jnp.float32),
                        pltpu.VMEM((_LG_BUFS, tk, tq), jnp.float32)],
        compiler_params=pltpu.CompilerParams(dimension_semantics=("arbitrary", "arbitrary"),
                                             vmem_limit_bytes=VMEM_LIMIT),
        name="dsa",
    )(qd, qi, wt, k_all, vt, ki_all)


def _merge_kernel(og_ref, od_ref, sga_ref, sgb_ref, x_ref, wg_ref, wd_ref, wo_ref, g_ref, b_ref, h_ref):
    m = sga_ref[...] * _dot(_bf16(og_ref[...]), wg_ref[...]) + sgb_ref[...] * _dot(_bf16(od_ref[...]), wd_ref[...])
    mix = _dot(_bf16(m), wo_ref[...])
    h_ref[...] = _layer_norm(ALPHA * x_ref[...] + mix, g_ref[...], b_ref[...])


def _merge(og, od, sga, sgb, x2d, wg, wd, wo, g, b, tm):
    n = x2d.shape[0]
    tok = lambda i: (i, 0)
    whole = lambda i: (0, 0)
    act = pl.BlockSpec((tm, D_MODEL), tok)
    wsp = pl.BlockSpec((D_MODEL, D_MODEL), whole)
    vec = pl.BlockSpec((1, D_MODEL), whole)
    return pl.pallas_call(
        _merge_kernel,
        grid=(n // tm,),
        in_specs=[act, act, act, act, act, wsp, wsp, wsp, vec, vec],
        out_specs=act,
        out_shape=jax.ShapeDtypeStruct((n, D_MODEL), jnp.float32),
        compiler_params=pltpu.CompilerParams(dimension_semantics=("arbitrary",), vmem_limit_bytes=VMEM_LIMIT),
        name="merge",
    )(og, od, sga, sgb, x2d, wg, wd, wo, g, b)


def _first_lane_where(mask, lane):
    return jnp.min(jnp.where(mask, lane, LANES), axis=-1, keepdims=True)


def _route(logits):
    lane = lax.broadcasted_iota(jnp.int32, logits.shape, 1)
    is_g = lane < N_GROUPS
    gl = jnp.where(is_g, logits, NEG_BIG)
    gmax = jnp.max(gl, axis=-1, keepdims=True)
    g_sel = _first_lane_where(is_g & (gl == gmax), lane)
    g_prob = 1.0 / jnp.sum(jnp.where(is_g, jnp.exp(gl - gmax), 0.0), axis=-1, keepdims=True)
    e_lane = lane - N_GROUPS
    in_grp = (e_lane >= 0) & (e_lane < N_EXPERTS) & ((e_lane >> 2) == g_sel)
    el = jnp.where(in_grp, logits, NEG_BIG)
    v1 = jnp.max(el, axis=-1, keepdims=True)
    i1 = _first_lane_where(in_grp & (el == v1), lane)
    rest = in_grp & (lane != i1)
    el2 = jnp.where(rest, logits, NEG_BIG)
    v2 = jnp.max(el2, axis=-1, keepdims=True)
    i2 = _first_lane_where(rest & (el2 == v2), lane)
    e2 = jnp.exp(v2 - v1)
    w1 = g_prob / (1.0 + e2)
    w2 = g_prob * e2 / (1.0 + e2)
    return jnp.where(lane == i1, w1, 0.0) + jnp.where(lane == i2, w2, 0.0)


def _moe_kernel(h_ref, wr_ref, br_ref, wg_ref, wu_ref, wd_ref, g_ref, b_ref, y_ref, gate_ref, acc_ref):
    e = pl.program_id(1)
    h = h_ref[...]

    hb = _bf16(h)

    @pl.when(e == 0)
    def _():
        h_lo = _bf16(h - hb.astype(jnp.float32))
        logits = _dot(hb, wr_ref[0]) + (_dot(h_lo, wr_ref[0]) + _dot(hb, wr_ref[1])) + br_ref[...]
        gate_ref[...] = _route(logits)
        acc_ref[...] = jnp.zeros(acc_ref.shape, jnp.float32)

    lane = lax.broadcasted_iota(jnp.int32, gate_ref.shape, 1)
    gcol = jnp.sum(jnp.where(lane == e + N_GROUPS, gate_ref[...], 0.0), axis=-1, keepdims=True)
    a = _dot(hb, wg_ref[...])
    hid = a * _sigmoid(a) * _dot(hb, wu_ref[...]) * gcol
    acc_ref[...] += _dot(_bf16(hid), wd_ref[...])

    @pl.when(e == N_EXPERTS - 1)
    def _():
        y_ref[...] = _layer_norm(ALPHA * h + acc_ref[...], g_ref[...], b_ref[...])


def _moe(h2d, wr, br, wg, wu, wd, g, b, tm):
    n = h2d.shape[0]
    tok = lambda i, e: (i, 0)
    whole = lambda i, e: (0, 0)
    per_e = lambda i, e: (e, 0, 0)
    return pl.pallas_call(
        _moe_kernel,
        grid=(n // tm, N_EXPERTS),
        in_specs=[pl.BlockSpec((tm, D_MODEL), tok),
                  pl.BlockSpec((2, D_MODEL, LANES), lambda i, e: (0, 0, 0)), pl.BlockSpec((1, LANES), whole),
                  pl.BlockSpec((None, D_MODEL, D_EXPERT), per_e),
                  pl.BlockSpec((None, D_MODEL, D_EXPERT), per_e),
                  pl.BlockSpec((None, D_EXPERT, D_MODEL), per_e),
                  pl.BlockSpec((1, D_MODEL), whole), pl.BlockSpec((1, D_MODEL), whole)],
        out_specs=pl.BlockSpec((tm, D_MODEL), tok),
        out_shape=jax.ShapeDtypeStruct((n, D_MODEL), jnp.float32),
        scratch_shapes=[pltpu.VMEM((tm, LANES), jnp.float32), pltpu.VMEM((tm, D_MODEL), jnp.float32)],
        compiler_params=pltpu.CompilerParams(dimension_semantics=("arbitrary", "arbitrary"),
                                             vmem_limit_bytes=VMEM_LIMIT),
        name="moe",
    )(h2d, wr, br, wg, wu, wd, g, b)


def _prep_weights(w_in, w_up, b_up, gla_g, gla_b, w_bg, w_bd, w_o, ln1_g, ln1_b, w_rg, b_rg, w_re, b_re,
                  w_eg, w_eu, w_ed, ln2_g, ln2_b):
    w, wup, bup = _prep_w_in(w_in, w_up, b_up)
    w_re2 = jnp.transpose(w_re, (1, 0, 2)).reshape(D_MODEL, N_EXPERTS)
    rpad = jnp.zeros((D_MODEL, LANES - N_GROUPS - N_EXPERTS), w_rg.dtype)
    wr = jnp.concatenate([w_rg, w_re2, rpad], axis=1)
    wr_hi = _bf16(wr)
    wr = jnp.stack([wr_hi, _bf16(wr - wr_hi.astype(jnp.float32))])
    br =jnp.concatenate([b_rg, b_re.reshape(N_EXPERTS), jnp.zeros((LANES - N_GROUPS - N_EXPERTS,), b_rg.dtype)])
    row = lambda a: a.reshape(1, -1)
    return dict(w=w, wup=wup, bup=bup, gla_g=row(gla_g), gla_b=row(gla_b), wbg=_bf16(w_bg), wbd=_bf16(w_bd),
                wo=_bf16(w_o), ln1_g=row(ln1_g), ln1_b=row(ln1_b), wr=wr, br=row(br), weg=_bf16(w_eg),
                weu=_bf16(w_eu), wed=_bf16(w_ed), ln2_g=row(ln2_g), ln2_b=row(ln2_b))


MOE_TOKEN_TILE = 1024


def _token_tile(n):
    for tm in (512, 256, 128):
        if n % tm == 0:
            return tm
    raise ValueError(f"token count {n} is not a multiple of 128")


def _trunk_layer(x, s0, past_k, past_v, past_ki, p):
    B, T, _ = x.shape
    past = past_k.shape[1]
    n = B * T
    tm = _token_tile(n)
    assert T % tm == 0 or tm % T == 0
    x2d = x.reshape(n, D_MODEL)
    gq, gk, gv, sgg, lf, dq, dk, dv, iq, sga, sgb, misc = _in_proj(x2d, p["w"], p["wup"], p["bup"], T, past, tm)
    seq = lambda a: a.reshape(B, T, a.shape[-1])

    o_gla, s_fin = _gla(seq(gq), seq(gk), seq(lf), seq(gv), seq(sgg), s0, p["gla_g"], p["gla_b"])

    kv_rows = lambda a: a.reshape(B, T, _DKV)
    k_new, v_new, ki_new = kv_rows(dk), kv_rows(dv), seq(misc)[:, :, :IDX_DIM]
    total = past + T
    tk = DSA_KEY_TILE
    lp = -(-total // tk) * tk

    def keys(past_rows, new_rows):
        rows = jnp.concatenate([past_rows.reshape(B, past, new_rows.shape[-1]), new_rows], axis=1)
        return _bf16(jnp.pad(rows, ((0, 0), (0, lp - total), (0, 0))))

    nkt = lp // tk
    vt = jnp.transpose(keys(past_v, v_new).reshape(B, nkt, tk, _DKV), (0, 1, 3, 2))
    vt = vt.reshape(B, nkt, DSA_KV_HEADS, DSA_HEAD_DIM, tk)
    ones = jnp.ones((B, nkt, DSA_KV_HEADS, 1, tk), jnp.bfloat16)
    zpad = jnp.zeros((B, nkt, DSA_KV_HEADS, _VT_ROWS - DSA_HEAD_DIM - 1, tk), jnp.bfloat16)
    vt = jnp.concatenate([vt, ones, zpad], axis=3)

    tq = 256 if T % 256 == 0 else LANES
    tp = -(-T // tq) * tq
    qpad = lambda a: jnp.pad(a, ((0, 0), (0, tp - T), (0, 0)))
    wt = jnp.transpose(qpad(seq(misc)[:, :, _MISC_IW:_MISC_IW + IDX_HEADS]), (0, 2, 1))
    o_dsa = _dsa(qpad(seq(dq)), qpad(seq(iq)), wt, keys(past_k, k_new), vt, keys(past_ki, ki_new),
                 past, T, tq, tk)[:, :T]

    h = _merge(o_gla.reshape(n, _GV), o_dsa.reshape(n, _DQ), sga, sgb, x2d, p["wbg"], p["wbd"], p["wo"],
               p["ln1_g"], p["ln1_b"], tm)
    tm_moe = MOE_TOKEN_TILE if n % MOE_TOKEN_TILE == 0 else tm
    y = _moe(h, p["wr"], p["br"], p["weg"], p["weu"], p["wed"], p["ln2_g"], p["ln2_b"], tm_moe)
    cache_rows = lambda a: a.reshape(B, T, DSA_KV_HEADS, DSA_HEAD_DIM)
    return y.reshape(B, T, D_MODEL), s_fin, cache_rows(dk), cache_rows(dv), ki_new


def kernel(x_prompt, x_sample, state_gla, cache_k, cache_v, cache_k_idx, w_in, w_gla_gate_up, b_gla_gate,
           gla_norm_g, gla_norm_b, w_branch_gla, w_branch_dsa, w_out, ln1_g, ln1_b, w_router_group,
           b_router_group, w_router_expert, b_router_expert, w_expert_gate, w_expert_up, w_expert_down,
           ln2_g, ln2_b):
    B = x_prompt.shape[0]
    l = 0
    p = _prep_weights(w_in[l], w_gla_gate_up[l], b_gla_gate[l], gla_norm_g[l], gla_norm_b[l], w_branch_gla[l],
                      w_branch_dsa[l], w_out[l], ln1_g[l], ln1_b[l], w_router_group[l], b_router_group[l],
                      w_router_expert[l], b_router_expert[l], w_expert_gate[l], w_expert_up[l],
                      w_expert_down[l], ln2_g[l], ln2_b[l])
    s0 = jnp.zeros((B, GLA_HEADS, GLA_DK, GLA_DV), jnp.float32)
    ek = jnp.zeros((B, 0, DSA_KV_HEADS, DSA_HEAD_DIM), x_prompt.dtype)
    eki = jnp.zeros((B, 0, IDX_DIM), x_prompt.dtype)
    yp, sp, kp, vp, kip = _trunk_layer(x_prompt, s0, ek, ek, eki, p)
    ys, ss, ksn, vsn, kisn = _trunk_layer(x_sample, state_gla[l], cache_k[l], cache_v[l], cache_k_idx[l], p)
    stack = lambda a: a[None]
    return (yp, ys, stack(sp), stack(kp), stack(vp), stack(kip), stack(ss), stack(ksn), stack(vsn), stack(kisn))
```

```python
import functools
import math

import jax
import jax.numpy as jnp
from jax import lax
from jax.experimental import pallas as pl
from jax.experimental.pallas import tpu as pltpu

D_MODEL = 1024
CHUNK = 64
ROPE_THETA = 10000.0
LN_EPS = 1e-5
GLA_HEADS = 4
GLA_DK = D_MODEL // 2 // GLA_HEADS
GLA_DV = D_MODEL // GLA_HEADS
GLA_GATE_RANK = 16
GLA_TAU = 16.0
DSA_HEADS = 8
DSA_KV_HEADS = 2
DSA_HEAD_DIM = D_MODEL // DSA_HEADS
IDX_HEADS = 8
IDX_DIM = 64
TOPK_MAX = 256
IDX_W_SCALE = (IDX_HEADS ** -0.5) * (IDX_DIM ** -0.5)
N_GROUPS = 4
EXPERTS_PER_GROUP = 4
N_EXPERTS = N_GROUPS * EXPERTS_PER_GROUP
D_EXPERT = 256
TOP_K_IN_GROUP = 2
DEPTH = 1
ALPHA = (2 * DEPTH) ** 0.25

LANES = 128
GLA_SUB = 16
GLA_CHUNKS_PER_STEP = 4
GLA_FACTORED_MAX_DECAY = 60.0
VMEM_LIMIT = 56 * 1024 * 1024
NEG_BIG = -1e30
MXU_DTYPE = jnp.bfloat16
DSA_Q_DTYPE = MXU_DTYPE
DSA_Q_SCALE = (DSA_HEAD_DIM ** -0.5) * math.log2(math.e)

_GK = GLA_HEADS * GLA_DK
_GV = GLA_HEADS * GLA_DV
_DQ = DSA_HEADS * DSA_HEAD_DIM
_DKV = DSA_KV_HEADS * DSA_HEAD_DIM
_IQ = IDX_HEADS * IDX_DIM
_C_GQ = 0
_C_GK = _C_GQ + _GK
_C_GV = _C_GK + _GK
_C_GG = _C_GV + _GV
_C_DQ = _C_GG + _GV
_C_DK = _C_DQ + _DQ
_C_DV = _C_DK + _DKV
_C_IQ = _C_DV + _DKV
_C_MISC = _C_IQ + _IQ
_C_GA = _C_MISC + LANES
_C_GB = _C_GA + D_MODEL
_C_END = _C_GB + D_MODEL
_MISC_GR = IDX_DIM
_MISC_IW = IDX_DIM + GLA_GATE_RANK


def _bf16(a):
    return a.astype(jnp.bfloat16)


def _dot(a, b):
    return jnp.dot(a, b, preferred_element_type=jnp.float32)


def _dot_nt(a, b):
    return lax.dot_general(a, b, (((1,), (1,)), ((), ())), preferred_element_type=jnp.float32)


def _sigmoid(x):
    return 1.0 / (1.0 + jnp.exp(-x))


def _layer_norm(x, g, b):
    mu = jnp.mean(x, axis=-1, keepdims=True)
    xc = x - mu
    var = jnp.mean(xc * xc, axis=-1, keepdims=True)
    return xc * lax.rsqrt(var + LN_EPS) * g + b


def _rope_full(z, cos, sin):
    return z * cos + pltpu.roll(z, DSA_HEAD_DIM // 2, axis=1) * sin


def _rope_pair(z, cos, sin, first_half):
    q = IDX_DIM // 2
    partner = jnp.where(first_half, pltpu.roll(z, LANES - q, axis=1), pltpu.roll(z, q, axis=1))
    return z * cos + partner * sin


def _in_proj_kernel(x_ref, w_ref, wup_ref, bup_ref, cos_ref, sin_ref, cosi_ref, sini_ref,
                    gq_ref, gk_ref, gv_ref, sgg_ref, lf_ref, dq_ref, dk_ref, dv_ref, iq_ref,
                    sga_ref, sgb_ref, misc_ref):
    xb = _bf16(x_ref[...])

    def proj(c0, width):
        return _dot(xb, w_ref[:, c0:c0 + width])

    gq_ref[...] = proj(_C_GQ, _GK) * (GLA_DK ** -0.5)
    gk_ref[...] = proj(_C_GK, _GK)
    gv_ref[...] = proj(_C_GV, _GV).astype(gv_ref.dtype)
    gg = proj(_C_GG, _GV)
    sgg_ref[...] = gg * _sigmoid(gg)
    sga_ref[...] = _sigmoid(proj(_C_GA, D_MODEL))
    sgb_ref[...] = _sigmoid(proj(_C_GB, D_MODEL))

    cos = cos_ref[...]
    sin = sin_ref[...]
    zq = proj(_C_DQ, _DQ)
    for h in range(DSA_HEADS):
        hs = slice(h * DSA_HEAD_DIM, (h + 1) * DSA_HEAD_DIM)
        dq_ref[:, hs] = (_rope_full(zq[:, hs], cos, sin) * DSA_Q_SCALE).astype(dq_ref.dtype)
    zkv = proj(_C_DK, 2 * _DKV)
    for h in range(DSA_KV_HEADS):
        hs = slice(h * DSA_HEAD_DIM, (h + 1) * DSA_HEAD_DIM)
        dk_ref[:, h, :] = _rope_full(zkv[:, hs], cos, sin)
        dv_ref[:, h, :] = zkv[:, _DKV + h * DSA_HEAD_DIM:_DKV + (h + 1) * DSA_HEAD_DIM]

    cosi = cosi_ref[...]
    sini = sini_ref[...]
    lane = lax.broadcasted_iota(jnp.int32, cosi.shape, 1)
    first_half = (lane & (IDX_DIM - 1)) < (IDX_DIM // 2)
    zi = proj(_C_IQ, _IQ + LANES)
    for p in range(_IQ // LANES):
        ps = slice(p * LANES, (p + 1) * LANES)
        iq_ref[:, ps] = _rope_pair(zi[:, ps], cosi, sini, first_half).astype(iq_ref.dtype)

    misc = zi[:, _IQ:]
    pre = _dot(_bf16(misc), wup_ref[...]) + bup_ref[...]
    lf_ref[...] = (jnp.minimum(pre, 0.0) - jnp.log(1.0 + jnp.exp(-jnp.abs(pre)))) * (1.0 / GLA_TAU)
    roped = _rope_pair(misc, cosi, sini, first_half)
    misc_ref[...] = jnp.where(lane < IDX_DIM, roped, misc * IDX_W_SCALE)


def _rope_tables(T, past, rows):
    pos = (past + jnp.arange(T, dtype=jnp.int32)).astype(jnp.float32)

    def tables(dim):
        half = dim // 2
        inv = ROPE_THETA ** (-jnp.arange(half, dtype=jnp.float32) / half)
        ang = pos[:, None] * inv[None, :]
        c, s = jnp.cos(ang), jnp.sin(ang)
        reps = LANES // dim
        cos = jnp.tile(jnp.concatenate([c, c], axis=1), (1, reps))
        sin = jnp.tile(jnp.concatenate([-s, s], axis=1), (1, reps))
        if rows > T:
            cos = jnp.tile(cos, (rows // T, 1))
            sin = jnp.tile(sin, (rows // T, 1))
        return cos, sin

    return tables(DSA_HEAD_DIM) + tables(IDX_DIM)


def _prep_w_in(w_in, w_up, b_up):
    sizes = (_GK, _GK, _GV, _GV, GLA_GATE_RANK, _DQ, _DKV, _DKV, _IQ, IDX_DIM, IDX_HEADS, D_MODEL, D_MODEL)
    offs = [0]
    for s in sizes:
        offs.append(offs[-1] + s)
    gq, gk, gv, gg, gr, dq, dk, dv, iq, ik, iw, ga, gb = (w_in[:, offs[i]:offs[i + 1]] for i in range(len(sizes)))
    pad = jnp.zeros((D_MODEL, LANES - IDX_DIM - GLA_GATE_RANK - IDX_HEADS), w_in.dtype)
    w = jnp.concatenate([gq, gk, gv, gg, dq, dk, dv, iq, ik, gr, iw, pad, ga, gb], axis=1)
    wup = jnp.zeros((LANES, _GK), w_up.dtype).at[_MISC_GR:_MISC_GR + GLA_GATE_RANK].set(w_up)
    return _bf16(w), _bf16(wup), b_up.reshape(1, _GK)


def _in_proj(x2d, w, wup, bup, T, past, tm):
    n = x2d.shape[0]
    rows = max(T, tm)
    cos, sin, cosi, sini = _rope_tables(T, past, rows)
    nt = rows // tm
    tok = lambda i: (i, 0)
    tab = lambda i: (i % nt, 0)
    whole = lambda i: (0, 0)
    widths = (_GK, _GK, _GV, _GV, _GK, _DQ, _DKV, _DKV, _IQ, D_MODEL, D_MODEL, LANES)
    f32 = jnp.float32
    dtypes = (f32, f32, MXU_DTYPE, f32, f32, DSA_Q_DTYPE, f32, f32, DSA_Q_DTYPE, f32, f32, f32)
    kv_outs = (6, 7)
    kv_shape = (n, DSA_KV_HEADS, DSA_HEAD_DIM)
    kv_spec = pl.BlockSpec((tm, DSA_KV_HEADS, DSA_HEAD_DIM), lambda i: (i, 0, 0))
    return pl.pallas_call(
        _in_proj_kernel,
        grid=(n // tm,),
        in_specs=[pl.BlockSpec((tm, D_MODEL), tok),
                  pl.BlockSpec((D_MODEL, _C_END), whole),
                  pl.BlockSpec((LANES, _GK), whole),
                  pl.BlockSpec((1, _GK), whole),
                  pl.BlockSpec((tm, LANES), tab), pl.BlockSpec((tm, LANES), tab),
                  pl.BlockSpec((tm, LANES), tab), pl.BlockSpec((tm, LANES), tab)],
        out_specs=[kv_spec if i in kv_outs else pl.BlockSpec((tm, wd), tok) for i, wd in enumerate(widths)],
        out_shape=[jax.ShapeDtypeStruct(kv_shape if i in kv_outs else (n, wd), dt)
                   for i, (wd, dt) in enumerate(zip(widths, dtypes))],
        compiler_params=pltpu.CompilerParams(dimension_semantics=("arbitrary",), vmem_limit_bytes=VMEM_LIMIT),
        name="in_proj",
    )(x2d, w, wup, bup, cos, sin, cosi, sini)


def _split3(a):
    hi = _bf16(a)
    r1 = a - hi.astype(jnp.float32)
    mid = _bf16(r1)
    lo = _bf16(r1 - mid.astype(jnp.float32))
    return hi, mid, lo


def _gla_kernel(q_ref, k_ref, lf_ref, v_ref, sgg_ref, s0_ref, g_ref, b_ref, o_ref, s_ref, a_ref, *, c):
    ci = pl.program_id(1)

    @pl.when(ci == 0)
    def _():
        s_ref[...] = s0_ref[...]

    block = q_ref.shape[0]
    gam = g_ref[...]
    bet = b_ref[...]

    def causal_mask(n):
        return lax.broadcasted_iota(jnp.int32, (n, n), 0) >= lax.broadcasted_iota(jnp.int32, (n, n), 1)

    def log_decay(lf, causal):
        tri = _bf16(jnp.where(causal, 1.0, 0.0))
        hi, mid, lo = _split3(lf)
        return _dot(tri, hi) + _dot(tri, mid) + _dot(tri, lo)

    def intra_factored(q, k, b, causal):
        b0 = b[0:1]
        a = _dot_nt(_bf16(q * jnp.exp(b - b0)), _bf16(k * jnp.exp(b0 - b)))
        return jnp.where(causal, a, 0.0)

    def intra_exact(q, k, b, causal):
        nsub = q.shape[0] // GLA_SUB
        srow = lax.broadcasted_iota(jnp.int32, (GLA_SUB, GLA_SUB), 0)
        scol = lax.broadcasted_iota(jnp.int32, (GLA_SUB, GLA_SUB), 1)
        for bi in range(nsub):
            r0 = bi * GLA_SUB
            rs = slice(r0, r0 + GLA_SUB)
            qi = q[rs]
            bq = b[rs]
            ki = k[rs]
            d = jnp.zeros((GLA_SUB, GLA_SUB), jnp.float32)
            for j in range(GLA_SUB):
                f = jnp.exp(jnp.minimum(bq - bq[j:j + 1], 0.0)) * qi * ki[j:j + 1]
                d = jnp.where(scol == j, jnp.sum(f, axis=-1, keepdims=True), d)
            a_ref[rs, r0:r0 + GLA_SUB] = jnp.where(srow >= scol, d, 0.0)
            if bi > 0:
                ref_b = b[r0:r0 + 1]
                qt = qi * jnp.exp(bq - ref_b)
                kt = k[:r0] * jnp.exp(ref_b - b[:r0])
                a_ref[rs, :r0] = _dot_nt(_bf16(qt), _bf16(kt))
            if bi < nsub - 1:
                a_ref[rs, r0 + GLA_SUB:] = jnp.zeros((GLA_SUB, c - r0 - GLA_SUB), jnp.float32)
        return a_ref[...]

    def run(n, intra):
        causal = causal_mask(n)
        chunks = [slice(i * n, (i + 1) * n) for i in range(block // n)]
        bs = [log_decay(lf_ref[rows], causal) for rows in chunks]
        for h in range(GLA_HEADS):
            ks = slice(h * GLA_DK, (h + 1) * GLA_DK)
            vs = slice(h * GLA_DV, (h + 1) * GLA_DV)
            s = s_ref[h]
            for rows, b_all in zip(chunks, bs):
                q = q_ref[rows, ks]
                k = k_ref[rows, ks]
                v = _bf16(v_ref[rows, vs])
                b = b_all[:, ks]
                o = _dot(_bf16(q * jnp.exp(b)), _bf16(s)) + _dot(_bf16(intra(q, k, b, causal)), v)

                bt = b.T
                bl = bt[:, n - 1:n]
                kdt = k.T * jnp.exp(bl - bt)
                s = jnp.exp(bl) * s + _dot(_bf16(kdt), v)

                o_ref[rows, vs] = (_layer_norm(o, gam, bet) * sgg_ref[rows, vs]).astype(o_ref.dtype)
            s_ref[h] = s

    block_decay = jnp.max(-jnp.sum(lf_ref[...], axis=0, keepdims=True))
    small_decay = block_decay <= GLA_FACTORED_MAX_DECAY

    @pl.when(small_decay)
    def _():
        run(block, intra_factored)

    @pl.when(jnp.logical_not(small_decay))
    def _():
        run(c, intra_exact)


def _gla(q, k, lf, v, sgg, s0, gam, bet):
    B, T, _ = q.shape
    c = min(CHUNK, T)
    rows = c * GLA_CHUNKS_PER_STEP if T % (c * GLA_CHUNKS_PER_STEP) == 0 else c
    tokk = lambda b, i: (b, i, 0)
    st = lambda b, i: (b, 0, 0, 0)
    whole = lambda b, i: (0, 0)
    return pl.pallas_call(
        functools.partial(_gla_kernel, c=c),
        grid=(B, T // rows),
        in_specs=[pl.BlockSpec((None, rows, _GK), tokk), pl.BlockSpec((None, rows, _GK), tokk),
                  pl.BlockSpec((None, rows, _GK), tokk), pl.BlockSpec((None, rows, _GV), tokk),
                  pl.BlockSpec((None, rows, _GV), tokk),
                  pl.BlockSpec((None, GLA_HEADS, GLA_DK, GLA_DV), st),
                  pl.BlockSpec((1, GLA_DV), whole), pl.BlockSpec((1, GLA_DV), whole)],
        out_specs=[pl.BlockSpec((None, rows, _GV), tokk),
                   pl.BlockSpec((None, GLA_HEADS, GLA_DK, GLA_DV), st)],
        out_shape=[jax.ShapeDtypeStruct((B, T, _GV), MXU_DTYPE),
                   jax.ShapeDtypeStruct((B, GLA_HEADS, GLA_DK, GLA_DV), jnp.float32)],
        scratch_shapes=[pltpu.VMEM((c, c), jnp.float32)],
        compiler_params=pltpu.CompilerParams(dimension_semantics=("arbitrary", "arbitrary"),
                                             vmem_limit_bytes=VMEM_LIMIT),
        name="gla",
    )(q, k, lf, v, sgg, s0, gam, bet)


def _sortable_key(s):
    bits = pltpu.bitcast(s, jnp.int32)
    key = bits ^ ((bits >> 31) & 0x7FFFFFFF)
    return jnp.where(key == -1, 0, key)


def _dsa_kernel(qd_ref, qi_ref, wt_ref, k_ref, vt_ref, ki_ref, o_ref,
                sc_ref, hi_ref, lo_ref, bias_ref, m_ref, acc_ref, lg_ref, *, tq, tk, past, seq_len, topk):
    t0 = pl.program_id(1) * tq
    total = past + seq_len
    qpos = past + t0 + lax.broadcasted_iota(jnp.int32, (1, tq), 1)
    limit = jnp.minimum(((qpos >> 6) + 1) * CHUNK, total)
    lim_max = jnp.minimum(((past + t0 + tq - 1) // CHUNK + 1) * CHUNK, total)
    nk = (lim_max + tk - 1) // tk
    lp = sc_ref.shape[0] * tk
    key_iota = lax.broadcasted_iota(jnp.int32, (tk, tq), 0)

    qis = [_bf16(qi_ref[:, h * IDX_DIM:(h + 1) * IDX_DIM]) for h in range(IDX_HEADS)]
    neg_inf = jnp.float32(-jnp.inf)

    def score_tile(kt, carry):
        kit = ki_ref[pl.ds(pl.multiple_of(kt * tk, tk), tk), :]
        s = None
        for h in range(IDX_HEADS):
            term = wt_ref[h:h + 1, :] * jnp.maximum(_dot_nt(kit, qis[h]), 0.0)
            s = term if s is None else s + term
        s = jnp.where(key_iota < limit - kt * tk, s, neg_inf)
        sc_ref[kt] = s
        key = _sortable_key(s)
        hi_ref[kt] = (key >> 16).astype(jnp.int16)
        lo_ref[kt] = ((key & 0xFFFF) - _HALF).astype(jnp.int16)
        return carry

    lax.fori_loop(0, nk, score_tile, 0)

    def count_ge(ref, cand_off):
        cb = jnp.broadcast_to((cand_off - _HALF).astype(jnp.int16), (_CNT_ROWS, tq))

        def body(kt, acc):
            tile = ref[kt]
            for j in range(tk // _CNT_ROWS):
                acc = acc + jnp.where(tile[j * _CNT_ROWS:(j + 1) * _CNT_ROWS] >= cb, jnp.int16(1), jnp.int16(0))
            return acc

        acc = lax.fori_loop(0, nk, body, jnp.zeros((_CNT_ROWS, tq), jnp.int16))
        return jnp.sum(acc.astype(jnp.int32), axis=0, keepdims=True)

    def count_above(ref, off):
        return jnp.where(off >= 2 * _HALF - 1, 0, count_ge(ref, jnp.minimum(off + 1, 2 * _HALF - 1)))

    def radix_search(ref, target, settled):
        def cond(state):
            i, _, _, open_lanes = state
            return (i < 16) & (open_lanes > 0)

        def bits(state):
            i, off, done, _ = state
            for j in range(_BITS_PER_CHECK):
                cand = off | jnp.left_shift(jnp.int32(1), 15 - (i + j))
                cnt = count_ge(ref, cand)
                off = jnp.where((done > 0) | (cnt < target), off, cand)
                done = jnp.where(cnt == target, 1, done)
            return i + _BITS_PER_CHECK, off, done, 1 - jnp.min(done)

        _, off, done, open_lanes = lax.while_loop(
            cond, bits, (jnp.int32(0), jnp.zeros((1, tq), jnp.int32), settled, 1 - jnp.min(settled)))
        return off, done, open_lanes

    settled0 = jnp.where((limit <= topk) | (qpos >= total), 1, 0)
    hi_off, settled1, open1 = radix_search(hi_ref, topk, settled0)
    hi16 = (hi_off - _HALF).astype(jnp.int16)

    n_above_hi = lax.cond(open1 > 0, lambda: count_above(hi_ref, hi_off), lambda: jnp.zeros((1, tq), jnp.int32))

    @pl.when(open1 > 0)
    def _():
        def band_tile(kt, carry):
            lo_ref[kt] = jnp.where(hi_ref[kt] == hi16, lo_ref[kt], jnp.int16(-_HALF))
            return carry

        lax.fori_loop(0, nk, band_tile, 0)

    need_lo = topk - n_above_hi
    lo_off, settled2, open2 = lax.cond(
        open1 > 0, lambda: radix_search(lo_ref, need_lo, settled1),
        lambda: (jnp.zeros((1, tq), jnp.int32), settled1, jnp.int32(0)))

    del settled2, open2
    thr_key = (hi_off - _HALF) * (2 * _HALF) + jnp.where(settled1 > 0, 0, lo_off)

    def key_to_score(key):
        t = pltpu.bitcast(key ^ ((key >> 31) & 0x7FFFFFFF), jnp.float32)
        return jnp.where(t != t, neg_inf, t)

    def count_scores(t):
        tb = jnp.broadcast_to(t, (_SCORE_CNT_ROWS, tq))

        def body(kt, accs):
            gt, ge = accs
            tile = sc_ref[kt]
            for j in range(tk // _SCORE_CNT_ROWS):
                rows = tile[j * _SCORE_CNT_ROWS:(j + 1) * _SCORE_CNT_ROWS]
                gt = gt + jnp.where(rows > tb, 1, 0)
                ge = ge + jnp.where(rows >= tb, 1, 0)
            return gt, ge

        zero = jnp.zeros((_SCORE_CNT_ROWS, tq), jnp.int32)
        gt, ge = lax.fori_loop(0, nk, body, (zero, zero))
        return jnp.sum(gt, axis=0, keepdims=True), jnp.sum(ge, axis=0, keepdims=True)

    def is_kth(n_gt, n_ge):
        return (settled0 > 0) | ((n_gt <= topk) & (n_ge >= topk))

    def search_on_scores():
        int_min = jnp.int32(-2 ** 31)

        def bit(i, key):
            cand = key ^ jnp.left_shift(jnp.int32(1), 31 - i)
            _, n_ge = count_scores(key_to_score(cand))
            return jnp.where(n_ge >= topk, cand, key)

        t = key_to_score(lax.fori_loop(0, 32, bit, jnp.full((1, tq), int_min, jnp.int32)))
        return (t,) + count_scores(t)

    thr = jnp.where(settled0 > 0, neg_inf, key_to_score(thr_key))
    n_gt, n_ge = count_scores(thr)
    confirmed = jnp.min(jnp.where(is_kth(n_gt, n_ge), 1, 0))
    thr, n_gt, n_ge = lax.cond(confirmed > 0, lambda: (thr, n_gt, n_ge), search_on_scores)
    thr = jnp.where(settled0 > 0, neg_inf, thr)

    tied = (n_ge > topk) & (settled0 == 0)
    need = jnp.where(tied, topk - n_gt, 2 * lp).astype(jnp.float32)
    any_tied = jnp.max(jnp.where(tied, 1, 0))

    @pl.when(any_tied == 0)
    def _():
        def select_tile(kt, carry):
            sel = (sc_ref[kt] >= thr) & (key_iota < limit - kt * tk)
            bias_ref[kt] = jnp.where(sel, 0.0, NEG_BIG)
            return carry

        lax.fori_loop(0, nk, select_tile, 0)

    @pl.when(any_tied > 0)
    def _():
        krow = lax.broadcasted_iota(jnp.int32, (tk, tk), 0)
        kcol = lax.broadcasted_iota(jnp.int32, (tk, tk), 1)
        prefix_ones = _bf16(jnp.where(krow >= kcol, 1.0, 0.0))

        def select_tile(kt, seen):
            s = sc_ref[kt]
            eq = s == thr
            rank = _dot(prefix_ones, _bf16(jnp.where(eq, 1.0, 0.0))) + seen
            sel = ((s > thr) | (eq & (rank <= need))) & (key_iota < limit - kt * tk)
            bias_ref[kt] = jnp.where(sel, 0.0, NEG_BIG)
            return rank[tk - 1:tk]

        lax.fori_loop(0, nk, select_tile, jnp.zeros((1, tq), jnp.float32))

    rep = DSA_HEADS // DSA_KV_HEADS
    m_ref[...] = jnp.full(m_ref.shape, NEG_BIG, jnp.float32)
    acc_ref[...] = jnp.zeros(acc_ref.shape, jnp.float32)
    qs = [_bf16(qd_ref[:, h * DSA_HEAD_DIM:(h + 1) * DSA_HEAD_DIM]) for h in range(DSA_HEADS)]

    def attend_tile(kt, carry):
        rows = pl.ds(pl.multiple_of(kt * tk, tk), tk)
        bias = bias_ref[kt]

        def logits(h):
            g = h // rep
            return _dot_nt(k_ref[rows, g * DSA_HEAD_DIM:(g + 1) * DSA_HEAD_DIM], qs[h])

        for h in range(_LG_BUFS - 1):
            lg_ref[h] = logits(h)
        for h in range(DSA_HEADS):
            if h + _LG_BUFS - 1 < DSA_HEADS:
                lg_ref[(h + _LG_BUFS - 1) % _LG_BUFS] = logits(h + _LG_BUFS - 1)
            lb = lg_ref[h % _LG_BUFS] + bias
            m_old = m_ref[h]
            m_new = jnp.maximum(m_old, jnp.max(lb, axis=0, keepdims=True))
            p = jnp.exp2(lb - m_new)
            a = jnp.exp2(m_old - m_new)
            vtg = vt_ref[kt, h // rep]
            acc_ref[h] = a * acc_ref[h] + _dot(vtg, _bf16(p))
            m_ref[h] = m_new
        return carry

    lax.fori_loop(0, nk, attend_tile, 0)
    for h in range(DSA_HEADS):
        acc = acc_ref[h]
        o_t = acc[:DSA_HEAD_DIM] / acc[DSA_HEAD_DIM:DSA_HEAD_DIM + 1]
        o_ref[:, h * DSA_HEAD_DIM:(h + 1) * DSA_HEAD_DIM] = o_t.T.astype(o_ref.dtype)


DSA_KEY_TILE = 512
_HALF = 1 << 15
_BITS_PER_CHECK = 4
_LG_BUFS = 4
_SCORE_CNT_ROWS = 16
_CNT_ROWS = 64
_VT_ROWS = DSA_HEAD_DIM + 16


def _dsa(qd, qi, wt, k_all, vt, ki_all, past, seq_len, tq, tk):
    B, tp, _ = qd.shape
    lp = k_all.shape[1]
    nkt = lp // tk
    topk = min(TOPK_MAX, (past + seq_len) // 4)
    tokk = lambda b, i: (b, i, 0)
    kv = lambda b, i: (b, 0, 0)
    return pl.pallas_call(
        functools.partial(_dsa_kernel, tq=tq, tk=tk, past=past, seq_len=seq_len, topk=topk),
        grid=(B, tp // tq),
        in_specs=[pl.BlockSpec((None, tq, _DQ), tokk), pl.BlockSpec((None, tq, _IQ), tokk),
                  pl.BlockSpec((None, IDX_HEADS, tq), lambda b, i: (b, 0, i)),
                  pl.BlockSpec((None, lp, _DKV), kv),
                  pl.BlockSpec((None, nkt, DSA_KV_HEADS, _VT_ROWS, tk), lambda b, i: (b, 0, 0, 0, 0)),
                  pl.BlockSpec((None, lp, IDX_DIM), kv)],
        out_specs=pl.BlockSpec((None, tq, _DQ), tokk),
        out_shape=jax.ShapeDtypeStruct((B, tp, _DQ), MXU_DTYPE),
        scratch_shapes=[pltpu.VMEM((nkt, tk, tq), jnp.float32),
                        pltpu.VMEM((nkt, tk, tq), jnp.int16), pltpu.VMEM((nkt, tk, tq), jnp.int16),
                        pltpu.VMEM((nkt, tk, tq), jnp.float32),
                        pltpu.VMEM((DSA_HEADS, 1, tq), jnp.float32),
                        pltpu.VMEM((DSA_HEADS, _VT_ROWS, tq), jnp.float32),
                        pltpu.VMEM((_LG_BUFS, tk, tq), jnp.float32)],
        compiler_params=pltpu.CompilerParams(dimension_semantics=("arbitrary", "arbitrary"),
                                             vmem_limit_bytes=VMEM_LIMIT),
        name="dsa",
    )(qd, qi, wt, k_all, vt, ki_all)


def _merge_kernel(og_ref, od_ref, sga_ref, sgb_ref, x_ref, wg_ref, wd_ref, wo_ref, g_ref, b_ref, h_ref):
    m = sga_ref[...] * _dot(_bf16(og_ref[...]), wg_ref[...]) + sgb_ref[...] * _dot(_bf16(od_ref[...]), wd_ref[...])
    mix = _dot(_bf16(m), wo_ref[...])
    h_ref[...] = _layer_norm(ALPHA * x_ref[...] + mix, g_ref[...], b_ref[...])


def _merge(og, od, sga, sgb, x2d, wg, wd, wo, g, b, tm):
    n = x2d.shape[0]
    tok = lambda i: (i, 0)
    whole = lambda i: (0, 0)
    act = pl.BlockSpec((tm, D_MODEL), tok)
    wsp = pl.BlockSpec((D_MODEL, D_MODEL), whole)
    vec = pl.BlockSpec((1, D_MODEL), whole)
    return pl.pallas_call(
        _merge_kernel,
        grid=(n // tm,),
        in_specs=[act, act, act, act, act, wsp, wsp, wsp, vec, vec],
        out_specs=act,
        out_shape=jax.ShapeDtypeStruct((n, D_MODEL), jnp.float32),
        compiler_params=pltpu.CompilerParams(dimension_semantics=("arbitrary",), vmem_limit_bytes=VMEM_LIMIT),
        name="merge",
    )(og, od, sga, sgb, x2d, wg, wd, wo, g, b)


def _first_lane_where(mask, lane):
    return jnp.min(jnp.where(mask, lane, LANES), axis=-1, keepdims=True)


def _route(logits):
    lane = lax.broadcasted_iota(jnp.int32, logits.shape, 1)
    is_g = lane < N_GROUPS
    gl = jnp.where(is_g, logits, NEG_BIG)
    gmax = jnp.max(gl, axis=-1, keepdims=True)
    g_sel = _first_lane_where(is_g & (gl == gmax), lane)
    g_prob = 1.0 / jnp.sum(jnp.where(is_g, jnp.exp(gl - gmax), 0.0), axis=-1, keepdims=True)
    e_lane = lane - N_GROUPS
    in_grp = (e_lane >= 0) & (e_lane < N_EXPERTS) & ((e_lane >> 2) == g_sel)
    el = jnp.where(in_grp, logits, NEG_BIG)
    v1 = jnp.max(el, axis=-1, keepdims=True)
    i1 = _first_lane_where(in_grp & (el == v1), lane)
    rest = in_grp & (lane != i1)
    el2 = jnp.where(rest, logits, NEG_BIG)
    v2 = jnp.max(el2, axis=-1, keepdims=True)
    i2 = _first_lane_where(rest & (el2 == v2), lane)
    e2 = jnp.exp(v2 - v1)
    w1 = g_prob / (1.0 + e2)
    w2 = g_prob * e2 / (1.0 + e2)
    return jnp.where(lane == i1, w1, 0.0) + jnp.where(lane == i2, w2, 0.0)


def _moe_kernel(h_ref, wr_ref, br_ref, wg_ref, wu_ref, wd_ref, g_ref, b_ref, y_ref, gate_ref, acc_ref, hb_ref):
    e = pl.program_id(1)

    @pl.when(e == 0)
    def _():
        h = h_ref[...]
        hb = _bf16(h)
        hb_ref[...] = hb
        h_lo = _bf16(h - hb.astype(jnp.float32))
        logits = _dot(hb, wr_ref[0]) + (_dot(h_lo, wr_ref[0]) + _dot(hb, wr_ref[1])) + br_ref[...]
        gate_ref[...] = _route(logits)
        acc_ref[...] = jnp.zeros(acc_ref.shape, jnp.float32)

    hb = hb_ref[...]
    lane = lax.broadcasted_iota(jnp.int32, gate_ref.shape, 1)
    gcol = jnp.sum(jnp.where(lane == e + N_GROUPS, gate_ref[...], 0.0), axis=-1, keepdims=True)
    a = _dot(hb, wg_ref[...])
    hid = a * _sigmoid(a) * _dot(hb, wu_ref[...]) * gcol
    acc_ref[...] += _dot(_bf16(hid), wd_ref[...])

    @pl.when(e == N_EXPERTS - 1)
    def _():
        y_ref[...] = _layer_norm(ALPHA * h_ref[...] + acc_ref[...], g_ref[...], b_ref[...])


def _moe(h2d, wr, br, wg, wu, wd, g, b, tm):
    n = h2d.shape[0]
    tok = lambda i, e: (i, 0)
    whole = lambda i, e: (0, 0)
    per_e = lambda i, e: (e, 0, 0)
    return pl.pallas_call(
        _moe_kernel,
        grid=(n // tm, N_EXPERTS),
        in_specs=[pl.BlockSpec((tm, D_MODEL), tok),
                  pl.BlockSpec((2, D_MODEL, LANES), lambda i, e: (0, 0, 0)), pl.BlockSpec((1, LANES), whole),
                  pl.BlockSpec((None, D_MODEL, D_EXPERT), per_e),
                  pl.BlockSpec((None, D_MODEL, D_EXPERT), per_e),
                  pl.BlockSpec((None, D_EXPERT, D_MODEL), per_e),
                  pl.BlockSpec((1, D_MODEL), whole), pl.BlockSpec((1, D_MODEL), whole)],
        out_specs=pl.BlockSpec((tm, D_MODEL), tok),
        out_shape=jax.ShapeDtypeStruct((n, D_MODEL), jnp.float32),
        scratch_shapes=[pltpu.VMEM((tm, LANES), jnp.float32), pltpu.VMEM((tm, D_MODEL), jnp.float32),
                        pltpu.VMEM((tm, D_MODEL), MXU_DTYPE)],
        compiler_params=pltpu.CompilerParams(dimension_semantics=("arbitrary", "arbitrary"),
                                             vmem_limit_bytes=VMEM_LIMIT),
        name="moe",
    )(h2d, wr, br, wg, wu, wd, g, b)


def _prep_weights(w_in, w_up, b_up, gla_g, gla_b, w_bg, w_bd, w_o, ln1_g, ln1_b, w_rg, b_rg, w_re, b_re,
                  w_eg, w_eu, w_ed, ln2_g, ln2_b):
    w, wup, bup = _prep_w_in(w_in, w_up, b_up)
    w_re2 = jnp.transpose(w_re, (1, 0, 2)).reshape(D_MODEL, N_EXPERTS)
    rpad = jnp.zeros((D_MODEL, LANES - N_GROUPS - N_EXPERTS), w_rg.dtype)
    wr = jnp.concatenate([w_rg, w_re2, rpad], axis=1)
    wr_hi = _bf16(wr)
    wr = jnp.stack([wr_hi, _bf16(wr - wr_hi.astype(jnp.float32))])
    br =jnp.concatenate([b_rg, b_re.reshape(N_EXPERTS), jnp.zeros((LANES - N_GROUPS - N_EXPERTS,), b_rg.dtype)])
    row = lambda a: a.reshape(1, -1)
    return dict(w=w, wup=wup, bup=bup, gla_g=row(gla_g), gla_b=row(gla_b), wbg=_bf16(w_bg), wbd=_bf16(w_bd),
                wo=_bf16(w_o), ln1_g=row(ln1_g), ln1_b=row(ln1_b), wr=wr, br=row(br), weg=_bf16(w_eg),
                weu=_bf16(w_eu), wed=_bf16(w_ed), ln2_g=row(ln2_g), ln2_b=row(ln2_b))


MOE_TOKEN_TILE = 1024


def _token_tile(n):
    for tm in (512, 256, 128):
        if n % tm == 0:
            return tm
    raise ValueError(f"token count {n} is not a multiple of 128")


def _trunk_layer(x, s0, past_k, past_v, past_ki, p):
    B, T, _ = x.shape
    past = past_k.shape[1]
    n = B * T
    tm = _token_tile(n)
    assert T % tm == 0 or tm % T == 0
    x2d = x.reshape(n, D_MODEL)
    gq, gk, gv, sgg, lf, dq, dk, dv, iq, sga, sgb, misc = _in_proj(x2d, p["w"], p["wup"], p["bup"], T, past, tm)
    seq = lambda a: a.reshape(B, T, a.shape[-1])

    o_gla, s_fin = _gla(seq(gq), seq(gk), seq(lf), seq(gv), seq(sgg), s0, p["gla_g"], p["gla_b"])

    kv_rows = lambda a: a.reshape(B, T, _DKV)
    k_new, v_new, ki_new = kv_rows(dk), kv_rows(dv), seq(misc)[:, :, :IDX_DIM]
    total = past + T
    tk = DSA_KEY_TILE
    lp = -(-total // tk) * tk

    def keys(past_rows, new_rows):
        rows = jnp.concatenate([past_rows.reshape(B, past, new_rows.shape[-1]), new_rows], axis=1)
        return _bf16(jnp.pad(rows, ((0, 0), (0, lp - total), (0, 0))))

    nkt = lp // tk
    vt = jnp.transpose(keys(past_v, v_new).reshape(B, nkt, tk, _DKV), (0, 1, 3, 2))
    vt = vt.reshape(B, nkt, DSA_KV_HEADS, DSA_HEAD_DIM, tk)
    ones = jnp.ones((B, nkt, DSA_KV_HEADS, 1, tk), jnp.bfloat16)
    zpad = jnp.zeros((B, nkt, DSA_KV_HEADS, _VT_ROWS - DSA_HEAD_DIM - 1, tk), jnp.bfloat16)
    vt = jnp.concatenate([vt, ones, zpad], axis=3)

    tq = 256 if T % 256 == 0 else LANES
    tp = -(-T // tq) * tq
    qpad = lambda a: jnp.pad(a, ((0, 0), (0, tp - T), (0, 0)))
    wt = jnp.transpose(qpad(seq(misc)[:, :, _MISC_IW:_MISC_IW + IDX_HEADS]), (0, 2, 1))
    o_dsa = _dsa(qpad(seq(dq)), qpad(seq(iq)), wt, keys(past_k, k_new), vt, keys(past_ki, ki_new),
                 past, T, tq, tk)[:, :T]

    h = _merge(o_gla.reshape(n, _GV), o_dsa.reshape(n, _DQ), sga, sgb, x2d, p["wbg"], p["wbd"], p["wo"],
               p["ln1_g"], p["ln1_b"], tm)
    tm_moe = MOE_TOKEN_TILE if n % MOE_TOKEN_TILE == 0 else tm
    y = _moe(h, p["wr"], p["br"], p["weg"], p["weu"], p["wed"], p["ln2_g"], p["ln2_b"], tm_moe)
    cache_rows = lambda a: a.reshape(B, T, DSA_KV_HEADS, DSA_HEAD_DIM)
    return y.reshape(B, T, D_MODEL), s_fin, cache_rows(dk), cache_rows(dv), ki_new


def kernel(x_prompt, x_sample, state_gla, cache_k, cache_v, cache_k_idx, w_in, w_gla_gate_up, b_gla_gate,
           gla_norm_g, gla_norm_b, w_branch_gla, w_branch_dsa, w_out, ln1_g, ln1_b, w_router_group,
           b_router_group, w_router_expert, b_router_expert, w_expert_gate, w_expert_up, w_expert_down,
           ln2_g, ln2_b):
    B = x_prompt.shape[0]
    l = 0
    p = _prep_weights(w_in[l], w_gla_gate_up[l], b_gla_gate[l], gla_norm_g[l], gla_norm_b[l], w_branch_gla[l],
                      w_branch_dsa[l], w_out[l], ln1_g[l], ln1_b[l], w_router_group[l], b_router_group[l],
                      w_router_expert[l], b_router_expert[l], w_expert_gate[l], w_expert_up[l],
                      w_expert_down[l], ln2_g[l], ln2_b[l])
    s0 = jnp.zeros((B, GLA_HEADS, GLA_DK, GLA_DV), jnp.float32)
    ek = jnp.zeros((B, 0, DSA_KV_HEADS, DSA_HEAD_DIM), x_prompt.dtype)
    eki = jnp.zeros((B, 0, IDX_DIM), x_prompt.dtype)
    yp, sp, kp, vp, kip = _trunk_layer(x_prompt, s0, ek, ek, eki, p)
    ys, ss, ksn, vsn, kisn = _trunk_layer(x_sample, state_gla[l], cache_k[l], cache_v[l], cache_k_idx[l], p)
    stack = lambda a: a[None]
    return (yp, ys, stack(sp), stack(kp), stack(vp), stack(kip), stack(ss), stack(ksn), stack(vsn), stack(kisn))
```

```python
import functools
import math

import jax
import jax.numpy as jnp
from jax import lax
from jax.experimental import pallas as pl
from jax.experimental.pallas import tpu as pltpu

D_MODEL = 1024
CHUNK = 64
ROPE_THETA = 10000.0
LN_EPS = 1e-5
GLA_HEADS = 4
GLA_DK = D_MODEL // 2 // GLA_HEADS
GLA_DV = D_MODEL // GLA_HEADS
GLA_GATE_RANK = 16
GLA_TAU = 16.0
DSA_HEADS = 8
DSA_KV_HEADS = 2
DSA_HEAD_DIM = D_MODEL // DSA_HEADS
IDX_HEADS = 8
IDX_DIM = 64
TOPK_MAX = 256
IDX_W_SCALE = (IDX_HEADS ** -0.5) * (IDX_DIM ** -0.5)
N_GROUPS = 4
EXPERTS_PER_GROUP = 4
N_EXPERTS = N_GROUPS * EXPERTS_PER_GROUP
D_EXPERT = 256
TOP_K_IN_GROUP = 2
DEPTH = 1
ALPHA = (2 * DEPTH) ** 0.25

LANES = 128
GLA_SUB = 16
GLA_CHUNKS_PER_STEP = 4
GLA_FACTORED_MAX_DECAY = 60.0
VMEM_LIMIT = 56 * 1024 * 1024
NEG_BIG = -1e30
MXU_DTYPE = jnp.bfloat16
DSA_Q_DTYPE = MXU_DTYPE
DSA_Q_SCALE = (DSA_HEAD_DIM ** -0.5) * math.log2(math.e)

_GK = GLA_HEADS * GLA_DK
_GV = GLA_HEADS * GLA_DV
_DQ = DSA_HEADS * DSA_HEAD_DIM
_DKV = DSA_KV_HEADS * DSA_HEAD_DIM
_IQ = IDX_HEADS * IDX_DIM
_C_GQ = 0
_C_GK = _C_GQ + _GK
_C_GV = _C_GK + _GK
_C_GG = _C_GV + _GV
_C_DQ = _C_GG + _GV
_C_DK = _C_DQ + _DQ
_C_DV = _C_DK + _DKV
_C_IQ = _C_DV + _DKV
_C_MISC = _C_IQ + _IQ
_C_GA = _C_MISC + LANES
_C_GB = _C_GA + D_MODEL
_C_END = _C_GB + D_MODEL
_MISC_GR = IDX_DIM
_MISC_IW = IDX_DIM + GLA_GATE_RANK


def _bf16(a):
    return a.astype(jnp.bfloat16)


def _dot(a, b):
    return jnp.dot(a, b, preferred_element_type=jnp.float32)


def _dot_nt(a, b):
    return lax.dot_general(a, b, (((1,), (1,)), ((), ())), preferred_element_type=jnp.float32)


def _sigmoid(x):
    return 1.0 / (1.0 + jnp.exp(-x))


def _layer_norm(x, g, b):
    mu = jnp.mean(x, axis=-1, keepdims=True)
    xc = x - mu
    var = jnp.mean(xc * xc, axis=-1, keepdims=True)
    return xc * lax.rsqrt(var + LN_EPS) * g + b


def _rope_full(z, cos, sin):
    return z * cos + pltpu.roll(z, DSA_HEAD_DIM // 2, axis=1) * sin


def _rope_pair(z, cos, sin, first_half):
    q = IDX_DIM // 2
    partner = jnp.where(first_half, pltpu.roll(z, LANES - q, axis=1), pltpu.roll(z, q, axis=1))
    return z * cos + partner * sin


def _in_proj_kernel(x_ref, w_ref, wup_ref, bup_ref, cos_ref, sin_ref, cosi_ref, sini_ref,
                    gq_ref, gk_ref, gv_ref, sgg_ref, lf_ref, dq_ref, dk_ref, dv_ref, iq_ref,
                    sga_ref, sgb_ref, ki_ref, kb_ref, vb_ref, kib_ref, wt_ref):
    xb = _bf16(x_ref[...])

    def proj(c0, width):
        return _dot(xb, w_ref[:, c0:c0 + width])

    gq_ref[...] = proj(_C_GQ, _GK) * (GLA_DK ** -0.5)
    gk_ref[...] = proj(_C_GK, _GK)
    gv_ref[...] = proj(_C_GV, _GV).astype(gv_ref.dtype)
    gg = proj(_C_GG, _GV)
    sgg_ref[...] = gg * _sigmoid(gg)
    sga_ref[...] = _sigmoid(proj(_C_GA, D_MODEL))
    sgb_ref[...] = _sigmoid(proj(_C_GB, D_MODEL))

    cos = cos_ref[...]
    sin = sin_ref[...]
    zq = proj(_C_DQ, _DQ)
    for h in range(DSA_HEADS):
        hs = slice(h * DSA_HEAD_DIM, (h + 1) * DSA_HEAD_DIM)
        dq_ref[:, hs] = (_rope_full(zq[:, hs], cos, sin) * DSA_Q_SCALE).astype(dq_ref.dtype)
    zkv = proj(_C_DK, 2 * _DKV)
    for h in range(DSA_KV_HEADS):
        hs = slice(h * DSA_HEAD_DIM, (h + 1) * DSA_HEAD_DIM)
        kh = _rope_full(zkv[:, hs], cos, sin)
        dk_ref[:, h, :] = kh
        kb_ref[:, hs] = kh.astype(kb_ref.dtype)
        dv_ref[:, h, :] = zkv[:, _DKV + h * DSA_HEAD_DIM:_DKV + (h + 1) * DSA_HEAD_DIM]
    vb_ref[...] = zkv[:, _DKV:].astype(vb_ref.dtype)

    cosi = cosi_ref[...]
    sini = sini_ref[...]
    lane = lax.broadcasted_iota(jnp.int32, cosi.shape, 1)
    first_half = (lane & (IDX_DIM - 1)) < (IDX_DIM // 2)
    zi = proj(_C_IQ, _IQ + LANES)
    for p in range(_IQ // LANES):
        ps = slice(p * LANES, (p + 1) * LANES)
        iq_ref[:, ps] = _rope_pair(zi[:, ps], cosi, sini, first_half).astype(iq_ref.dtype)

    misc = zi[:, _IQ:]
    pre = _dot(_bf16(misc), wup_ref[...]) + bup_ref[...]
    lf_ref[...] = (jnp.minimum(pre, 0.0) - jnp.log(1.0 + jnp.exp(-jnp.abs(pre)))) * (1.0 / GLA_TAU)
    ki = _rope_pair(misc, cosi, sini, first_half)[:, :IDX_DIM]
    ki_ref[...] = ki
    kib_ref[...] = ki.astype(kib_ref.dtype)
    wt_ref[...] = (misc * IDX_W_SCALE).T[_MISC_IW:_MISC_IW + IDX_HEADS]


def _rope_tables(T, past, rows):
    pos = (past + jnp.arange(T, dtype=jnp.int32)).astype(jnp.float32)

    def tables(dim):
        half = dim // 2
        inv = ROPE_THETA ** (-jnp.arange(half, dtype=jnp.float32) / half)
        ang = pos[:, None] * inv[None, :]
        c, s = jnp.cos(ang), jnp.sin(ang)
        reps = LANES // dim
        cos = jnp.tile(jnp.concatenate([c, c], axis=1), (1, reps))
        sin = jnp.tile(jnp.concatenate([-s, s], axis=1), (1, reps))
        if rows > T:
            cos = jnp.tile(cos, (rows // T, 1))
            sin = jnp.tile(sin, (rows // T, 1))
        return cos, sin

    return tables(DSA_HEAD_DIM) + tables(IDX_DIM)


def _prep_w_in(w_in, w_up, b_up):
    sizes = (_GK, _GK, _GV, _GV, GLA_GATE_RANK, _DQ, _DKV, _DKV, _IQ, IDX_DIM, IDX_HEADS, D_MODEL, D_MODEL)
    offs = [0]
    for s in sizes:
        offs.append(offs[-1] + s)
    gq, gk, gv, gg, gr, dq, dk, dv, iq, ik, iw, ga, gb = (w_in[:, offs[i]:offs[i + 1]] for i in range(len(sizes)))
    pad = jnp.zeros((D_MODEL, LANES - IDX_DIM - GLA_GATE_RANK - IDX_HEADS), w_in.dtype)
    w = jnp.concatenate([gq, gk, gv, gg, dq, dk, dv, iq, ik, gr, iw, pad, ga, gb], axis=1)
    wup = jnp.zeros((LANES, _GK), w_up.dtype).at[_MISC_GR:_MISC_GR + GLA_GATE_RANK].set(w_up)
    return _bf16(w), _bf16(wup), b_up.reshape(1, _GK)


def _in_proj(x2d, w, wup, bup, T, past, tm):
    n = x2d.shape[0]
    rows = max(T, tm)
    cos, sin, cosi, sini = _rope_tables(T, past, rows)
    nt = rows // tm
    tok = lambda i: (i, 0)
    tab = lambda i: (i % nt, 0)
    whole = lambda i: (0, 0)
    f32 = jnp.float32
    rows_of = lambda wd, dt: (jax.ShapeDtypeStruct((n, wd), dt), pl.BlockSpec((tm, wd), tok))
    kv_rows = (jax.ShapeDtypeStruct((n, DSA_KV_HEADS, DSA_HEAD_DIM), f32),
               pl.BlockSpec((tm, DSA_KV_HEADS, DSA_HEAD_DIM), lambda i: (i, 0, 0)))
    outs = [rows_of(_GK, f32), rows_of(_GK, f32), rows_of(_GV, MXU_DTYPE), rows_of(_GV, f32), rows_of(_GK, f32),
            rows_of(_DQ, DSA_Q_DTYPE), kv_rows, kv_rows, rows_of(_IQ, DSA_Q_DTYPE),
            rows_of(D_MODEL, f32), rows_of(D_MODEL, f32),
            rows_of(IDX_DIM, f32), rows_of(_DKV, MXU_DTYPE), rows_of(_DKV, MXU_DTYPE), rows_of(IDX_DIM, MXU_DTYPE),
            (jax.ShapeDtypeStruct((IDX_HEADS, n), f32), pl.BlockSpec((IDX_HEADS, tm), lambda i: (0, i)))]
    return pl.pallas_call(
        _in_proj_kernel,
        grid=(n // tm,),
        in_specs=[pl.BlockSpec((tm, D_MODEL), tok),
                  pl.BlockSpec((D_MODEL, _C_END), whole),
                  pl.BlockSpec((LANES, _GK), whole),
                  pl.BlockSpec((1, _GK), whole),
                  pl.BlockSpec((tm, LANES), tab), pl.BlockSpec((tm, LANES), tab),
                  pl.BlockSpec((tm, LANES), tab), pl.BlockSpec((tm, LANES), tab)],
        out_specs=[spec for _, spec in outs],
        out_shape=[shape for shape, _ in outs],
        compiler_params=pltpu.CompilerParams(dimension_semantics=("arbitrary",), vmem_limit_bytes=VMEM_LIMIT),
        name="in_proj",
    )(x2d, w, wup, bup, cos, sin, cosi, sini)


def _split3(a):
    hi = _bf16(a)
    r1 = a - hi.astype(jnp.float32)
    mid = _bf16(r1)
    lo = _bf16(r1 - mid.astype(jnp.float32))
    return hi, mid, lo


def _gla_kernel(q_ref, k_ref, lf_ref, v_ref, sgg_ref, s0_ref, g_ref, b_ref, o_ref, s_ref, a_ref, *, c):
    ci = pl.program_id(1)

    @pl.when(ci == 0)
    def _():
        s_ref[...] = s0_ref[...]

    block = q_ref.shape[0]
    gam = g_ref[...]
    bet = b_ref[...]

    def causal_mask(n):
        return lax.broadcasted_iota(jnp.int32, (n, n), 0) >= lax.broadcasted_iota(jnp.int32, (n, n), 1)

    def log_decay(lf, causal):
        tri = _bf16(jnp.where(causal, 1.0, 0.0))
        hi, mid, lo = _split3(lf)
        return _dot(tri, hi) + _dot(tri, mid) + _dot(tri, lo)

    def intra_factored(q, k, b, causal):
        b0 = b[0:1]
        a = _dot_nt(_bf16(q * jnp.exp(b - b0)), _bf16(k * jnp.exp(b0 - b)))
        return jnp.where(causal, a, 0.0)

    def intra_exact(q, k, b, causal):
        nsub = q.shape[0] // GLA_SUB
        srow = lax.broadcasted_iota(jnp.int32, (GLA_SUB, GLA_SUB), 0)
        scol = lax.broadcasted_iota(jnp.int32, (GLA_SUB, GLA_SUB), 1)
        for bi in range(nsub):
            r0 = bi * GLA_SUB
            rs = slice(r0, r0 + GLA_SUB)
            qi = q[rs]
            bq = b[rs]
            ki = k[rs]
            d = jnp.zeros((GLA_SUB, GLA_SUB), jnp.float32)
            for j in range(GLA_SUB):
                f = jnp.exp(jnp.minimum(bq - bq[j:j + 1], 0.0)) * qi * ki[j:j + 1]
                d = jnp.where(scol == j, jnp.sum(f, axis=-1, keepdims=True), d)
            a_ref[rs, r0:r0 + GLA_SUB] = jnp.where(srow >= scol, d, 0.0)
            if bi > 0:
                ref_b = b[r0:r0 + 1]
                qt = qi * jnp.exp(bq - ref_b)
                kt = k[:r0] * jnp.exp(ref_b - b[:r0])
                a_ref[rs, :r0] = _dot_nt(_bf16(qt), _bf16(kt))
            if bi < nsub - 1:
                a_ref[rs, r0 + GLA_SUB:] = jnp.zeros((GLA_SUB, c - r0 - GLA_SUB), jnp.float32)
        return a_ref[...]

    def run(n, intra):
        causal = causal_mask(n)
        chunks = [slice(i * n, (i + 1) * n) for i in range(block // n)]
        bs = [log_decay(lf_ref[rows], causal) for rows in chunks]
        for h in range(GLA_HEADS):
            ks = slice(h * GLA_DK, (h + 1) * GLA_DK)
            vs = slice(h * GLA_DV, (h + 1) * GLA_DV)
            s = s_ref[h]
            for rows, b_all in zip(chunks, bs):
                q = q_ref[rows, ks]
                k = k_ref[rows, ks]
                v = _bf16(v_ref[rows, vs])
                b = b_all[:, ks]
                o = _dot(_bf16(q * jnp.exp(b)), _bf16(s)) + _dot(_bf16(intra(q, k, b, causal)), v)

                bt = b.T
                bl = bt[:, n - 1:n]
                kdt = k.T * jnp.exp(bl - bt)
                s = jnp.exp(bl) * s + _dot(_bf16(kdt), v)

                o_ref[rows, vs] = (_layer_norm(o, gam, bet) * sgg_ref[rows, vs]).astype(o_ref.dtype)
            s_ref[h] = s

    block_decay = jnp.max(-jnp.sum(lf_ref[...], axis=0, keepdims=True))
    small_decay = block_decay <= GLA_FACTORED_MAX_DECAY

    @pl.when(small_decay)
    def _():
        run(block, intra_factored)

    @pl.when(jnp.logical_not(small_decay))
    def _():
        run(c, intra_exact)


def _gla(q, k, lf, v, sgg, s0, gam, bet):
    B, T, _ = q.shape
    c = min(CHUNK, T)
    rows = c * GLA_CHUNKS_PER_STEP if T % (c * GLA_CHUNKS_PER_STEP) == 0 else c
    tokk = lambda b, i: (b, i, 0)
    st = lambda b, i: (b, 0, 0, 0)
    whole = lambda b, i: (0, 0)
    return pl.pallas_call(
        functools.partial(_gla_kernel, c=c),
        grid=(B, T // rows),
        in_specs=[pl.BlockSpec((None, rows, _GK), tokk), pl.BlockSpec((None, rows, _GK), tokk),
                  pl.BlockSpec((None, rows, _GK), tokk), pl.BlockSpec((None, rows, _GV), tokk),
                  pl.BlockSpec((None, rows, _GV), tokk),
                  pl.BlockSpec((None, GLA_HEADS, GLA_DK, GLA_DV), st),
                  pl.BlockSpec((1, GLA_DV), whole), pl.BlockSpec((1, GLA_DV), whole)],
        out_specs=[pl.BlockSpec((None, rows, _GV), tokk),
                   pl.BlockSpec((None, GLA_HEADS, GLA_DK, GLA_DV), st)],
        out_shape=[jax.ShapeDtypeStruct((B, T, _GV), MXU_DTYPE),
                   jax.ShapeDtypeStruct((B, GLA_HEADS, GLA_DK, GLA_DV), jnp.float32)],
        scratch_shapes=[pltpu.VMEM((c, c), jnp.float32)],
        compiler_params=pltpu.CompilerParams(dimension_semantics=("arbitrary", "arbitrary"),
                                             vmem_limit_bytes=VMEM_LIMIT),
        name="gla",
    )(q, k, lf, v, sgg, s0, gam, bet)


def _sortable_key(s):
    bits = pltpu.bitcast(s, jnp.int32)
    key = bits ^ ((bits >> 31) & 0x7FFFFFFF)
    return jnp.where(key == -1, 0, key)


def _dsa_kernel(qd_ref, qi_ref, wt_ref, k_ref, vt_ref, ki_ref, o_ref,
                sc_ref, hi_ref, lo_ref, bias_ref, m_ref, acc_ref, lg_ref, *, tq, tk, past, seq_len, topk):
    t0 = pl.program_id(1) * tq
    total = past + seq_len
    qpos = past + t0 + lax.broadcasted_iota(jnp.int32, (1, tq), 1)
    limit = jnp.minimum(((qpos >> 6) + 1) * CHUNK, total)
    lim_max = jnp.minimum(((past + t0 + tq - 1) // CHUNK + 1) * CHUNK, total)
    nk = (lim_max + tk - 1) // tk
    lp = sc_ref.shape[0] * tk
    key_iota = lax.broadcasted_iota(jnp.int32, (tk, tq), 0)

    qis = [_bf16(qi_ref[:, h * IDX_DIM:(h + 1) * IDX_DIM]) for h in range(IDX_HEADS)]
    neg_inf = jnp.float32(-jnp.inf)

    def score_tile(kt, carry):
        kit = ki_ref[pl.ds(pl.multiple_of(kt * tk, tk), tk), :]
        s = None
        for h in range(IDX_HEADS):
            term = wt_ref[h:h + 1, :] * jnp.maximum(_dot_nt(kit, qis[h]), 0.0)
            s = term if s is None else s + term
        s = jnp.where(key_iota < limit - kt * tk, s, neg_inf)
        sc_ref[kt] = s
        key = _sortable_key(s)
        hi_ref[kt] = (key >> 16).astype(jnp.int16)
        lo_ref[kt] = ((key & 0xFFFF) - _HALF).astype(jnp.int16)
        return carry

    lax.fori_loop(0, nk, score_tile, 0)

    def count_ge(ref, cand_off):
        cb = jnp.broadcast_to((cand_off - _HALF).astype(jnp.int16), (_CNT_ROWS, tq))

        def body(kt, acc):
            tile = ref[kt]
            for j in range(tk // _CNT_ROWS):
                acc = acc + jnp.where(tile[j * _CNT_ROWS:(j + 1) * _CNT_ROWS] >= cb, jnp.int16(1), jnp.int16(0))
            return acc

        acc = lax.fori_loop(0, nk, body, jnp.zeros((_CNT_ROWS, tq), jnp.int16))
        return jnp.sum(acc.astype(jnp.int32), axis=0, keepdims=True)

    def count_above(ref, off):
        return jnp.where(off >= 2 * _HALF - 1, 0, count_ge(ref, jnp.minimum(off + 1, 2 * _HALF - 1)))

    def radix_search(ref, target, settled):
        def cond(state):
            i, _, _, open_lanes = state
            return (i < 16) & (open_lanes > 0)

        def bits(state):
            i, off, done, _ = state
            for j in range(_BITS_PER_CHECK):
                cand = off | jnp.left_shift(jnp.int32(1), 15 - (i + j))
                cnt = count_ge(ref, cand)
                off = jnp.where((done > 0) | (cnt < target), off, cand)
                done = jnp.where(cnt == target, 1, done)
            return i + _BITS_PER_CHECK, off, done, 1 - jnp.min(done)

        _, off, done, open_lanes = lax.while_loop(
            cond, bits, (jnp.int32(0), jnp.zeros((1, tq), jnp.int32), settled, 1 - jnp.min(settled)))
        return off, done, open_lanes

    settled0 = jnp.where((limit <= topk) | (qpos >= total), 1, 0)
    hi_off, settled1, open1 = radix_search(hi_ref, topk, settled0)
    hi16 = (hi_off - _HALF).astype(jnp.int16)

    n_above_hi = lax.cond(open1 > 0, lambda: count_above(hi_ref, hi_off), lambda: jnp.zeros((1, tq), jnp.int32))

    @pl.when(open1 > 0)
    def _():
        def band_tile(kt, carry):
            lo_ref[kt] = jnp.where(hi_ref[kt] == hi16, lo_ref[kt], jnp.int16(-_HALF))
            return carry

        lax.fori_loop(0, nk, band_tile, 0)

    need_lo = topk - n_above_hi
    lo_off, settled2, open2 = lax.cond(
        open1 > 0, lambda: radix_search(lo_ref, need_lo, settled1),
        lambda: (jnp.zeros((1, tq), jnp.int32), settled1, jnp.int32(0)))

    del settled2, open2
    thr_key = (hi_off - _HALF) * (2 * _HALF) + jnp.where(settled1 > 0, 0, lo_off)

    def key_to_score(key):
        t = pltpu.bitcast(key ^ ((key >> 31) & 0x7FFFFFFF), jnp.float32)
        return jnp.where(t != t, neg_inf, t)

    def count_scores(t):
        tb = jnp.broadcast_to(t, (_SCORE_CNT_ROWS, tq))

        def body(kt, accs):
            gt, ge = accs
            tile = sc_ref[kt]
            for j in range(tk // _SCORE_CNT_ROWS):
                rows = tile[j * _SCORE_CNT_ROWS:(j + 1) * _SCORE_CNT_ROWS]
                gt = gt + jnp.where(rows > tb, 1, 0)
                ge = ge + jnp.where(rows >= tb, 1, 0)
            return gt, ge

        zero = jnp.zeros((_SCORE_CNT_ROWS, tq), jnp.int32)
        gt, ge = lax.fori_loop(0, nk, body, (zero, zero))
        return jnp.sum(gt, axis=0, keepdims=True), jnp.sum(ge, axis=0, keepdims=True)

    def is_kth(n_gt, n_ge):
        return (settled0 > 0) | ((n_gt <= topk) & (n_ge >= topk))

    def search_on_scores():
        int_min = jnp.int32(-2 ** 31)

        def bit(i, key):
            cand = key ^ jnp.left_shift(jnp.int32(1), 31 - i)
            _, n_ge = count_scores(key_to_score(cand))
            return jnp.where(n_ge >= topk, cand, key)

        t = key_to_score(lax.fori_loop(0, 32, bit, jnp.full((1, tq), int_min, jnp.int32)))
        return (t,) + count_scores(t)

    thr = jnp.where(settled0 > 0, neg_inf, key_to_score(thr_key))
    n_gt, n_ge = count_scores(thr)
    confirmed = jnp.min(jnp.where(is_kth(n_gt, n_ge), 1, 0))
    thr, n_gt, n_ge = lax.cond(confirmed > 0, lambda: (thr, n_gt, n_ge), search_on_scores)
    thr = jnp.where(settled0 > 0, neg_inf, thr)

    tied = (n_ge > topk) & (settled0 == 0)
    need = jnp.where(tied, topk - n_gt, 2 * lp).astype(jnp.float32)
    any_tied = jnp.max(jnp.where(tied, 1, 0))

    @pl.when(any_tied == 0)
    def _():
        def select_tile(kt, carry):
            sel = (sc_ref[kt] >= thr) & (key_iota < limit - kt * tk)
            bias_ref[kt] = jnp.where(sel, 0.0, NEG_BIG)
            return carry

        lax.fori_loop(0, nk, select_tile, 0)

    @pl.when(any_tied > 0)
    def _():
        krow = lax.broadcasted_iota(jnp.int32, (tk, tk), 0)
        kcol = lax.broadcasted_iota(jnp.int32, (tk, tk), 1)
        prefix_ones = _bf16(jnp.where(krow >= kcol, 1.0, 0.0))

        def select_tile(kt, seen):
            s = sc_ref[kt]
            eq = s == thr
            rank = _dot(prefix_ones, _bf16(jnp.where(eq, 1.0, 0.0))) + seen
            sel = ((s > thr) | (eq & (rank <= need))) & (key_iota < limit - kt * tk)
            bias_ref[kt] = jnp.where(sel, 0.0, NEG_BIG)
            return rank[tk - 1:tk]

        lax.fori_loop(0, nk, select_tile, jnp.zeros((1, tq), jnp.float32))

    rep = DSA_HEADS // DSA_KV_HEADS
    m_ref[...] = jnp.full(m_ref.shape, NEG_BIG, jnp.float32)
    acc_ref[...] = jnp.zeros(acc_ref.shape, jnp.float32)
    qs = [_bf16(qd_ref[:, h * DSA_HEAD_DIM:(h + 1) * DSA_HEAD_DIM]) for h in range(DSA_HEADS)]

    def attend_tile(kt, carry):
        rows = pl.ds(pl.multiple_of(kt * tk, tk), tk)
        bias = bias_ref[kt]

        def logits(h):
            g = h // rep
            return _dot_nt(k_ref[rows, g * DSA_HEAD_DIM:(g + 1) * DSA_HEAD_DIM], qs[h])

        for h in range(_LG_BUFS - 1):
            lg_ref[h] = logits(h)
        for h in range(DSA_HEADS):
            if h + _LG_BUFS - 1 < DSA_HEADS:
                lg_ref[(h + _LG_BUFS - 1) % _LG_BUFS] = logits(h + _LG_BUFS - 1)
            lb = lg_ref[h % _LG_BUFS] + bias
            m_old = m_ref[h]
            m_new = jnp.maximum(m_old, jnp.max(lb, axis=0, keepdims=True))
            p = jnp.exp2(lb - m_new)
            a = jnp.exp2(m_old - m_new)
            vtg = vt_ref[kt, h // rep]
            acc_ref[h] = a * acc_ref[h] + _dot(vtg, _bf16(p))
            m_ref[h] = m_new
        return carry

    lax.fori_loop(0, nk, attend_tile, 0)
    for h in range(DSA_HEADS):
        acc = acc_ref[h]
        o_t = acc[:DSA_HEAD_DIM] / acc[DSA_HEAD_DIM:DSA_HEAD_DIM + 1]
        o_ref[:, h * DSA_HEAD_DIM:(h + 1) * DSA_HEAD_DIM] = o_t.T.astype(o_ref.dtype)


DSA_KEY_TILE = 512
_HALF = 1 << 15
_BITS_PER_CHECK = 4
_LG_BUFS = 4
_SCORE_CNT_ROWS = 16
_CNT_ROWS = 64
_VT_ROWS = DSA_HEAD_DIM + 16


def _dsa(qd, qi, wt, k_all, vt, ki_all, past, seq_len, tq, tk):
    B, tp, _ = qd.shape
    lp = k_all.shape[1]
    nkt = lp // tk
    topk = min(TOPK_MAX, (past + seq_len) // 4)
    tokk = lambda b, i: (b, i, 0)
    kv = lambda b, i: (b, 0, 0)
    return pl.pallas_call(
        functools.partial(_dsa_kernel, tq=tq, tk=tk, past=past, seq_len=seq_len, topk=topk),
        grid=(B, tp // tq),
        in_specs=[pl.BlockSpec((None, tq, _DQ), tokk), pl.BlockSpec((None, tq, _IQ), tokk),
                  pl.BlockSpec((IDX_HEADS, tq), lambda b, i: (0, b * (tp // tq) + i)),
                  pl.BlockSpec((None, lp, _DKV), kv),
                  pl.BlockSpec((None, nkt, DSA_KV_HEADS, _VT_ROWS, tk), lambda b, i: (b, 0, 0, 0, 0)),
                  pl.BlockSpec((None, lp, IDX_DIM), kv)],
        out_specs=pl.BlockSpec((None, tq, _DQ), tokk),
        out_shape=jax.ShapeDtypeStruct((B, tp, _DQ), MXU_DTYPE),
        scratch_shapes=[pltpu.VMEM((nkt, tk, tq), jnp.float32),
                        pltpu.VMEM((nkt, tk, tq), jnp.int16), pltpu.VMEM((nkt, tk, tq), jnp.int16),
                        pltpu.VMEM((nkt, tk, tq), jnp.float32),
                        pltpu.VMEM((DSA_HEADS, 1, tq), jnp.float32),
                        pltpu.VMEM((DSA_HEADS, _VT_ROWS, tq), jnp.float32),
                        pltpu.VMEM((_LG_BUFS, tk, tq), jnp.float32)],
        compiler_params=pltpu.CompilerParams(dimension_semantics=("arbitrary", "arbitrary"),
                                             vmem_limit_bytes=VMEM_LIMIT),
        name="dsa",
    )(qd, qi, wt, k_all, vt, ki_all)


def _merge_kernel(og_ref, od_ref, sga_ref, sgb_ref, x_ref, wg_ref, wd_ref, wo_ref, g_ref, b_ref, h_ref):
    m = sga_ref[...] * _dot(_bf16(og_ref[...]), wg_ref[...]) + sgb_ref[...] * _dot(_bf16(od_ref[...]), wd_ref[...])
    mix = _dot(_bf16(m), wo_ref[...])
    h_ref[...] = _layer_norm(ALPHA * x_ref[...] + mix, g_ref[...], b_ref[...])


def _merge(og, od, sga, sgb, x2d, wg, wd, wo, g, b, tm):
    n = x2d.shape[0]
    tok = lambda i: (i, 0)
    whole = lambda i: (0, 0)
    act = pl.BlockSpec((tm, D_MODEL), tok)
    wsp = pl.BlockSpec((D_MODEL, D_MODEL), whole)
    vec = pl.BlockSpec((1, D_MODEL), whole)
    return pl.pallas_call(
        _merge_kernel,
        grid=(n // tm,),
        in_specs=[act, act, act, act, act, wsp, wsp, wsp, vec, vec],
        out_specs=act,
        out_shape=jax.ShapeDtypeStruct((n, D_MODEL), jnp.float32),
        compiler_params=pltpu.CompilerParams(dimension_semantics=("arbitrary",), vmem_limit_bytes=VMEM_LIMIT),
        name="merge",
    )(og, od, sga, sgb, x2d, wg, wd, wo, g, b)


def _first_lane_where(mask, lane):
    return jnp.min(jnp.where(mask, lane, LANES), axis=-1, keepdims=True)


def _route(logits):
    lane = lax.broadcasted_iota(jnp.int32, logits.shape, 1)
    is_g = lane < N_GROUPS
    gl = jnp.where(is_g, logits, NEG_BIG)
    gmax = jnp.max(gl, axis=-1, keepdims=True)
    g_sel = _first_lane_where(is_g & (gl == gmax), lane)
    g_prob = 1.0 / jnp.sum(jnp.where(is_g, jnp.exp(gl - gmax), 0.0), axis=-1, keepdims=True)
    e_lane = lane - N_GROUPS
    in_grp = (e_lane >= 0) & (e_lane < N_EXPERTS) & ((e_lane >> 2) == g_sel)
    el = jnp.where(in_grp, logits, NEG_BIG)
    v1 = jnp.max(el, axis=-1, keepdims=True)
    i1 = _first_lane_where(in_grp & (el == v1), lane)
    rest = in_grp & (lane != i1)
    el2 = jnp.where(rest, logits, NEG_BIG)
    v2 = jnp.max(el2, axis=-1, keepdims=True)
    i2 = _first_lane_where(rest & (el2 == v2), lane)
    e2 = jnp.exp(v2 - v1)
    w1 = g_prob / (1.0 + e2)
    w2 = g_prob * e2 / (1.0 + e2)
    return jnp.where(lane == i1, w1, 0.0) + jnp.where(lane == i2, w2, 0.0)


def _moe_kernel(h_ref, wr_ref, br_ref, wg_ref, wu_ref, wd_ref, g_ref, b_ref, y_ref, gate_ref, acc_ref, hb_ref):
    e = pl.program_id(1)

    @pl.when(e == 0)
    def _():
        h = h_ref[...]
        hb = _bf16(h)
        hb_ref[...] = hb
        h_lo = _bf16(h - hb.astype(jnp.float32))
        logits = _dot(hb, wr_ref[0]) + (_dot(h_lo, wr_ref[0]) + _dot(hb, wr_ref[1])) + br_ref[...]
        gate_ref[...] = _route(logits)
        acc_ref[...] = jnp.zeros(acc_ref.shape, jnp.float32)

    hb = hb_ref[...]
    lane = lax.broadcasted_iota(jnp.int32, gate_ref.shape, 1)
    gcol = jnp.sum(jnp.where(lane == e + N_GROUPS, gate_ref[...], 0.0), axis=-1, keepdims=True)
    a = _dot(hb, wg_ref[...])
    hid = a * _sigmoid(a) * _dot(hb, wu_ref[...]) * gcol
    acc_ref[...] += _dot(_bf16(hid), wd_ref[...])

    @pl.when(e == N_EXPERTS - 1)
    def _():
        y_ref[...] = _layer_norm(ALPHA * h_ref[...] + acc_ref[...], g_ref[...], b_ref[...])


def _moe(h2d, wr, br, wg, wu, wd, g, b, tm):
    n = h2d.shape[0]
    tok = lambda i, e: (i, 0)
    whole = lambda i, e: (0, 0)
    per_e = lambda i, e: (e, 0, 0)
    return pl.pallas_call(
        _moe_kernel,
        grid=(n // tm, N_EXPERTS),
        in_specs=[pl.BlockSpec((tm, D_MODEL), tok),
                  pl.BlockSpec((2, D_MODEL, LANES), lambda i, e: (0, 0, 0)), pl.BlockSpec((1, LANES), whole),
                  pl.BlockSpec((None, D_MODEL, D_EXPERT), per_e),
                  pl.BlockSpec((None, D_MODEL, D_EXPERT), per_e),
                  pl.BlockSpec((None, D_EXPERT, D_MODEL), per_e),
                  pl.BlockSpec((1, D_MODEL), whole), pl.BlockSpec((1, D_MODEL), whole)],
        out_specs=pl.BlockSpec((tm, D_MODEL), tok),
        out_shape=jax.ShapeDtypeStruct((n, D_MODEL), jnp.float32),
        scratch_shapes=[pltpu.VMEM((tm, LANES), jnp.float32), pltpu.VMEM((tm, D_MODEL), jnp.float32),
                        pltpu.VMEM((tm, D_MODEL), MXU_DTYPE)],
        compiler_params=pltpu.CompilerParams(dimension_semantics=("arbitrary", "arbitrary"),
                                             vmem_limit_bytes=VMEM_LIMIT),
        name="moe",
    )(h2d, wr, br, wg, wu, wd, g, b)


def _prep_weights(w_in, w_up, b_up, gla_g, gla_b, w_bg, w_bd, w_o, ln1_g, ln1_b, w_rg, b_rg, w_re, b_re,
                  w_eg, w_eu, w_ed, ln2_g, ln2_b):
    w, wup, bup = _prep_w_in(w_in, w_up, b_up)
    w_re2 = jnp.transpose(w_re, (1, 0, 2)).reshape(D_MODEL, N_EXPERTS)
    rpad = jnp.zeros((D_MODEL, LANES - N_GROUPS - N_EXPERTS), w_rg.dtype)
    wr = jnp.concatenate([w_rg, w_re2, rpad], axis=1)
    wr_hi = _bf16(wr)
    wr = jnp.stack([wr_hi, _bf16(wr - wr_hi.astype(jnp.float32))])
    br =jnp.concatenate([b_rg, b_re.reshape(N_EXPERTS), jnp.zeros((LANES - N_GROUPS - N_EXPERTS,), b_rg.dtype)])
    row = lambda a: a.reshape(1, -1)
    return dict(w=w, wup=wup, bup=bup, gla_g=row(gla_g), gla_b=row(gla_b), wbg=_bf16(w_bg), wbd=_bf16(w_bd),
                wo=_bf16(w_o), ln1_g=row(ln1_g), ln1_b=row(ln1_b), wr=wr, br=row(br), weg=_bf16(w_eg),
                weu=_bf16(w_eu), wed=_bf16(w_ed), ln2_g=row(ln2_g), ln2_b=row(ln2_b))


MOE_TOKEN_TILE = 1024


def _token_tile(n):
    for tm in (512, 256, 128):
        if n % tm == 0:
            return tm
    raise ValueError(f"token count {n} is not a multiple of 128")


def _trunk_layer(x, s0, past_k, past_v, past_ki, p):
    B, T, _ = x.shape
    past = past_k.shape[1]
    n = B * T
    tm = _token_tile(n)
    assert T % tm == 0 or tm % T == 0
    x2d = x.reshape(n, D_MODEL)
    (gq, gk, gv, sgg, lf, dq, dk, dv, iq, sga, sgb, ki, kb, vb, kib, wt) = _in_proj(
        x2d, p["w"], p["wup"], p["bup"], T, past, tm)
    seq = lambda a: a.reshape(B, T, a.shape[-1])

    o_gla, s_fin = _gla(seq(gq), seq(gk), seq(lf), seq(gv), seq(sgg), s0, p["gla_g"], p["gla_b"])

    total = past + T
    tk = DSA_KEY_TILE
    lp = -(-total // tk) * tk

    def keys(past_rows, new_rows):
        rows = jnp.concatenate([_bf16(past_rows.reshape(B, past, new_rows.shape[-1])), seq(new_rows)], axis=1)
        return jnp.pad(rows, ((0, 0), (0, lp - total), (0, 0)))

    nkt = lp // tk
    vt = jnp.transpose(keys(past_v, vb).reshape(B, nkt, tk, _DKV), (0, 1, 3, 2))
    vt = vt.reshape(B, nkt, DSA_KV_HEADS, DSA_HEAD_DIM, tk)
    ones = jnp.ones((B, nkt, DSA_KV_HEADS, 1, tk), jnp.bfloat16)
    zpad = jnp.zeros((B, nkt, DSA_KV_HEADS, _VT_ROWS - DSA_HEAD_DIM - 1, tk), jnp.bfloat16)
    vt = jnp.concatenate([vt, ones, zpad], axis=3)

    tq = 256 if T % 256 == 0 else LANES
    tp = -(-T // tq) * tq
    qpad = lambda a: jnp.pad(a, ((0, 0), (0, tp - T), (0, 0)))
    wt = jnp.pad(wt.reshape(IDX_HEADS, B, T), ((0, 0), (0, 0), (0, tp - T))).reshape(IDX_HEADS, B * tp)
    o_dsa = _dsa(qpad(seq(dq)), qpad(seq(iq)), wt, keys(past_k, kb), vt, keys(past_ki, kib),
                 past, T, tq, tk)[:, :T]

    h = _merge(o_gla.reshape(n, _GV), o_dsa.reshape(n, _DQ), sga, sgb, x2d, p["wbg"], p["wbd"], p["wo"],
               p["ln1_g"], p["ln1_b"], tm)
    tm_moe = MOE_TOKEN_TILE if n % MOE_TOKEN_TILE == 0 else tm
    y = _moe(h, p["wr"], p["br"], p["weg"], p["weu"], p["wed"], p["ln2_g"], p["ln2_b"], tm_moe)
    cache_rows = lambda a: a.reshape(B, T, DSA_KV_HEADS, DSA_HEAD_DIM)
    return y.reshape(B, T, D_MODEL), s_fin, cache_rows(dk), cache_rows(dv), seq(ki)


def kernel(x_prompt, x_sample, state_gla, cache_k, cache_v, cache_k_idx, w_in, w_gla_gate_up, b_gla_gate,
           gla_norm_g, gla_norm_b, w_branch_gla, w_branch_dsa, w_out, ln1_g, ln1_b, w_router_group,
           b_router_group, w_router_expert, b_router_expert, w_expert_gate, w_expert_up, w_expert_down,
           ln2_g, ln2_b):
    B = x_prompt.shape[0]
    l = 0
    p = _prep_weights(w_in[l], w_gla_gate_up[l], b_gla_gate[l], gla_norm_g[l], gla_norm_b[l], w_branch_gla[l],
                      w_branch_dsa[l], w_out[l], ln1_g[l], ln1_b[l], w_router_group[l], b_router_group[l],
                      w_router_expert[l], b_router_expert[l], w_expert_gate[l], w_expert_up[l],
                      w_expert_down[l], ln2_g[l], ln2_b[l])
    s0 = jnp.zeros((B, GLA_HEADS, GLA_DK, GLA_DV), jnp.float32)
    ek = jnp.zeros((B, 0, DSA_KV_HEADS, DSA_HEAD_DIM), x_prompt.dtype)
    eki = jnp.zeros((B, 0, IDX_DIM), x_prompt.dtype)
    yp, sp, kp, vp, kip = _trunk_layer(x_prompt, s0, ek, ek, eki, p)
    ys, ss, ksn, vsn, kisn = _trunk_layer(x_sample, state_gla[l], cache_k[l], cache_v[l], cache_k_idx[l], p)
    stack = lambda a: a[None]
    return (yp, ys, stack(sp), stack(kp), stack(vp), stack(kip), stack(ss), stack(ksn), stack(vsn), stack(kisn))
```

```python
import functools
import math

import jax
import jax.numpy as jnp
from jax import lax
from jax.experimental import pallas as pl
from jax.experimental.pallas import tpu as pltpu

D_MODEL = 1024
CHUNK = 64
ROPE_THETA = 10000.0
LN_EPS = 1e-5
GLA_HEADS = 4
GLA_DK = D_MODEL // 2 // GLA_HEADS
GLA_DV = D_MODEL // GLA_HEADS
GLA_GATE_RANK = 16
GLA_TAU = 16.0
DSA_HEADS = 8
DSA_KV_HEADS = 2
DSA_HEAD_DIM = D_MODEL // DSA_HEADS
IDX_HEADS = 8
IDX_DIM = 64
TOPK_MAX = 256
IDX_W_SCALE = (IDX_HEADS ** -0.5) * (IDX_DIM ** -0.5)
N_GROUPS = 4
EXPERTS_PER_GROUP = 4
N_EXPERTS = N_GROUPS * EXPERTS_PER_GROUP
D_EXPERT = 256
TOP_K_IN_GROUP = 2
DEPTH = 1
ALPHA = (2 * DEPTH) ** 0.25

LANES = 128
GLA_SUB = 16
GLA_CHUNKS_PER_STEP = 4
GLA_FACTORED_MAX_DECAY = 60.0
VMEM_LIMIT = 56 * 1024 * 1024
NEG_BIG = -1e30
MXU_DTYPE = jnp.bfloat16
DSA_Q_DTYPE = MXU_DTYPE
DSA_Q_SCALE = (DSA_HEAD_DIM ** -0.5) * math.log2(math.e)

_GK = GLA_HEADS * GLA_DK
_GV = GLA_HEADS * GLA_DV
_DQ = DSA_HEADS * DSA_HEAD_DIM
_DKV = DSA_KV_HEADS * DSA_HEAD_DIM
_IQ = IDX_HEADS * IDX_DIM
_C_GQ = 0
_C_GK = _C_GQ + _GK
_C_GV = _C_GK + _GK
_C_GG = _C_GV + _GV
_C_DQ = _C_GG + _GV
_C_DK = _C_DQ + _DQ
_C_DV = _C_DK + _DKV
_C_IQ = _C_DV + _DKV
_C_MISC = _C_IQ + _IQ
_C_GA = _C_MISC + LANES
_C_GB = _C_GA + D_MODEL
_C_END = _C_GB + D_MODEL
_MISC_GR = IDX_DIM
_MISC_IW = IDX_DIM + GLA_GATE_RANK


def _bf16(a):
    return a.astype(jnp.bfloat16)


def _dot(a, b):
    return jnp.dot(a, b, preferred_element_type=jnp.float32)


def _dot_nt(a, b):
    return lax.dot_general(a, b, (((1,), (1,)), ((), ())), preferred_element_type=jnp.float32)


def _sigmoid(x):
    return 1.0 / (1.0 + jnp.exp(-x))


def _layer_norm(x, g, b):
    mu = jnp.mean(x, axis=-1, keepdims=True)
    xc = x - mu
    var = jnp.mean(xc * xc, axis=-1, keepdims=True)
    return xc * lax.rsqrt(var + LN_EPS) * g + b


def _rope_full(z, cos, sin):
    return z * cos + pltpu.roll(z, DSA_HEAD_DIM // 2, axis=1) * sin


def _rope_pair(z, cos, sin, first_half):
    q = IDX_DIM // 2
    partner = jnp.where(first_half, pltpu.roll(z, LANES - q, axis=1), pltpu.roll(z, q, axis=1))
    return z * cos + partner * sin


def _in_proj_kernel(x_ref, w_ref, wup_ref, bup_ref, cos_ref, sin_ref, cosi_ref, sini_ref,
                    gq_ref, gk_ref, gv_ref, sgg_ref, lf_ref, dq_ref, dk_ref, dv_ref, iq_ref,
                    sga_ref, sgb_ref, ki_ref, kb_ref, vb_ref, kib_ref, wt_ref):
    xb = _bf16(x_ref[...])

    def proj(c0, width):
        return _dot(xb, w_ref[:, c0:c0 + width])

    gq_ref[...] = proj(_C_GQ, _GK) * (GLA_DK ** -0.5)
    gk_ref[...] = proj(_C_GK, _GK)
    gv_ref[...] = proj(_C_GV, _GV).astype(gv_ref.dtype)
    gg = proj(_C_GG, _GV)
    sgg_ref[...] = gg * _sigmoid(gg)
    sga_ref[...] = _sigmoid(proj(_C_GA, D_MODEL))
    sgb_ref[...] = _sigmoid(proj(_C_GB, D_MODEL))

    cos = cos_ref[...]
    sin = sin_ref[...]
    zq = proj(_C_DQ, _DQ)
    for h in range(DSA_HEADS):
        hs = slice(h * DSA_HEAD_DIM, (h + 1) * DSA_HEAD_DIM)
        dq_ref[:, hs] = (_rope_full(zq[:, hs], cos, sin) * DSA_Q_SCALE).astype(dq_ref.dtype)
    zkv = proj(_C_DK, 2 * _DKV)
    for h in range(DSA_KV_HEADS):
        hs = slice(h * DSA_HEAD_DIM, (h + 1) * DSA_HEAD_DIM)
        kh = _rope_full(zkv[:, hs], cos, sin)
        dk_ref[:, h, :] = kh
        kb_ref[:, hs] = kh.astype(kb_ref.dtype)
        dv_ref[:, h, :] = zkv[:, _DKV + h * DSA_HEAD_DIM:_DKV + (h + 1) * DSA_HEAD_DIM]
    vb_ref[...] = zkv[:, _DKV:].astype(vb_ref.dtype)

    cosi = cosi_ref[...]
    sini = sini_ref[...]
    lane = lax.broadcasted_iota(jnp.int32, cosi.shape, 1)
    first_half = (lane & (IDX_DIM - 1)) < (IDX_DIM // 2)
    zi = proj(_C_IQ, _IQ + LANES)
    for p in range(_IQ // LANES):
        ps = slice(p * LANES, (p + 1) * LANES)
        iq_ref[:, ps] = _rope_pair(zi[:, ps], cosi, sini, first_half).astype(iq_ref.dtype)

    misc = zi[:, _IQ:]
    pre = _dot(_bf16(misc), wup_ref[...]) + bup_ref[...]
    lf_ref[...] = (jnp.minimum(pre, 0.0) - jnp.log(1.0 + jnp.exp(-jnp.abs(pre)))) * (1.0 / GLA_TAU)
    ki = _rope_pair(misc, cosi, sini, first_half)[:, :IDX_DIM]
    ki_ref[...] = ki
    kib_ref[...] = ki.astype(kib_ref.dtype)
    wt_ref[...] = (misc * IDX_W_SCALE).T[_MISC_IW:_MISC_IW + IDX_HEADS]


def _rope_tables(T, past, rows):
    pos = (past + jnp.arange(T, dtype=jnp.int32)).astype(jnp.float32)

    def tables(dim):
        half = dim // 2
        inv = ROPE_THETA ** (-jnp.arange(half, dtype=jnp.float32) / half)
        ang = pos[:, None] * inv[None, :]
        c, s = jnp.cos(ang), jnp.sin(ang)
        reps = LANES // dim
        cos = jnp.tile(jnp.concatenate([c, c], axis=1), (1, reps))
        sin = jnp.tile(jnp.concatenate([-s, s], axis=1), (1, reps))
        if rows > T:
            cos = jnp.tile(cos, (rows // T, 1))
            sin = jnp.tile(sin, (rows // T, 1))
        return cos, sin

    return tables(DSA_HEAD_DIM) + tables(IDX_DIM)


def _prep_w_in(w_in, w_up, b_up):
    sizes = (_GK, _GK, _GV, _GV, GLA_GATE_RANK, _DQ, _DKV, _DKV, _IQ, IDX_DIM, IDX_HEADS, D_MODEL, D_MODEL)
    offs = [0]
    for s in sizes:
        offs.append(offs[-1] + s)
    gq, gk, gv, gg, gr, dq, dk, dv, iq, ik, iw, ga, gb = (w_in[:, offs[i]:offs[i + 1]] for i in range(len(sizes)))
    pad = jnp.zeros((D_MODEL, LANES - IDX_DIM - GLA_GATE_RANK - IDX_HEADS), w_in.dtype)
    w = jnp.concatenate([gq, gk, gv, gg, dq, dk, dv, iq, ik, gr, iw, pad, ga, gb], axis=1)
    wup = jnp.zeros((LANES, _GK), w_up.dtype).at[_MISC_GR:_MISC_GR + GLA_GATE_RANK].set(w_up)
    return _bf16(w), _bf16(wup), b_up.reshape(1, _GK)


def _in_proj(x2d, w, wup, bup, T, past, tm):
    n = x2d.shape[0]
    rows = max(T, tm)
    cos, sin, cosi, sini = _rope_tables(T, past, rows)
    nt = rows // tm
    tok = lambda i: (i, 0)
    tab = lambda i: (i % nt, 0)
    whole = lambda i: (0, 0)
    f32 = jnp.float32
    rows_of = lambda wd, dt: (jax.ShapeDtypeStruct((n, wd), dt), pl.BlockSpec((tm, wd), tok))
    kv_rows = (jax.ShapeDtypeStruct((n, DSA_KV_HEADS, DSA_HEAD_DIM), f32),
               pl.BlockSpec((tm, DSA_KV_HEADS, DSA_HEAD_DIM), lambda i: (i, 0, 0)))
    outs = [rows_of(_GK, f32), rows_of(_GK, f32), rows_of(_GV, MXU_DTYPE), rows_of(_GV, f32), rows_of(_GK, f32),
            rows_of(_DQ, DSA_Q_DTYPE), kv_rows, kv_rows, rows_of(_IQ, DSA_Q_DTYPE),
            rows_of(D_MODEL, f32), rows_of(D_MODEL, f32),
            rows_of(IDX_DIM, f32), rows_of(_DKV, MXU_DTYPE), rows_of(_DKV, MXU_DTYPE), rows_of(IDX_DIM, MXU_DTYPE),
            (jax.ShapeDtypeStruct((IDX_HEADS, n), f32), pl.BlockSpec((IDX_HEADS, tm), lambda i: (0, i)))]
    return pl.pallas_call(
        _in_proj_kernel,
        grid=(n // tm,),
        in_specs=[pl.BlockSpec((tm, D_MODEL), tok),
                  pl.BlockSpec((D_MODEL, _C_END), whole),
                  pl.BlockSpec((LANES, _GK), whole),
                  pl.BlockSpec((1, _GK), whole),
                  pl.BlockSpec((tm, LANES), tab), pl.BlockSpec((tm, LANES), tab),
                  pl.BlockSpec((tm, LANES), tab), pl.BlockSpec((tm, LANES), tab)],
        out_specs=[spec for _, spec in outs],
        out_shape=[shape for shape, _ in outs],
        compiler_params=pltpu.CompilerParams(dimension_semantics=("arbitrary",), vmem_limit_bytes=VMEM_LIMIT),
        name="in_proj",
    )(x2d, w, wup, bup, cos, sin, cosi, sini)


def _split3(a):
    hi = _bf16(a)
    r1 = a - hi.astype(jnp.float32)
    mid = _bf16(r1)
    lo = _bf16(r1 - mid.astype(jnp.float32))
    return hi, mid, lo


def _gla_kernel(q_ref, k_ref, lf_ref, v_ref, sgg_ref, s0_ref, g_ref, b_ref, o_ref, s_ref, a_ref, *, c):
    ci = pl.program_id(1)

    @pl.when(ci == 0)
    def _():
        s_ref[...] = s0_ref[...]

    block = q_ref.shape[0]
    gam = g_ref[...]
    bet = b_ref[...]

    def causal_mask(n):
        return lax.broadcasted_iota(jnp.int32, (n, n), 0) >= lax.broadcasted_iota(jnp.int32, (n, n), 1)

    def log_decay(lf, causal):
        tri = _bf16(jnp.where(causal, 1.0, 0.0))
        hi, mid, lo = _split3(lf)
        return _dot(tri, hi) + _dot(tri, mid) + _dot(tri, lo)

    def intra_factored(q, k, b, causal):
        b0 = b[0:1]
        a = _dot_nt(_bf16(q * jnp.exp(b - b0)), _bf16(k * jnp.exp(b0 - b)))
        return jnp.where(causal, a, 0.0)

    def intra_exact(q, k, b, causal):
        nsub = q.shape[0] // GLA_SUB
        srow = lax.broadcasted_iota(jnp.int32, (GLA_SUB, GLA_SUB), 0)
        scol = lax.broadcasted_iota(jnp.int32, (GLA_SUB, GLA_SUB), 1)
        for bi in range(nsub):
            r0 = bi * GLA_SUB
            rs = slice(r0, r0 + GLA_SUB)
            qi = q[rs]
            bq = b[rs]
            ki = k[rs]
            d = jnp.zeros((GLA_SUB, GLA_SUB), jnp.float32)
            for j in range(GLA_SUB):
                f = jnp.exp(jnp.minimum(bq - bq[j:j + 1], 0.0)) * qi * ki[j:j + 1]
                d = jnp.where(scol == j, jnp.sum(f, axis=-1, keepdims=True), d)
            a_ref[rs, r0:r0 + GLA_SUB] = jnp.where(srow >= scol, d, 0.0)
            if bi > 0:
                ref_b = b[r0:r0 + 1]
                qt = qi * jnp.exp(bq - ref_b)
                kt = k[:r0] * jnp.exp(ref_b - b[:r0])
                a_ref[rs, :r0] = _dot_nt(_bf16(qt), _bf16(kt))
            if bi < nsub - 1:
                a_ref[rs, r0 + GLA_SUB:] = jnp.zeros((GLA_SUB, c - r0 - GLA_SUB), jnp.float32)
        return a_ref[...]

    def run(n, intra):
        causal = causal_mask(n)
        chunks = [slice(i * n, (i + 1) * n) for i in range(block // n)]
        bs = [log_decay(lf_ref[rows], causal) for rows in chunks]
        for h in range(GLA_HEADS):
            ks = slice(h * GLA_DK, (h + 1) * GLA_DK)
            vs = slice(h * GLA_DV, (h + 1) * GLA_DV)
            s = s_ref[h]
            for rows, b_all in zip(chunks, bs):
                q = q_ref[rows, ks]
                k = k_ref[rows, ks]
                v = _bf16(v_ref[rows, vs])
                b = b_all[:, ks]
                o = _dot(_bf16(q * jnp.exp(b)), _bf16(s)) + _dot(_bf16(intra(q, k, b, causal)), v)

                bt = b.T
                bl = bt[:, n - 1:n]
                kdt = k.T * jnp.exp(bl - bt)
                s = jnp.exp(bl) * s + _dot(_bf16(kdt), v)

                o_ref[rows, vs] = (_layer_norm(o, gam, bet) * sgg_ref[rows, vs]).astype(o_ref.dtype)
            s_ref[h] = s

    block_decay = jnp.max(-jnp.sum(lf_ref[...], axis=0, keepdims=True))
    small_decay = block_decay <= GLA_FACTORED_MAX_DECAY

    @pl.when(small_decay)
    def _():
        run(block, intra_factored)

    @pl.when(jnp.logical_not(small_decay))
    def _():
        run(c, intra_exact)


def _gla(q, k, lf, v, sgg, s0, gam, bet):
    B, T, _ = q.shape
    c = min(CHUNK, T)
    rows = c * GLA_CHUNKS_PER_STEP if T % (c * GLA_CHUNKS_PER_STEP) == 0 else c
    tokk = lambda b, i: (b, i, 0)
    st = lambda b, i: (b, 0, 0, 0)
    whole = lambda b, i: (0, 0)
    return pl.pallas_call(
        functools.partial(_gla_kernel, c=c),
        grid=(B, T // rows),
        in_specs=[pl.BlockSpec((None, rows, _GK), tokk), pl.BlockSpec((None, rows, _GK), tokk),
                  pl.BlockSpec((None, rows, _GK), tokk), pl.BlockSpec((None, rows, _GV), tokk),
                  pl.BlockSpec((None, rows, _GV), tokk),
                  pl.BlockSpec((None, GLA_HEADS, GLA_DK, GLA_DV), st),
                  pl.BlockSpec((1, GLA_DV), whole), pl.BlockSpec((1, GLA_DV), whole)],
        out_specs=[pl.BlockSpec((None, rows, _GV), tokk),
                   pl.BlockSpec((None, GLA_HEADS, GLA_DK, GLA_DV), st)],
        out_shape=[jax.ShapeDtypeStruct((B, T, _GV), MXU_DTYPE),
                   jax.ShapeDtypeStruct((B, GLA_HEADS, GLA_DK, GLA_DV), jnp.float32)],
        scratch_shapes=[pltpu.VMEM((c, c), jnp.float32)],
        compiler_params=pltpu.CompilerParams(dimension_semantics=("arbitrary", "arbitrary"),
                                             vmem_limit_bytes=VMEM_LIMIT),
        name="gla",
    )(q, k, lf, v, sgg, s0, gam, bet)


def _sortable_key(s):
    bits = pltpu.bitcast(s, jnp.int32)
    key = bits ^ ((bits >> 31) & 0x7FFFFFFF)
    return jnp.where(key == -1, 0, key)


def _dsa_kernel(qd_ref, qi_ref, wt_ref, k_ref, vt_ref, ki_ref, o_ref,
                sc_ref, hi_ref, lo_ref, bias_ref, m_ref, acc_ref, lg_ref, *, tq, tk, past, seq_len, topk):
    t0 = pl.program_id(1) * tq
    total = past + seq_len
    qpos = past + t0 + lax.broadcasted_iota(jnp.int32, (1, tq), 1)
    limit = jnp.minimum(((qpos >> 6) + 1) * CHUNK, total)
    lim_max = jnp.minimum(((past + t0 + tq - 1) // CHUNK + 1) * CHUNK, total)
    nk = (lim_max + tk - 1) // tk
    lp = sc_ref.shape[0] * tk
    key_iota = lax.broadcasted_iota(jnp.int32, (tk, tq), 0)

    qis = [_bf16(qi_ref[:, h * IDX_DIM:(h + 1) * IDX_DIM]) for h in range(IDX_HEADS)]
    neg_inf = jnp.float32(-jnp.inf)

    def score_tile(kt, carry):
        kit = ki_ref[pl.ds(pl.multiple_of(kt * tk, tk), tk), :]
        s = None
        for h in range(IDX_HEADS):
            term = wt_ref[h:h + 1, :] * jnp.maximum(_dot_nt(kit, qis[h]), 0.0)
            s = term if s is None else s + term
        s = jnp.where(key_iota < limit - kt * tk, s, neg_inf)
        sc_ref[kt] = s
        key = _sortable_key(s)
        hi_ref[kt] = (key >> 16).astype(jnp.int16)
        lo_ref[kt] = ((key & 0xFFFF) - _HALF).astype(jnp.int16)
        return carry

    lax.fori_loop(0, nk, score_tile, 0)

    def count_ge(ref, cand_off):
        cb = jnp.broadcast_to((cand_off - _HALF).astype(jnp.int16), (_CNT_ROWS, tq))

        def body(kt, acc):
            tile = ref[kt]
            for j in range(tk // _CNT_ROWS):
                acc = acc + jnp.where(tile[j * _CNT_ROWS:(j + 1) * _CNT_ROWS] >= cb, jnp.int16(1), jnp.int16(0))
            return acc

        acc = lax.fori_loop(0, nk, body, jnp.zeros((_CNT_ROWS, tq), jnp.int16))
        return jnp.sum(acc.astype(jnp.int32), axis=0, keepdims=True)

    def count_above(ref, off):
        return jnp.where(off >= 2 * _HALF - 1, 0, count_ge(ref, jnp.minimum(off + 1, 2 * _HALF - 1)))

    def radix_search(ref, target, settled):
        def cond(state):
            i, _, _, open_lanes = state
            return (i < 16) & (open_lanes > 0)

        def bits(state):
            i, off, done, _ = state
            for j in range(_BITS_PER_CHECK):
                cand = off | jnp.left_shift(jnp.int32(1), 15 - (i + j))
                cnt = count_ge(ref, cand)
                off = jnp.where((done > 0) | (cnt < target), off, cand)
                done = jnp.where(cnt == target, 1, done)
            return i + _BITS_PER_CHECK, off, done, 1 - jnp.min(done)

        _, off, done, open_lanes = lax.while_loop(
            cond, bits, (jnp.int32(0), jnp.zeros((1, tq), jnp.int32), settled, 1 - jnp.min(settled)))
        return off, done, open_lanes

    settled0 = jnp.where((limit <= topk) | (qpos >= total), 1, 0)
    hi_off, settled1, open1 = radix_search(hi_ref, topk, settled0)
    hi16 = (hi_off - _HALF).astype(jnp.int16)

    n_above_hi = lax.cond(open1 > 0, lambda: count_above(hi_ref, hi_off), lambda: jnp.zeros((1, tq), jnp.int32))

    @pl.when(open1 > 0)
    def _():
        def band_tile(kt, carry):
            lo_ref[kt] = jnp.where(hi_ref[kt] == hi16, lo_ref[kt], jnp.int16(-_HALF))
            return carry

        lax.fori_loop(0, nk, band_tile, 0)

    need_lo = topk - n_above_hi
    lo_off, settled2, open2 = lax.cond(
        open1 > 0, lambda: radix_search(lo_ref, need_lo, settled1),
        lambda: (jnp.zeros((1, tq), jnp.int32), settled1, jnp.int32(0)))

    del settled2, open2
    thr_key = (hi_off - _HALF) * (2 * _HALF) + jnp.where(settled1 > 0, 0, lo_off)

    def key_to_score(key):
        t = pltpu.bitcast(key ^ ((key >> 31) & 0x7FFFFFFF), jnp.float32)
        return jnp.where(t != t, neg_inf, t)

    def count_scores(t):
        tb = jnp.broadcast_to(t, (_SCORE_CNT_ROWS, tq))

        def body(kt, accs):
            gt, ge = accs
            tile = sc_ref[kt]
            for j in range(tk // _SCORE_CNT_ROWS):
                rows = tile[j * _SCORE_CNT_ROWS:(j + 1) * _SCORE_CNT_ROWS]
                gt = gt + jnp.where(rows > tb, 1, 0)
                ge = ge + jnp.where(rows >= tb, 1, 0)
            return gt, ge

        zero = jnp.zeros((_SCORE_CNT_ROWS, tq), jnp.int32)
        gt, ge = lax.fori_loop(0, nk, body, (zero, zero))
        return jnp.sum(gt, axis=0, keepdims=True), jnp.sum(ge, axis=0, keepdims=True)

    def is_kth(n_gt, n_ge):
        return (settled0 > 0) | ((n_gt <= topk) & (n_ge >= topk))

    def search_on_scores():
        int_min = jnp.int32(-2 ** 31)

        def bit(i, key):
            cand = key ^ jnp.left_shift(jnp.int32(1), 31 - i)
            _, n_ge = count_scores(key_to_score(cand))
            return jnp.where(n_ge >= topk, cand, key)

        t = key_to_score(lax.fori_loop(0, 32, bit, jnp.full((1, tq), int_min, jnp.int32)))
        return (t,) + count_scores(t)

    thr = jnp.where(settled0 > 0, neg_inf, key_to_score(thr_key))
    n_gt, n_ge = count_scores(thr)
    confirmed = jnp.min(jnp.where(is_kth(n_gt, n_ge), 1, 0))
    thr, n_gt, n_ge = lax.cond(confirmed > 0, lambda: (thr, n_gt, n_ge), search_on_scores)
    thr = jnp.where(settled0 > 0, neg_inf, thr)

    tied = (n_ge > topk) & (settled0 == 0)
    need = jnp.where(tied, topk - n_gt, 2 * lp).astype(jnp.float32)
    any_tied = jnp.max(jnp.where(tied, 1, 0))

    @pl.when(any_tied == 0)
    def _():
        def select_tile(kt, carry):
            sel = (sc_ref[kt] >= thr) & (key_iota < limit - kt * tk)
            bias_ref[kt] = jnp.where(sel, 0.0, NEG_BIG)
            return carry

        lax.fori_loop(0, nk, select_tile, 0)

    @pl.when(any_tied > 0)
    def _():
        krow = lax.broadcasted_iota(jnp.int32, (tk, tk), 0)
        kcol = lax.broadcasted_iota(jnp.int32, (tk, tk), 1)
        prefix_ones = _bf16(jnp.where(krow >= kcol, 1.0, 0.0))

        def select_tile(kt, seen):
            s = sc_ref[kt]
            eq = s == thr
            rank = _dot(prefix_ones, _bf16(jnp.where(eq, 1.0, 0.0))) + seen
            sel = ((s > thr) | (eq & (rank <= need))) & (key_iota < limit - kt * tk)
            bias_ref[kt] = jnp.where(sel, 0.0, NEG_BIG)
            return rank[tk - 1:tk]

        lax.fori_loop(0, nk, select_tile, jnp.zeros((1, tq), jnp.float32))

    rep = DSA_HEADS // DSA_KV_HEADS
    m_ref[...] = jnp.full(m_ref.shape, NEG_BIG, jnp.float32)
    acc_ref[...] = jnp.zeros(acc_ref.shape, jnp.float32)
    qs = [_bf16(qd_ref[:, h * DSA_HEAD_DIM:(h + 1) * DSA_HEAD_DIM]) for h in range(DSA_HEADS)]

    def attend_tile(kt, carry):
        rows = pl.ds(pl.multiple_of(kt * tk, tk), tk)
        bias = bias_ref[kt]

        def logits(h):
            g = h // rep
            return _dot_nt(k_ref[rows, g * DSA_HEAD_DIM:(g + 1) * DSA_HEAD_DIM], qs[h])

        for h in range(_LG_BUFS - 1):
            lg_ref[h] = logits(h)
        for h in range(DSA_HEADS):
            if h + _LG_BUFS - 1 < DSA_HEADS:
                lg_ref[(h + _LG_BUFS - 1) % _LG_BUFS] = logits(h + _LG_BUFS - 1)
            lb = lg_ref[h % _LG_BUFS] + bias
            m_old = m_ref[h]
            m_new = jnp.maximum(m_old, jnp.max(lb, axis=0, keepdims=True))
            p = jnp.exp2(lb - m_new)
            a = jnp.exp2(m_old - m_new)
            vtg = vt_ref[kt, h // rep]
            acc_ref[h] = a * acc_ref[h] + _dot(vtg, _bf16(p))
            m_ref[h] = m_new
        return carry

    lax.fori_loop(0, nk, attend_tile, 0)
    for h in range(DSA_HEADS):
        acc = acc_ref[h]
        o_t = acc[:DSA_HEAD_DIM] / acc[DSA_HEAD_DIM:DSA_HEAD_DIM + 1]
        o_ref[:, h * DSA_HEAD_DIM:(h + 1) * DSA_HEAD_DIM] = o_t.T.astype(o_ref.dtype)


DSA_KEY_TILE = 512
_HALF = 1 << 15
_BITS_PER_CHECK = 4
_LG_BUFS = 4
_SCORE_CNT_ROWS = 16
_CNT_ROWS = 64
_VT_ROWS = DSA_HEAD_DIM + 16


def _dsa(qd, qi, wt, k_all, vt, ki_all, past, seq_len, tq, tk):
    B, tp, _ = qd.shape
    lp = k_all.shape[1]
    nkt = lp // tk
    topk = min(TOPK_MAX, (past + seq_len) // 4)
    tokk = lambda b, i: (b, i, 0)
    kv = lambda b, i: (b, 0, 0)
    return pl.pallas_call(
        functools.partial(_dsa_kernel, tq=tq, tk=tk, past=past, seq_len=seq_len, topk=topk),
        grid=(B, tp // tq),
        in_specs=[pl.BlockSpec((None, tq, _DQ), tokk), pl.BlockSpec((None, tq, _IQ), tokk),
                  pl.BlockSpec((IDX_HEADS, tq), lambda b, i: (0, b * (tp // tq) + i)),
                  pl.BlockSpec((None, lp, _DKV), kv),
                  pl.BlockSpec((None, nkt, DSA_KV_HEADS, _VT_ROWS, tk), lambda b, i: (b, 0, 0, 0, 0)),
                  pl.BlockSpec((None, lp, IDX_DIM), kv)],
        out_specs=pl.BlockSpec((None, tq, _DQ), tokk),
        out_shape=jax.ShapeDtypeStruct((B, tp, _DQ), MXU_DTYPE),
        scratch_shapes=[pltpu.VMEM((nkt, tk, tq), jnp.float32),
                        pltpu.VMEM((nkt, tk, tq), jnp.int16), pltpu.VMEM((nkt, tk, tq), jnp.int16),
                        pltpu.VMEM((nkt, tk, tq), jnp.float32),
                        pltpu.VMEM((DSA_HEADS, 1, tq), jnp.float32),
                        pltpu.VMEM((DSA_HEADS, _VT_ROWS, tq), jnp.float32),
                        pltpu.VMEM((_LG_BUFS, tk, tq), jnp.float32)],
        compiler_params=pltpu.CompilerParams(dimension_semantics=("arbitrary", "arbitrary"),
                                             vmem_limit_bytes=VMEM_LIMIT),
        name="dsa",
    )(qd, qi, wt, k_all, vt, ki_all)


def _merge_kernel(og_ref, od_ref, sga_ref, sgb_ref, x_ref, wg_ref, wd_ref, wo_ref, g_ref, b_ref, h_ref):
    m = sga_ref[...] * _dot(_bf16(og_ref[...]), wg_ref[...]) + sgb_ref[...] * _dot(_bf16(od_ref[...]), wd_ref[...])
    mix = _dot(_bf16(m), wo_ref[...])
    h_ref[...] = _layer_norm(ALPHA * x_ref[...] + mix, g_ref[...], b_ref[...])


def _merge(og, od, sga, sgb, x2d, wg, wd, wo, g, b, tm):
    n = x2d.shape[0]
    tok = lambda i: (i, 0)
    whole = lambda i: (0, 0)
    act = pl.BlockSpec((tm, D_MODEL), tok)
    wsp = pl.BlockSpec((D_MODEL, D_MODEL), whole)
    vec = pl.BlockSpec((1, D_MODEL), whole)
    return pl.pallas_call(
        _merge_kernel,
        grid=(n // tm,),
        in_specs=[act, act, act, act, act, wsp, wsp, wsp, vec, vec],
        out_specs=act,
        out_shape=jax.ShapeDtypeStruct((n, D_MODEL), jnp.float32),
        compiler_params=pltpu.CompilerParams(dimension_semantics=("arbitrary",), vmem_limit_bytes=VMEM_LIMIT),
        name="merge",
    )(og, od, sga, sgb, x2d, wg, wd, wo, g, b)


def _first_lane_where(mask, lane):
    return jnp.min(jnp.where(mask, lane, LANES), axis=-1, keepdims=True)


def _route(logits):
    lane = lax.broadcasted_iota(jnp.int32, logits.shape, 1)
    is_g = lane < N_GROUPS
    gl = jnp.where(is_g, logits, NEG_BIG)
    gmax = jnp.max(gl, axis=-1, keepdims=True)
    g_sel = _first_lane_where(is_g & (gl == gmax), lane)
    g_prob = 1.0 / jnp.sum(jnp.where(is_g, jnp.exp(gl - gmax), 0.0), axis=-1, keepdims=True)
    e_lane = lane - N_GROUPS
    in_grp = (e_lane >= 0) & (e_lane < N_EXPERTS) & ((e_lane >> 2) == g_sel)
    el = jnp.where(in_grp, logits, NEG_BIG)
    v1 = jnp.max(el, axis=-1, keepdims=True)
    i1 = _first_lane_where(in_grp & (el == v1), lane)
    rest = in_grp & (lane != i1)
    el2 = jnp.where(rest, logits, NEG_BIG)
    v2 = jnp.max(el2, axis=-1, keepdims=True)
    i2 = _first_lane_where(rest & (el2 == v2), lane)
    e2 = jnp.exp(v2 - v1)
    w1 = g_prob / (1.0 + e2)
    w2 = g_prob * e2 / (1.0 + e2)
    return jnp.where(lane == i1, w1, 0.0) + jnp.where(lane == i2, w2, 0.0)


def _moe_kernel(h_ref, wr_ref, br_ref, wg_ref, wu_ref, wd_ref, g_ref, b_ref, y_ref, gate_ref, acc_ref, hb_ref):
    step = pl.program_id(1)

    @pl.when(step == 0)
    def _():
        h = h_ref[...]
        hb = _bf16(h)
        hb_ref[...] = hb
        h_lo = _bf16(h - hb.astype(jnp.float32))
        logits = _dot(hb, wr_ref[0]) + (_dot(h_lo, wr_ref[0]) + _dot(hb, wr_ref[1])) + br_ref[...]
        gate_ref[...] = _route(logits)
        acc_ref[...] = jnp.zeros(acc_ref.shape, jnp.float32)

    hb = hb_ref[...]
    gate = gate_ref[...]
    lane = lax.broadcasted_iota(jnp.int32, gate.shape, 1)
    hids = []
    for j in range(MOE_EXPERTS_PER_STEP):
        e = step * MOE_EXPERTS_PER_STEP + j
        gcol = jnp.sum(jnp.where(lane == e + N_GROUPS, gate, 0.0), axis=-1, keepdims=True)
        a = _dot(hb, wg_ref[j])
        hids.append(_bf16(a * _sigmoid(a) * _dot(hb, wu_ref[j]) * gcol))
    wd = wd_ref[...].reshape(MOE_EXPERTS_PER_STEP * D_EXPERT, D_MODEL)
    acc_ref[...] += _dot(jnp.concatenate(hids, axis=1), wd)

    @pl.when(step == N_EXPERTS // MOE_EXPERTS_PER_STEP - 1)
    def _():
        y_ref[...] = _layer_norm(ALPHA * h_ref[...] + acc_ref[...], g_ref[...], b_ref[...])


def _moe(h2d, wr, br, wg, wu, wd, g, b, tm):
    n = h2d.shape[0]
    tok = lambda i, e: (i, 0)
    whole = lambda i, e: (0, 0)
    per_e = lambda i, e: (e, 0, 0)
    eps = MOE_EXPERTS_PER_STEP
    return pl.pallas_call(
        _moe_kernel,
        grid=(n // tm, N_EXPERTS // eps),
        in_specs=[pl.BlockSpec((tm, D_MODEL), tok),
                  pl.BlockSpec((2, D_MODEL, LANES), lambda i, e: (0, 0, 0)), pl.BlockSpec((1, LANES), whole),
                  pl.BlockSpec((eps, D_MODEL, D_EXPERT), per_e),
                  pl.BlockSpec((eps, D_MODEL, D_EXPERT), per_e),
                  pl.BlockSpec((eps, D_EXPERT, D_MODEL), per_e),
                  pl.BlockSpec((1, D_MODEL), whole), pl.BlockSpec((1, D_MODEL), whole)],
        out_specs=pl.BlockSpec((tm, D_MODEL), tok),
        out_shape=jax.ShapeDtypeStruct((n, D_MODEL), jnp.float32),
        scratch_shapes=[pltpu.VMEM((tm, LANES), jnp.float32), pltpu.VMEM((tm, D_MODEL), jnp.float32),
                        pltpu.VMEM((tm, D_MODEL), MXU_DTYPE)],
        compiler_params=pltpu.CompilerParams(dimension_semantics=("arbitrary", "arbitrary"),
                                             vmem_limit_bytes=VMEM_LIMIT),
        name="moe",
    )(h2d, wr, br, wg, wu, wd, g, b)


def _prep_weights(w_in, w_up, b_up, gla_g, gla_b, w_bg, w_bd, w_o, ln1_g, ln1_b, w_rg, b_rg, w_re, b_re,
                  w_eg, w_eu, w_ed, ln2_g, ln2_b):
    w, wup, bup = _prep_w_in(w_in, w_up, b_up)
    w_re2 = jnp.transpose(w_re, (1, 0, 2)).reshape(D_MODEL, N_EXPERTS)
    rpad = jnp.zeros((D_MODEL, LANES - N_GROUPS - N_EXPERTS), w_rg.dtype)
    wr = jnp.concatenate([w_rg, w_re2, rpad], axis=1)
    wr_hi = _bf16(wr)
    wr = jnp.stack([wr_hi, _bf16(wr - wr_hi.astype(jnp.float32))])
    br =jnp.concatenate([b_rg, b_re.reshape(N_EXPERTS), jnp.zeros((LANES - N_GROUPS - N_EXPERTS,), b_rg.dtype)])
    row = lambda a: a.reshape(1, -1)
    return dict(w=w, wup=wup, bup=bup, gla_g=row(gla_g), gla_b=row(gla_b), wbg=_bf16(w_bg), wbd=_bf16(w_bd),
                wo=_bf16(w_o), ln1_g=row(ln1_g), ln1_b=row(ln1_b), wr=wr, br=row(br), weg=_bf16(w_eg),
                weu=_bf16(w_eu), wed=_bf16(w_ed), ln2_g=row(ln2_g), ln2_b=row(ln2_b))


MOE_TOKEN_TILE = 1024
MOE_EXPERTS_PER_STEP = 4


def _token_tile(n):
    for tm in (512, 256, 128):
        if n % tm == 0:
            return tm
    raise ValueError(f"token count {n} is not a multiple of 128")


def _trunk_layer(x, s0, past_k, past_v, past_ki, p):
    B, T, _ = x.shape
    past = past_k.shape[1]
    n = B * T
    tm = _token_tile(n)
    assert T % tm == 0 or tm % T == 0
    x2d = x.reshape(n, D_MODEL)
    (gq, gk, gv, sgg, lf, dq, dk, dv, iq, sga, sgb, ki, kb, vb, kib, wt) = _in_proj(
        x2d, p["w"], p["wup"], p["bup"], T, past, tm)
    seq = lambda a: a.reshape(B, T, a.shape[-1])

    o_gla, s_fin = _gla(seq(gq), seq(gk), seq(lf), seq(gv), seq(sgg), s0, p["gla_g"], p["gla_b"])

    total = past + T
    tk = DSA_KEY_TILE
    lp = -(-total // tk) * tk

    def keys(past_rows, new_rows):
        rows = jnp.concatenate([_bf16(past_rows.reshape(B, past, new_rows.shape[-1])), seq(new_rows)], axis=1)
        return jnp.pad(rows, ((0, 0), (0, lp - total), (0, 0)))

    nkt = lp // tk
    vt = jnp.transpose(keys(past_v, vb).reshape(B, nkt, tk, _DKV), (0, 1, 3, 2))
    vt = vt.reshape(B, nkt, DSA_KV_HEADS, DSA_HEAD_DIM, tk)
    ones = jnp.ones((B, nkt, DSA_KV_HEADS, 1, tk), jnp.bfloat16)
    zpad = jnp.zeros((B, nkt, DSA_KV_HEADS, _VT_ROWS - DSA_HEAD_DIM - 1, tk), jnp.bfloat16)
    vt = jnp.concatenate([vt, ones, zpad], axis=3)

    tq = 256 if T % 256 == 0 else LANES
    tp = -(-T // tq) * tq
    qpad = lambda a: jnp.pad(a, ((0, 0), (0, tp - T), (0, 0)))
    wt = jnp.pad(wt.reshape(IDX_HEADS, B, T), ((0, 0), (0, 0), (0, tp - T))).reshape(IDX_HEADS, B * tp)
    o_dsa = _dsa(qpad(seq(dq)), qpad(seq(iq)), wt, keys(past_k, kb), vt, keys(past_ki, kib),
                 past, T, tq, tk)[:, :T]

    h = _merge(o_gla.reshape(n, _GV), o_dsa.reshape(n, _DQ), sga, sgb, x2d, p["wbg"], p["wbd"], p["wo"],
               p["ln1_g"], p["ln1_b"], tm)
    tm_moe = MOE_TOKEN_TILE if n % MOE_TOKEN_TILE == 0 else tm
    y = _moe(h, p["wr"], p["br"], p["weg"], p["weu"], p["wed"], p["ln2_g"], p["ln2_b"], tm_moe)
    cache_rows = lambda a: a.reshape(B, T, DSA_KV_HEADS, DSA_HEAD_DIM)
    return y.reshape(B, T, D_MODEL), s_fin, cache_rows(dk), cache_rows(dv), seq(ki)


def kernel(x_prompt, x_sample, state_gla, cache_k, cache_v, cache_k_idx, w_in, w_gla_gate_up, b_gla_gate,
           gla_norm_g, gla_norm_b, w_branch_gla, w_branch_dsa, w_out, ln1_g, ln1_b, w_router_group,
           b_router_group, w_router_expert, b_router_expert, w_expert_gate, w_expert_up, w_expert_down,
           ln2_g, ln2_b):
    B = x_prompt.shape[0]
    l = 0
    p = _prep_weights(w_in[l], w_gla_gate_up[l], b_gla_gate[l], gla_norm_g[l], gla_norm_b[l], w_branch_gla[l],
                      w_branch_dsa[l], w_out[l], ln1_g[l], ln1_b[l], w_router_group[l], b_router_group[l],
                      w_router_expert[l], b_router_expert[l], w_expert_gate[l], w_expert_up[l],
                      w_expert_down[l], ln2_g[l], ln2_b[l])
    s0 = jnp.zeros((B, GLA_HEADS, GLA_DK, GLA_DV), jnp.float32)
    ek = jnp.zeros((B, 0, DSA_KV_HEADS, DSA_HEAD_DIM), x_prompt.dtype)
    eki = jnp.zeros((B, 0, IDX_DIM), x_prompt.dtype)
    yp, sp, kp, vp, kip = _trunk_layer(x_prompt, s0, ek, ek, eki, p)
    ys, ss, ksn, vsn, kisn = _trunk_layer(x_sample, state_gla[l], cache_k[l], cache_v[l], cache_k_idx[l], p)
    stack = lambda a: a[None]
    return (yp, ys, stack(sp), stack(kp), stack(vp), stack(kip), stack(ss), stack(ksn), stack(vsn), stack(kisn))
```

```python
import functools
import math

import jax
import jax.numpy as jnp
from jax import lax
from jax.experimental import pallas as pl
from jax.experimental.pallas import tpu as pltpu

D_MODEL = 1024
CHUNK = 64
ROPE_THETA = 10000.0
LN_EPS = 1e-5
GLA_HEADS = 4
GLA_DK = D_MODEL // 2 // GLA_HEADS
GLA_DV = D_MODEL // GLA_HEADS
GLA_GATE_RANK = 16
GLA_TAU = 16.0
DSA_HEADS = 8
DSA_KV_HEADS = 2
DSA_HEAD_DIM = D_MODEL // DSA_HEADS
IDX_HEADS = 8
IDX_DIM = 64
TOPK_MAX = 256
IDX_W_SCALE = (IDX_HEADS ** -0.5) * (IDX_DIM ** -0.5)
N_GROUPS = 4
EXPERTS_PER_GROUP = 4
N_EXPERTS = N_GROUPS * EXPERTS_PER_GROUP
D_EXPERT = 256
TOP_K_IN_GROUP = 2
DEPTH = 1
ALPHA = (2 * DEPTH) ** 0.25

LANES = 128
GLA_SUB = 16
GLA_CHUNKS_PER_STEP = 4
GLA_FACTORED_MAX_DECAY = 60.0
VMEM_LIMIT = 56 * 1024 * 1024
NEG_BIG = -1e30
MXU_DTYPE = jnp.bfloat16
DSA_Q_DTYPE = MXU_DTYPE
DSA_Q_SCALE = (DSA_HEAD_DIM ** -0.5) * math.log2(math.e)

_GK = GLA_HEADS * GLA_DK
_GV = GLA_HEADS * GLA_DV
_DQ = DSA_HEADS * DSA_HEAD_DIM
_DKV = DSA_KV_HEADS * DSA_HEAD_DIM
_IQ = IDX_HEADS * IDX_DIM
_C_GQ = 0
_C_GK = _C_GQ + _GK
_C_GV = _C_GK + _GK
_C_GG = _C_GV + _GV
_C_DQ = _C_GG + _GV
_C_DK = _C_DQ + _DQ
_C_DV = _C_DK + _DKV
_C_IQ = _C_DV + _DKV
_C_MISC = _C_IQ + _IQ
_C_GA = _C_MISC + LANES
_C_GB = _C_GA + D_MODEL
_C_END = _C_GB + D_MODEL
_MISC_GR = IDX_DIM
_MISC_IW = IDX_DIM + GLA_GATE_RANK


def _bf16(a):
    return a.astype(jnp.bfloat16)


def _dot(a, b):
    return jnp.dot(a, b, preferred_element_type=jnp.float32)


def _dot_nt(a, b):
    return lax.dot_general(a, b, (((1,), (1,)), ((), ())), preferred_element_type=jnp.float32)


def _sigmoid(x):
    return 1.0 / (1.0 + jnp.exp(-x))


def _layer_norm(x, g, b):
    mu = jnp.mean(x, axis=-1, keepdims=True)
    xc = x - mu
    var = jnp.mean(xc * xc, axis=-1, keepdims=True)
    return xc * lax.rsqrt(var + LN_EPS) * g + b


def _rope_full(z, cos, sin):
    return z * cos + pltpu.roll(z, DSA_HEAD_DIM // 2, axis=1) * sin


def _rope_pair(z, cos, sin, first_half):
    q = IDX_DIM // 2
    partner = jnp.where(first_half, pltpu.roll(z, LANES - q, axis=1), pltpu.roll(z, q, axis=1))
    return z * cos + partner * sin


def _in_proj_kernel(x_ref, w_ref, wup_ref, bup_ref, cos_ref, sin_ref, cosi_ref, sini_ref,
                    gq_ref, gk_ref, gv_ref, sgg_ref, lf_ref, dq_ref, dk_ref, dv_ref, iq_ref,
                    sga_ref, sgb_ref, ki_ref, kb_ref, vb_ref, kib_ref, wt_ref):
    xb = _bf16(x_ref[...])

    def proj(c0, width):
        return _dot(xb, w_ref[:, c0:c0 + width])

    gq_ref[...] = proj(_C_GQ, _GK) * (GLA_DK ** -0.5)
    gk_ref[...] = proj(_C_GK, _GK)
    gv_ref[...] = proj(_C_GV, _GV).astype(gv_ref.dtype)
    gg = proj(_C_GG, _GV)
    sgg_ref[...] = gg * _sigmoid(gg)
    sga_ref[...] = _sigmoid(proj(_C_GA, D_MODEL))
    sgb_ref[...] = _sigmoid(proj(_C_GB, D_MODEL))

    cos = cos_ref[...]
    sin = sin_ref[...]
    zq = proj(_C_DQ, _DQ)
    for h in range(DSA_HEADS):
        hs = slice(h * DSA_HEAD_DIM, (h + 1) * DSA_HEAD_DIM)
        dq_ref[:, hs] = (_rope_full(zq[:, hs], cos, sin) * DSA_Q_SCALE).astype(dq_ref.dtype)
    zkv = proj(_C_DK, 2 * _DKV)
    for h in range(DSA_KV_HEADS):
        hs = slice(h * DSA_HEAD_DIM, (h + 1) * DSA_HEAD_DIM)
        kh = _rope_full(zkv[:, hs], cos, sin)
        dk_ref[:, h, :] = kh
        kb_ref[:, hs] = kh.astype(kb_ref.dtype)
        dv_ref[:, h, :] = zkv[:, _DKV + h * DSA_HEAD_DIM:_DKV + (h + 1) * DSA_HEAD_DIM]
    vb_ref[...] = zkv[:, _DKV:].astype(vb_ref.dtype)

    cosi = cosi_ref[...]
    sini = sini_ref[...]
    lane = lax.broadcasted_iota(jnp.int32, cosi.shape, 1)
    first_half = (lane & (IDX_DIM - 1)) < (IDX_DIM // 2)
    zi = proj(_C_IQ, _IQ + LANES)
    for p in range(_IQ // LANES):
        ps = slice(p * LANES, (p + 1) * LANES)
        iq_ref[:, ps] = _rope_pair(zi[:, ps], cosi, sini, first_half).astype(iq_ref.dtype)

    misc = zi[:, _IQ:]
    pre = _dot(_bf16(misc), wup_ref[...]) + bup_ref[...]
    lf_ref[...] = (jnp.minimum(pre, 0.0) - jnp.log(1.0 + jnp.exp(-jnp.abs(pre)))) * (1.0 / GLA_TAU)
    ki = _rope_pair(misc, cosi, sini, first_half)[:, :IDX_DIM]
    ki_ref[...] = ki
    kib_ref[...] = ki.astype(kib_ref.dtype)
    wt_ref[...] = (misc * IDX_W_SCALE).T[_MISC_IW:_MISC_IW + IDX_HEADS]


def _rope_tables(T, past, rows):
    pos = (past + jnp.arange(T, dtype=jnp.int32)).astype(jnp.float32)

    def tables(dim):
        half = dim // 2
        inv = ROPE_THETA ** (-jnp.arange(half, dtype=jnp.float32) / half)
        ang = pos[:, None] * inv[None, :]
        c, s = jnp.cos(ang), jnp.sin(ang)
        reps = LANES // dim
        cos = jnp.tile(jnp.concatenate([c, c], axis=1), (1, reps))
        sin = jnp.tile(jnp.concatenate([-s, s], axis=1), (1, reps))
        if rows > T:
            cos = jnp.tile(cos, (rows // T, 1))
            sin = jnp.tile(sin, (rows // T, 1))
        return cos, sin

    return tables(DSA_HEAD_DIM) + tables(IDX_DIM)


def _prep_w_in(w_in, w_up, b_up):
    sizes = (_GK, _GK, _GV, _GV, GLA_GATE_RANK, _DQ, _DKV, _DKV, _IQ, IDX_DIM, IDX_HEADS, D_MODEL, D_MODEL)
    offs = [0]
    for s in sizes:
        offs.append(offs[-1] + s)
    gq, gk, gv, gg, gr, dq, dk, dv, iq, ik, iw, ga, gb = (w_in[:, offs[i]:offs[i + 1]] for i in range(len(sizes)))
    pad = jnp.zeros((D_MODEL, LANES - IDX_DIM - GLA_GATE_RANK - IDX_HEADS), w_in.dtype)
    w = jnp.concatenate([gq, gk, gv, gg, dq, dk, dv, iq, ik, gr, iw, pad, ga, gb], axis=1)
    wup = jnp.zeros((LANES, _GK), w_up.dtype).at[_MISC_GR:_MISC_GR + GLA_GATE_RANK].set(w_up)
    return _bf16(w), _bf16(wup), b_up.reshape(1, _GK)


def _in_proj(x2d, w, wup, bup, T, past, tm):
    n = x2d.shape[0]
    rows = max(T, tm)
    cos, sin, cosi, sini = _rope_tables(T, past, rows)
    nt = rows // tm
    tok = lambda i: (i, 0)
    tab = lambda i: (i % nt, 0)
    whole = lambda i: (0, 0)
    f32 = jnp.float32
    rows_of = lambda wd, dt: (jax.ShapeDtypeStruct((n, wd), dt), pl.BlockSpec((tm, wd), tok))
    kv_rows = (jax.ShapeDtypeStruct((n, DSA_KV_HEADS, DSA_HEAD_DIM), f32),
               pl.BlockSpec((tm, DSA_KV_HEADS, DSA_HEAD_DIM), lambda i: (i, 0, 0)))
    outs = [rows_of(_GK, f32), rows_of(_GK, f32), rows_of(_GV, MXU_DTYPE), rows_of(_GV, f32), rows_of(_GK, f32),
            rows_of(_DQ, DSA_Q_DTYPE), kv_rows, kv_rows, rows_of(_IQ, DSA_Q_DTYPE),
            rows_of(D_MODEL, f32), rows_of(D_MODEL, f32),
            rows_of(IDX_DIM, f32), rows_of(_DKV, MXU_DTYPE), rows_of(_DKV, MXU_DTYPE), rows_of(IDX_DIM, MXU_DTYPE),
            (jax.ShapeDtypeStruct((IDX_HEADS, n), f32), pl.BlockSpec((IDX_HEADS, tm), lambda i: (0, i)))]
    return pl.pallas_call(
        _in_proj_kernel,
        grid=(n // tm,),
        in_specs=[pl.BlockSpec((tm, D_MODEL), tok),
                  pl.BlockSpec((D_MODEL, _C_END), whole),
                  pl.BlockSpec((LANES, _GK), whole),
                  pl.BlockSpec((1, _GK), whole),
                  pl.BlockSpec((tm, LANES), tab), pl.BlockSpec((tm, LANES), tab),
                  pl.BlockSpec((tm, LANES), tab), pl.BlockSpec((tm, LANES), tab)],
        out_specs=[spec for _, spec in outs],
        out_shape=[shape for shape, _ in outs],
        compiler_params=pltpu.CompilerParams(dimension_semantics=("arbitrary",), vmem_limit_bytes=VMEM_LIMIT),
        name="in_proj",
    )(x2d, w, wup, bup, cos, sin, cosi, sini)


def _split3(a):
    hi = _bf16(a)
    r1 = a - hi.astype(jnp.float32)
    mid = _bf16(r1)
    lo = _bf16(r1 - mid.astype(jnp.float32))
    return hi, mid, lo


def _gla_kernel(q_ref, k_ref, lf_ref, v_ref, sgg_ref, s0_ref, g_ref, b_ref, o_ref, s_ref, a_ref, *, c):
    ci = pl.program_id(1)

    @pl.when(ci == 0)
    def _():
        s_ref[...] = s0_ref[...]

    block = q_ref.shape[0]
    gam = g_ref[...]
    bet = b_ref[...]

    def causal_mask(n):
        return lax.broadcasted_iota(jnp.int32, (n, n), 0) >= lax.broadcasted_iota(jnp.int32, (n, n), 1)

    def log_decay(lf, causal):
        tri = _bf16(jnp.where(causal, 1.0, 0.0))
        hi, mid, lo = _split3(lf)
        return _dot(tri, hi) + _dot(tri, mid) + _dot(tri, lo)

    def intra_factored(q, k, b, causal):
        b0 = b[0:1]
        a = _dot_nt(_bf16(q * jnp.exp(b - b0)), _bf16(k * jnp.exp(b0 - b)))
        return jnp.where(causal, a, 0.0)

    def intra_exact(q, k, b, causal):
        nsub = q.shape[0] // GLA_SUB
        srow = lax.broadcasted_iota(jnp.int32, (GLA_SUB, GLA_SUB), 0)
        scol = lax.broadcasted_iota(jnp.int32, (GLA_SUB, GLA_SUB), 1)
        for bi in range(nsub):
            r0 = bi * GLA_SUB
            rs = slice(r0, r0 + GLA_SUB)
            qi = q[rs]
            bq = b[rs]
            ki = k[rs]
            d = jnp.zeros((GLA_SUB, GLA_SUB), jnp.float32)
            for j in range(GLA_SUB):
                f = jnp.exp(jnp.minimum(bq - bq[j:j + 1], 0.0)) * qi * ki[j:j + 1]
                d = jnp.where(scol == j, jnp.sum(f, axis=-1, keepdims=True), d)
            a_ref[rs, r0:r0 + GLA_SUB] = jnp.where(srow >= scol, d, 0.0)
            if bi > 0:
                ref_b = b[r0:r0 + 1]
                qt = qi * jnp.exp(bq - ref_b)
                kt = k[:r0] * jnp.exp(ref_b - b[:r0])
                a_ref[rs, :r0] = _dot_nt(_bf16(qt), _bf16(kt))
            if bi < nsub - 1:
                a_ref[rs, r0 + GLA_SUB:] = jnp.zeros((GLA_SUB, c - r0 - GLA_SUB), jnp.float32)
        return a_ref[...]

    def run(n, intra):
        causal = causal_mask(n)
        chunks = [slice(i * n, (i + 1) * n) for i in range(block // n)]
        bs = [log_decay(lf_ref[rows], causal) for rows in chunks]
        for h in range(GLA_HEADS):
            ks = slice(h * GLA_DK, (h + 1) * GLA_DK)
            vs = slice(h * GLA_DV, (h + 1) * GLA_DV)
            s = s_ref[h]
            for rows, b_all in zip(chunks, bs):
                q = q_ref[rows, ks]
                k = k_ref[rows, ks]
                v = _bf16(v_ref[rows, vs])
                b = b_all[:, ks]
                o = _dot(_bf16(q * jnp.exp(b)), _bf16(s)) + _dot(_bf16(intra(q, k, b, causal)), v)

                bt = b.T
                bl = bt[:, n - 1:n]
                kdt = k.T * jnp.exp(bl - bt)
                s = jnp.exp(bl) * s + _dot(_bf16(kdt), v)

                o_ref[rows, vs] = (_layer_norm(o, gam, bet) * sgg_ref[rows, vs]).astype(o_ref.dtype)
            s_ref[h] = s

    block_decay = jnp.max(-jnp.sum(lf_ref[...], axis=0, keepdims=True))
    small_decay = block_decay <= GLA_FACTORED_MAX_DECAY

    @pl.when(small_decay)
    def _():
        run(block, intra_factored)

    @pl.when(jnp.logical_not(small_decay))
    def _():
        run(c, intra_exact)


def _gla(q, k, lf, v, sgg, s0, gam, bet):
    B, T, _ = q.shape
    c = min(CHUNK, T)
    rows = c * GLA_CHUNKS_PER_STEP if T % (c * GLA_CHUNKS_PER_STEP) == 0 else c
    tokk = lambda b, i: (b, i, 0)
    st = lambda b, i: (b, 0, 0, 0)
    whole = lambda b, i: (0, 0)
    return pl.pallas_call(
        functools.partial(_gla_kernel, c=c),
        grid=(B, T // rows),
        in_specs=[pl.BlockSpec((None, rows, _GK), tokk), pl.BlockSpec((None, rows, _GK), tokk),
                  pl.BlockSpec((None, rows, _GK), tokk), pl.BlockSpec((None, rows, _GV), tokk),
                  pl.BlockSpec((None, rows, _GV), tokk),
                  pl.BlockSpec((None, GLA_HEADS, GLA_DK, GLA_DV), st),
                  pl.BlockSpec((1, GLA_DV), whole), pl.BlockSpec((1, GLA_DV), whole)],
        out_specs=[pl.BlockSpec((None, rows, _GV), tokk),
                   pl.BlockSpec((None, GLA_HEADS, GLA_DK, GLA_DV), st)],
        out_shape=[jax.ShapeDtypeStruct((B, T, _GV), MXU_DTYPE),
                   jax.ShapeDtypeStruct((B, GLA_HEADS, GLA_DK, GLA_DV), jnp.float32)],
        scratch_shapes=[pltpu.VMEM((c, c), jnp.float32)],
        compiler_params=pltpu.CompilerParams(dimension_semantics=("arbitrary", "arbitrary"),
                                             vmem_limit_bytes=VMEM_LIMIT),
        name="gla",
    )(q, k, lf, v, sgg, s0, gam, bet)


def _sortable_key(s):
    bits = pltpu.bitcast(s, jnp.int32)
    key = bits ^ ((bits >> 31) & 0x7FFFFFFF)
    return jnp.where(key == -1, 0, key)


def _dsa_kernel(qd_ref, qi_ref, wt_ref, k_ref, vt_ref, ki_ref, o_ref,
                sc_ref, hi_ref, lo_ref, bias_ref, m_ref, acc_ref, lg_ref, *, tq, tk, past, seq_len, topk):
    t0 = pl.program_id(1) * tq
    total = past + seq_len
    qpos = past + t0 + lax.broadcasted_iota(jnp.int32, (1, tq), 1)
    limit = jnp.minimum(((qpos >> 6) + 1) * CHUNK, total)
    lim_max = jnp.minimum(((past + t0 + tq - 1) // CHUNK + 1) * CHUNK, total)
    nk = (lim_max + tk - 1) // tk
    lp = sc_ref.shape[0] * tk
    key_iota = lax.broadcasted_iota(jnp.int32, (tk, tq), 0)

    qis = [_bf16(qi_ref[:, h * IDX_DIM:(h + 1) * IDX_DIM]) for h in range(IDX_HEADS)]
    neg_inf = jnp.float32(-jnp.inf)

    def score_tile(kt, carry):
        kit = ki_ref[pl.ds(pl.multiple_of(kt * tk, tk), tk), :]
        s = None
        for h in range(IDX_HEADS):
            term = wt_ref[h:h + 1, :] * jnp.maximum(_dot_nt(kit, qis[h]), 0.0)
            s = term if s is None else s + term
        s = jnp.where(key_iota < limit - kt * tk, s, neg_inf)
        sc_ref[kt] = s
        key = _sortable_key(s)
        hi_ref[kt] = (key >> 16).astype(jnp.int16)
        lo_ref[kt] = ((key & 0xFFFF) - _HALF).astype(jnp.int16)
        return carry

    lax.fori_loop(0, nk, score_tile, 0)

    def count_ge(ref, cand_off):
        cb = jnp.broadcast_to((cand_off - _HALF).astype(jnp.int16), (_CNT_ROWS, tq))

        def body(kt, acc):
            tile = ref[kt]
            for j in range(tk // _CNT_ROWS):
                acc = acc + jnp.where(tile[j * _CNT_ROWS:(j + 1) * _CNT_ROWS] >= cb, jnp.int16(1), jnp.int16(0))
            return acc

        acc = lax.fori_loop(0, nk, body, jnp.zeros((_CNT_ROWS, tq), jnp.int16))
        return jnp.sum(acc.astype(jnp.int32), axis=0, keepdims=True)

    def count_above(ref, off):
        return jnp.where(off >= 2 * _HALF - 1, 0, count_ge(ref, jnp.minimum(off + 1, 2 * _HALF - 1)))

    def radix_search(ref, target, settled):
        def cond(state):
            i, _, _, open_lanes = state
            return (i < 16) & (open_lanes > 0)

        def bits(state):
            i, off, done, _ = state
            for j in range(_BITS_PER_CHECK):
                cand = off | jnp.left_shift(jnp.int32(1), 15 - (i + j))
                cnt = count_ge(ref, cand)
                off = jnp.where((done > 0) | (cnt < target), off, cand)
                done = jnp.where(cnt == target, 1, done)
            return i + _BITS_PER_CHECK, off, done, 1 - jnp.min(done)

        _, off, done, open_lanes = lax.while_loop(
            cond, bits, (jnp.int32(0), jnp.zeros((1, tq), jnp.int32), settled, 1 - jnp.min(settled)))
        return off, done, open_lanes

    settled0 = jnp.where((limit <= topk) | (qpos >= total), 1, 0)
    hi_off, settled1, open1 = radix_search(hi_ref, topk, settled0)
    hi16 = (hi_off - _HALF).astype(jnp.int16)

    n_above_hi = lax.cond(open1 > 0, lambda: count_above(hi_ref, hi_off), lambda: jnp.zeros((1, tq), jnp.int32))

    @pl.when(open1 > 0)
    def _():
        def band_tile(kt, carry):
            lo_ref[kt] = jnp.where(hi_ref[kt] == hi16, lo_ref[kt], jnp.int16(-_HALF))
            return carry

        lax.fori_loop(0, nk, band_tile, 0)

    need_lo = topk - n_above_hi
    lo_off, settled2, open2 = lax.cond(
        open1 > 0, lambda: radix_search(lo_ref, need_lo, settled1),
        lambda: (jnp.zeros((1, tq), jnp.int32), settled1, jnp.int32(0)))

    del settled2, open2
    thr_key = (hi_off - _HALF) * (2 * _HALF) + jnp.where(settled1 > 0, 0, lo_off)

    def key_to_score(key):
        t = pltpu.bitcast(key ^ ((key >> 31) & 0x7FFFFFFF), jnp.float32)
        return jnp.where(t != t, neg_inf, t)

    def count_scores(t):
        tb = jnp.broadcast_to(t, (_SCORE_CNT_ROWS, tq))

        def body(kt, accs):
            gt, ge = accs
            tile = sc_ref[kt]
            for j in range(tk // _SCORE_CNT_ROWS):
                rows = tile[j * _SCORE_CNT_ROWS:(j + 1) * _SCORE_CNT_ROWS]
                gt = gt + jnp.where(rows > tb, 1, 0)
                ge = ge + jnp.where(rows >= tb, 1, 0)
            return gt, ge

        zero = jnp.zeros((_SCORE_CNT_ROWS, tq), jnp.int32)
        gt, ge = lax.fori_loop(0, nk, body, (zero, zero))
        return jnp.sum(gt, axis=0, keepdims=True), jnp.sum(ge, axis=0, keepdims=True)

    def is_kth(n_gt, n_ge):
        return (settled0 > 0) | ((n_gt <= topk) & (n_ge >= topk))

    def search_on_scores():
        int_min = jnp.int32(-2 ** 31)

        def bit(i, key):
            cand = key ^ jnp.left_shift(jnp.int32(1), 31 - i)
            _, n_ge = count_scores(key_to_score(cand))
            return jnp.where(n_ge >= topk, cand, key)

        t = key_to_score(lax.fori_loop(0, 32, bit, jnp.full((1, tq), int_min, jnp.int32)))
        return (t,) + count_scores(t)

    thr = jnp.where(settled0 > 0, neg_inf, key_to_score(thr_key))
    n_gt, n_ge = count_scores(thr)
    confirmed = jnp.min(jnp.where(is_kth(n_gt, n_ge), 1, 0))
    thr, n_gt, n_ge = lax.cond(confirmed > 0, lambda: (thr, n_gt, n_ge), search_on_scores)
    thr = jnp.where(settled0 > 0, neg_inf, thr)

    tied = (n_ge > topk) & (settled0 == 0)
    need = jnp.where(tied, topk - n_gt, 2 * lp).astype(jnp.float32)
    any_tied = jnp.max(jnp.where(tied, 1, 0))

    @pl.when(any_tied == 0)
    def _():
        def select_tile(kt, carry):
            sel = (sc_ref[kt] >= thr) & (key_iota < limit - kt * tk)
            bias_ref[kt] = jnp.where(sel, 0.0, NEG_BIG)
            return carry

        lax.fori_loop(0, nk, select_tile, 0)

    @pl.when(any_tied > 0)
    def _():
        krow = lax.broadcasted_iota(jnp.int32, (tk, tk), 0)
        kcol = lax.broadcasted_iota(jnp.int32, (tk, tk), 1)
        prefix_ones = _bf16(jnp.where(krow >= kcol, 1.0, 0.0))

        def select_tile(kt, seen):
            s = sc_ref[kt]
            eq = s == thr
            rank = _dot(prefix_ones, _bf16(jnp.where(eq, 1.0, 0.0))) + seen
            sel = ((s > thr) | (eq & (rank <= need))) & (key_iota < limit - kt * tk)
            bias_ref[kt] = jnp.where(sel, 0.0, NEG_BIG)
            return rank[tk - 1:tk]

        lax.fori_loop(0, nk, select_tile, jnp.zeros((1, tq), jnp.float32))

    rep = DSA_HEADS // DSA_KV_HEADS
    m_ref[...] = jnp.full(m_ref.shape, NEG_BIG, jnp.float32)
    acc_ref[...] = jnp.zeros(acc_ref.shape, jnp.float32)
    qs = [_bf16(qd_ref[:, h * DSA_HEAD_DIM:(h + 1) * DSA_HEAD_DIM]) for h in range(DSA_HEADS)]

    def attend_tile(kt, carry):
        rows = pl.ds(pl.multiple_of(kt * tk, tk), tk)
        bias = bias_ref[kt]

        def logits(h):
            g = h // rep
            return _dot_nt(k_ref[rows, g * DSA_HEAD_DIM:(g + 1) * DSA_HEAD_DIM], qs[h])

        for h in range(_LG_BUFS - 1):
            lg_ref[h] = logits(h)
        for h in range(DSA_HEADS):
            if h + _LG_BUFS - 1 < DSA_HEADS:
                lg_ref[(h + _LG_BUFS - 1) % _LG_BUFS] = logits(h + _LG_BUFS - 1)
            lb = lg_ref[h % _LG_BUFS] + bias
            m_old = m_ref[h]
            m_new = jnp.maximum(m_old, jnp.max(lb, axis=0, keepdims=True))
            p = jnp.exp2(lb - m_new)
            a = jnp.exp2(m_old - m_new)
            vtg = vt_ref[kt, h // rep]
            acc_ref[h] = a * acc_ref[h] + _dot(vtg, _bf16(p))
            m_ref[h] = m_new
        return carry

    lax.fori_loop(0, nk, attend_tile, 0)
    for h in range(DSA_HEADS):
        acc = acc_ref[h]
        o_t = acc[:DSA_HEAD_DIM] / acc[DSA_HEAD_DIM:DSA_HEAD_DIM + 1]
        o_ref[:, h * DSA_HEAD_DIM:(h + 1) * DSA_HEAD_DIM] = o_t.T.astype(o_ref.dtype)


DSA_KEY_TILE = 512
_HALF = 1 << 15
_BITS_PER_CHECK = 4
_LG_BUFS = 4
_SCORE_CNT_ROWS = 16
_CNT_ROWS = 64
_VT_ROWS = DSA_HEAD_DIM + 16


def _dsa(qd, qi, wt, k_all, vt, ki_all, past, seq_len, tq, tk):
    B, tp, _ = qd.shape
    lp = k_all.shape[1]
    nkt = lp // tk
    topk = min(TOPK_MAX, (past + seq_len) // 4)
    tokk = lambda b, i: (b, i, 0)
    kv = lambda b, i: (b, 0, 0)
    return pl.pallas_call(
        functools.partial(_dsa_kernel, tq=tq, tk=tk, past=past, seq_len=seq_len, topk=topk),
        grid=(B, tp // tq),
        in_specs=[pl.BlockSpec((None, tq, _DQ), tokk), pl.BlockSpec((None, tq, _IQ), tokk),
                  pl.BlockSpec((IDX_HEADS, tq), lambda b, i: (0, b * (tp // tq) + i)),
                  pl.BlockSpec((None, lp, _DKV), kv),
                  pl.BlockSpec((None, nkt, DSA_KV_HEADS, _VT_ROWS, tk), lambda b, i: (b, 0, 0, 0, 0)),
                  pl.BlockSpec((None, lp, IDX_DIM), kv)],
        out_specs=pl.BlockSpec((None, tq, _DQ), tokk),
        out_shape=jax.ShapeDtypeStruct((B, tp, _DQ), MXU_DTYPE),
        scratch_shapes=[pltpu.VMEM((nkt, tk, tq), jnp.float32),
                        pltpu.VMEM((nkt, tk, tq), jnp.int16), pltpu.VMEM((nkt, tk, tq), jnp.int16),
                        pltpu.VMEM((nkt, tk, tq), jnp.float32),
                        pltpu.VMEM((DSA_HEADS, 1, tq), jnp.float32),
                        pltpu.VMEM((DSA_HEADS, _VT_ROWS, tq), jnp.float32),
                        pltpu.VMEM((_LG_BUFS, tk, tq), jnp.float32)],
        compiler_params=pltpu.CompilerParams(dimension_semantics=("arbitrary", "arbitrary"),
                                             vmem_limit_bytes=VMEM_LIMIT),
        name="dsa",
    )(qd, qi, wt, k_all, vt, ki_all)


def _merge_kernel(og_ref, od_ref, sga_ref, sgb_ref, x_ref, wg_ref, wd_ref, wo_ref, g_ref, b_ref, h_ref):
    m = sga_ref[...] * _dot(_bf16(og_ref[...]), wg_ref[...]) + sgb_ref[...] * _dot(_bf16(od_ref[...]), wd_ref[...])
    mix = _dot(_bf16(m), wo_ref[...])
    h_ref[...] = _layer_norm(ALPHA * x_ref[...] + mix, g_ref[...], b_ref[...])


def _merge(og, od, sga, sgb, x2d, wg, wd, wo, g, b, tm):
    n = x2d.shape[0]
    tok = lambda i: (i, 0)
    whole = lambda i: (0, 0)
    act = pl.BlockSpec((tm, D_MODEL), tok)
    wsp = pl.BlockSpec((D_MODEL, D_MODEL), whole)
    vec = pl.BlockSpec((1, D_MODEL), whole)
    return pl.pallas_call(
        _merge_kernel,
        grid=(n // tm,),
        in_specs=[act, act, act, act, act, wsp, wsp, wsp, vec, vec],
        out_specs=act,
        out_shape=jax.ShapeDtypeStruct((n, D_MODEL), jnp.float32),
        compiler_params=pltpu.CompilerParams(dimension_semantics=("arbitrary",), vmem_limit_bytes=VMEM_LIMIT),
        name="merge",
    )(og, od, sga, sgb, x2d, wg, wd, wo, g, b)


def _first_lane_where(mask, lane):
    return jnp.min(jnp.where(mask, lane, LANES), axis=-1, keepdims=True)


def _route(logits):
    lane = lax.broadcasted_iota(jnp.int32, logits.shape, 1)
    is_g = lane < N_GROUPS
    gl = jnp.where(is_g, logits, NEG_BIG)
    gmax = jnp.max(gl, axis=-1, keepdims=True)
    g_sel = _first_lane_where(is_g & (gl == gmax), lane)
    g_prob = 1.0 / jnp.sum(jnp.where(is_g, jnp.exp(gl - gmax), 0.0), axis=-1, keepdims=True)
    e_lane = lane - N_GROUPS
    in_grp = (e_lane >= 0) & (e_lane < N_EXPERTS) & ((e_lane >> 2) == g_sel)
    el = jnp.where(in_grp, logits, NEG_BIG)
    v1 = jnp.max(el, axis=-1, keepdims=True)
    i1 = _first_lane_where(in_grp & (el == v1), lane)
    rest = in_grp & (lane != i1)
    el2 = jnp.where(rest, logits, NEG_BIG)
    v2 = jnp.max(el2, axis=-1, keepdims=True)
    i2 = _first_lane_where(rest & (el2 == v2), lane)
    e2 = jnp.exp(v2 - v1)
    w1 = g_prob / (1.0 + e2)
    w2 = g_prob * e2 / (1.0 + e2)
    return jnp.where(lane == i1, w1, 0.0) + jnp.where(lane == i2, w2, 0.0)


def _moe_kernel(h_ref, wr_ref, br_ref, wg_ref, wu_ref, wd_ref, g_ref, b_ref, y_ref, gate_ref, acc_ref, hb_ref):
    step = pl.program_id(1)

    @pl.when(step == 0)
    def _():
        h = h_ref[...]
        hb = _bf16(h)
        hb_ref[...] = hb
        h_lo = _bf16(h - hb.astype(jnp.float32))
        hi_terms = _dot(hb, wr_ref[...])
        logits = hi_terms[:, :LANES] + (_dot(h_lo, wr_ref[:, :LANES]) + hi_terms[:, LANES:]) + br_ref[...]
        gate_ref[...] = _route(logits)
        acc_ref[...] = jnp.zeros(acc_ref.shape, jnp.float32)

    hb = hb_ref[...]
    gate = gate_ref[...]
    lane = lax.broadcasted_iota(jnp.int32, gate.shape, 1)
    hids = []
    for j in range(MOE_EXPERTS_PER_STEP):
        e = step * MOE_EXPERTS_PER_STEP + j
        gcol = jnp.sum(jnp.where(lane == e + N_GROUPS, gate, 0.0), axis=-1, keepdims=True)
        a = _dot(hb, wg_ref[j])
        hids.append(_bf16(a * _sigmoid(a) * _dot(hb, wu_ref[j]) * gcol))
    wd = wd_ref[...].reshape(MOE_EXPERTS_PER_STEP * D_EXPERT, D_MODEL)
    acc_ref[...] += _dot(jnp.concatenate(hids, axis=1), wd)

    @pl.when(step == N_EXPERTS // MOE_EXPERTS_PER_STEP - 1)
    def _():
        y_ref[...] = _layer_norm(ALPHA * h_ref[...] + acc_ref[...], g_ref[...], b_ref[...])


def _moe(h2d, wr, br, wg, wu, wd, g, b, tm):
    n = h2d.shape[0]
    tok = lambda i, e: (i, 0)
    whole = lambda i, e: (0, 0)
    per_e = lambda i, e: (e, 0, 0)
    eps = MOE_EXPERTS_PER_STEP
    return pl.pallas_call(
        _moe_kernel,
        grid=(n // tm, N_EXPERTS // eps),
        in_specs=[pl.BlockSpec((tm, D_MODEL), tok),
                  pl.BlockSpec((D_MODEL, 2 * LANES), whole), pl.BlockSpec((1, LANES), whole),
                  pl.BlockSpec((eps, D_MODEL, D_EXPERT), per_e),
                  pl.BlockSpec((eps, D_MODEL, D_EXPERT), per_e),
                  pl.BlockSpec((eps, D_EXPERT, D_MODEL), per_e),
                  pl.BlockSpec((1, D_MODEL), whole), pl.BlockSpec((1, D_MODEL), whole)],
        out_specs=pl.BlockSpec((tm, D_MODEL), tok),
        out_shape=jax.ShapeDtypeStruct((n, D_MODEL), jnp.float32),
        scratch_shapes=[pltpu.VMEM((tm, LANES), jnp.float32), pltpu.VMEM((tm, D_MODEL), jnp.float32),
                        pltpu.VMEM((tm, D_MODEL), MXU_DTYPE)],
        compiler_params=pltpu.CompilerParams(dimension_semantics=("arbitrary", "arbitrary"),
                                             vmem_limit_bytes=VMEM_LIMIT),
        name="moe",
    )(h2d, wr, br, wg, wu, wd, g, b)


def _prep_weights(w_in, w_up, b_up, gla_g, gla_b, w_bg, w_bd, w_o, ln1_g, ln1_b, w_rg, b_rg, w_re, b_re,
                  w_eg, w_eu, w_ed, ln2_g, ln2_b):
    w, wup, bup = _prep_w_in(w_in, w_up, b_up)
    w_re2 = jnp.transpose(w_re, (1, 0, 2)).reshape(D_MODEL, N_EXPERTS)
    rpad = jnp.zeros((D_MODEL, LANES - N_GROUPS - N_EXPERTS), w_rg.dtype)
    wr = jnp.concatenate([w_rg, w_re2, rpad], axis=1)
    wr_hi = _bf16(wr)
    wr = jnp.concatenate([wr_hi, _bf16(wr - wr_hi.astype(jnp.float32))], axis=1)
    br =jnp.concatenate([b_rg, b_re.reshape(N_EXPERTS), jnp.zeros((LANES - N_GROUPS - N_EXPERTS,), b_rg.dtype)])
    row = lambda a: a.reshape(1, -1)
    return dict(w=w, wup=wup, bup=bup, gla_g=row(gla_g), gla_b=row(gla_b), wbg=_bf16(w_bg), wbd=_bf16(w_bd),
                wo=_bf16(w_o), ln1_g=row(ln1_g), ln1_b=row(ln1_b), wr=wr, br=row(br), weg=_bf16(w_eg),
                weu=_bf16(w_eu), wed=_bf16(w_ed), ln2_g=row(ln2_g), ln2_b=row(ln2_b))


MOE_TOKEN_TILE = 1024
MOE_EXPERTS_PER_STEP = 4


def _token_tile(n):
    for tm in (512, 256, 128):
        if n % tm == 0:
            return tm
    raise ValueError(f"token count {n} is not a multiple of 128")


def _trunk_layer(x, s0, past_k, past_v, past_ki, p):
    B, T, _ = x.shape
    past = past_k.shape[1]
    n = B * T
    tm = _token_tile(n)
    assert T % tm == 0 or tm % T == 0
    x2d = x.reshape(n, D_MODEL)
    (gq, gk, gv, sgg, lf, dq, dk, dv, iq, sga, sgb, ki, kb, vb, kib, wt) = _in_proj(
        x2d, p["w"], p["wup"], p["bup"], T, past, tm)
    seq = lambda a: a.reshape(B, T, a.shape[-1])

    o_gla, s_fin = _gla(seq(gq), seq(gk), seq(lf), seq(gv), seq(sgg), s0, p["gla_g"], p["gla_b"])

    total = past + T
    tk = DSA_KEY_TILE
    lp = -(-total // tk) * tk

    def keys(past_rows, new_rows):
        rows = jnp.concatenate([_bf16(past_rows.reshape(B, past, new_rows.shape[-1])), seq(new_rows)], axis=1)
        return jnp.pad(rows, ((0, 0), (0, lp - total), (0, 0)))

    nkt = lp // tk
    vt = jnp.transpose(keys(past_v, vb).reshape(B, nkt, tk, _DKV), (0, 1, 3, 2))
    vt = vt.reshape(B, nkt, DSA_KV_HEADS, DSA_HEAD_DIM, tk)
    ones = jnp.ones((B, nkt, DSA_KV_HEADS, 1, tk), jnp.bfloat16)
    zpad = jnp.zeros((B, nkt, DSA_KV_HEADS, _VT_ROWS - DSA_HEAD_DIM - 1, tk), jnp.bfloat16)
    vt = jnp.concatenate([vt, ones, zpad], axis=3)

    tq = 256 if T % 256 == 0 else LANES
    tp = -(-T // tq) * tq
    qpad = lambda a: jnp.pad(a, ((0, 0), (0, tp - T), (0, 0)))
    wt = jnp.pad(wt.reshape(IDX_HEADS, B, T), ((0, 0), (0, 0), (0, tp - T))).reshape(IDX_HEADS, B * tp)
    o_dsa = _dsa(qpad(seq(dq)), qpad(seq(iq)), wt, keys(past_k, kb), vt, keys(past_ki, kib),
                 past, T, tq, tk)[:, :T]

    h = _merge(o_gla.reshape(n, _GV), o_dsa.reshape(n, _DQ), sga, sgb, x2d, p["wbg"], p["wbd"], p["wo"],
               p["ln1_g"], p["ln1_b"], tm)
    tm_moe = MOE_TOKEN_TILE if n % MOE_TOKEN_TILE == 0 else tm
    y = _moe(h, p["wr"], p["br"], p["weg"], p["weu"], p["wed"], p["ln2_g"], p["ln2_b"], tm_moe)
    cache_rows = lambda a: a.reshape(B, T, DSA_KV_HEADS, DSA_HEAD_DIM)
    return y.reshape(B, T, D_MODEL), s_fin, cache_rows(dk), cache_rows(dv), seq(ki)


def kernel(x_prompt, x_sample, state_gla, cache_k, cache_v, cache_k_idx, w_in, w_gla_gate_up, b_gla_gate,
           gla_norm_g, gla_norm_b, w_branch_gla, w_branch_dsa, w_out, ln1_g, ln1_b, w_router_group,
           b_router_group, w_router_expert, b_router_expert, w_expert_gate, w_expert_up, w_expert_down,
           ln2_g, ln2_b):
    B = x_prompt.shape[0]
    l = 0
    p = _prep_weights(w_in[l], w_gla_gate_up[l], b_gla_gate[l], gla_norm_g[l], gla_norm_b[l], w_branch_gla[l],
                      w_branch_dsa[l], w_out[l], ln1_g[l], ln1_b[l], w_router_group[l], b_router_group[l],
                      w_router_expert[l], b_router_expert[l], w_expert_gate[l], w_expert_up[l],
                      w_expert_down[l], ln2_g[l], ln2_b[l])
    s0 = jnp.zeros((B, GLA_HEADS, GLA_DK, GLA_DV), jnp.float32)
    ek = jnp.zeros((B, 0, DSA_KV_HEADS, DSA_HEAD_DIM), x_prompt.dtype)
    eki = jnp.zeros((B, 0, IDX_DIM), x_prompt.dtype)
    yp, sp, kp, vp, kip = _trunk_layer(x_prompt, s0, ek, ek, eki, p)
    ys, ss, ksn, vsn, kisn = _trunk_layer(x_sample, state_gla[l], cache_k[l], cache_v[l], cache_k_idx[l], p)
    stack = lambda a: a[None]
    return (yp, ys, stack(sp), stack(kp), stack(vp), stack(kip), stack(ss), stack(ksn), stack(vsn), stack(kisn))
```
